```python
import math
import jax, jax.numpy as jnp
from jax import lax
import numpy as np

D_MODEL = 2048
BATCH = 2
SEQ = 4096
DEPTH = 1
DEC_BATCH = 128
DEC_SEQ = 8
PAST_LEN = 8192
PAGE_SIZE = 128

N_META = 16
MIX_WIDTH = D_MODEL
ATT_WIDTH = MIX_WIDTH // 2
HEAD_DIM = 64
N_HEADS = ATT_WIDTH // HEAD_DIM
N_KV_HEADS = 4
GQA_GROUP = N_HEADS // N_KV_HEADS
KV_WIDTH = N_KV_HEADS * HEAD_DIM
WINDOW = 128
ATT_BLOCK = WINDOW
N_BUCKETS = 32
MAX_DISTANCE = 128
GLA_WIDTH = MIX_WIDTH - ATT_WIDTH
GLA_HEADS = 4
GLA_DV = GLA_WIDTH // GLA_HEADS
GLA_DK = GLA_DV // 2
GLA_KEY_WIDTH = GLA_HEADS * GLA_DK
GLA_RANK = 16
GLA_NORMALIZER = 16.0
GLA_CHUNK = 64
FRONT_PAD = ATT_BLOCK - N_META
EPS = 1e-6

IN_SIZES = (ATT_WIDTH, KV_WIDTH, KV_WIDTH, ATT_WIDTH,
            GLA_KEY_WIDTH, GLA_KEY_WIDTH, GLA_WIDTH, GLA_WIDTH, GLA_RANK)
IN_WIDTH = sum(IN_SIZES)
SPLIT_POINTS = tuple(int(c) for c in np.cumsum(IN_SIZES)[:-1])

kernel_name = 'hymba_swa_sink_gla_decode_step'


def rmsnorm(x, w):
    xf = x.astype(jnp.float32)
    var = jnp.mean(xf * xf, axis=-1, keepdims=True)
    return (xf * lax.rsqrt(var + EPS) * w.astype(jnp.float32)).astype(x.dtype)


def t5_bucket(dist):
    max_exact = N_BUCKETS // 2
    d = jnp.maximum(dist, 0)
    ratio = jnp.maximum(d, max_exact).astype(jnp.float32) / max_exact
    large = max_exact + (jnp.log(ratio) / math.log(MAX_DISTANCE / max_exact)
                         * (N_BUCKETS - max_exact)).astype(jnp.int32)
    large = jnp.minimum(large, N_BUCKETS - 1)
    return jnp.where(d < max_exact, d, large)


def sink_softmax(s, mask, sink):
    s = jnp.where(mask, s, -jnp.inf)
    m = jnp.maximum(jnp.max(s, axis=-1, keepdims=True), sink)
    e = jnp.exp(s - m)
    return e / (jnp.sum(e, axis=-1, keepdims=True) + jnp.exp(sink - m))


def project(h, w_in, w_a2, b_a):
    q_a, k_a, v_a, gate_a, q_g, k_g, v_g, gate_g, z_g = jnp.split(h @ w_in, SPLIT_POINTS, axis=-1)
    lead = h.shape[:-1]
    q_a = q_a.reshape(*lead, N_KV_HEADS, GQA_GROUP, HEAD_DIM)
    k_a = k_a.reshape(*lead, N_KV_HEADS, HEAD_DIM)
    v_a = v_a.reshape(*lead, N_KV_HEADS, HEAD_DIM)
    q_g = q_g.reshape(*lead, GLA_HEADS, GLA_DK) * (GLA_DK ** -0.5)
    k_g = k_g.reshape(*lead, GLA_HEADS, GLA_DK)
    v_g = v_g.reshape(*lead, GLA_HEADS, GLA_DV)
    log_f = jax.nn.log_sigmoid((z_g @ w_a2 + b_a).astype(jnp.float32)) / GLA_NORMALIZER
    log_f = log_f.reshape(*lead, GLA_HEADS, GLA_DK)
    return q_a, k_a, v_a, gate_a, q_g, k_g, v_g, gate_g, log_f


def merge(att_o, gate_a, gla_o, gate_g, gla_norm, w_out):
    lead = att_o.shape[:-1]
    gla_o = rmsnorm(gla_o, gla_norm).reshape(*lead, GLA_WIDTH).astype(gate_g.dtype)
    mixed = jnp.concatenate([att_o * jax.nn.silu(gate_a), gla_o * jax.nn.silu(gate_g)], axis=-1)
    return mixed @ w_out


def swa_prompt(q, k, v, sinks, rel_bias):
    B, Lp = q.shape[:2]
    nb = Lp // ATT_BLOCK
    qb = q.reshape(B, nb, ATT_BLOCK, N_KV_HEADS, GQA_GROUP, HEAD_DIM)
    kb = k.reshape(B, nb, ATT_BLOCK, N_KV_HEADS, HEAD_DIM)
    vb = v.reshape(B, nb, ATT_BLOCK, N_KV_HEADS, HEAD_DIM)

    def with_prev(t):
        prev = jnp.concatenate([jnp.zeros_like(t[:, :1]), t[:, :-1]], axis=1)
        return jnp.concatenate([prev, t], axis=2)

    kk, vv = with_prev(kb), with_prev(vb)
    qpos = (jnp.arange(Lp) - FRONT_PAD).reshape(nb, ATT_BLOCK)
    kpos = (jnp.arange(nb)[:, None] - 1) * ATT_BLOCK + jnp.arange(2 * ATT_BLOCK)[None, :] - FRONT_PAD
    dist = qpos[:, :, None] - kpos[:, None, :]
    mask = (dist >= 0) & (dist < WINDOW) & (kpos[:, None, :] >= 0)
    bias = rel_bias[t5_bucket(dist)].astype(jnp.float32)
    bias = bias.reshape(nb, ATT_BLOCK, 2 * ATT_BLOCK, N_KV_HEADS, GQA_GROUP).transpose(0, 3, 4, 1, 2)
    s = jnp.einsum('bnqhgd,bnshd->bnhgqs', qb, kk).astype(jnp.float32) * (HEAD_DIM ** -0.5) + bias
    p = sink_softmax(s, mask[:, None, None], sinks.astype(jnp.float32).reshape(N_KV_HEADS, GQA_GROUP, 1, 1))
    o = jnp.einsum('bnhgqs,bnshd->bnqhgd', p.astype(v.dtype), vv)
    return o.reshape(B, Lp, ATT_WIDTH)


def swa_sample(q, k, v, k_buf, v_buf, sinks, rel_bias):
    DB, T = q.shape[:2]
    W = k_buf.shape[1]
    kk = jnp.concatenate([k_buf, k], axis=1)
    vv = jnp.concatenate([v_buf, v], axis=1)
    qpos = PAST_LEN + jnp.arange(T)
    kpos = PAST_LEN - W + jnp.arange(W + T)
    dist = qpos[:, None] - kpos[None, :]
    mask = (dist >= 0) & (dist < WINDOW)
    bias = rel_bias[t5_bucket(dist)].astype(jnp.float32)
    bias = bias.reshape(T, W + T, N_KV_HEADS, GQA_GROUP).transpose(2, 3, 0, 1)
    s = jnp.einsum('bqhgd,bshd->bhgqs', q, kk).astype(jnp.float32) * (HEAD_DIM ** -0.5) + bias
    p = sink_softmax(s, mask, sinks.astype(jnp.float32).reshape(N_KV_HEADS, GQA_GROUP, 1, 1))
    o = jnp.einsum('bhgqs,bshd->bqhgd', p.astype(v.dtype), vv)
    return o.reshape(DB, T, ATT_WIDTH), kk[:, -W:], vv[:, -W:]


def gla_chunk(q, k, v, g, s0):
    C = q.shape[2]
    b = jnp.cumsum(g, axis=2)
    causal = jnp.tril(jnp.ones((C, C), dtype=bool))
    diff = b[:, :, :, None, :] - b[:, :, None, :, :]
    decay = jnp.exp(jnp.where(causal[:, :, None], diff, -jnp.inf))
    scores = jnp.einsum('bhtk,bhtsk,bhsk->bhts', q, decay, k)
    o = (jnp.einsum('bhts,bhsv->bhtv', scores, v)
         + jnp.einsum('bhtk,bhkv->bhtv', q * jnp.exp(b), s0))
    b_last = b[:, :, -1:, :]
    s_new = (jnp.exp(b_last[:, :, 0, :, None]) * s0
             + jnp.einsum('bhsk,bhsv->bhkv', k * jnp.exp(b_last - b), v))
    return o, s_new


def gla_prompt(q, k, v, log_f):
    B, Lp = q.shape[:2]
    nc = Lp // GLA_CHUNK
    valid = (jnp.arange(Lp) >= FRONT_PAD)[None, :, None, None]
    k = jnp.where(valid, k, 0)
    log_f = jnp.where(valid, log_f, 0.0)

    def to_chunks(t):
        return t.astype(jnp.float32).reshape(B, nc, GLA_CHUNK, GLA_HEADS, -1).transpose(1, 0, 3, 2, 4)

    def step(s, inp):
        qc, kc, vc, gc = inp
        o, s = gla_chunk(qc, kc, vc, gc, s)
        return s, o

    s0 = jnp.zeros((B, GLA_HEADS, GLA_DK, GLA_DV), jnp.float32)
    s_fin, o = lax.scan(step, s0, (to_chunks(q), to_chunks(k), to_chunks(v), to_chunks(log_f)))
    o = o.transpose(1, 0, 3, 2, 4).reshape(B, Lp, GLA_HEADS, GLA_DV)
    return o, s_fin


def setup_inputs(seed: int = 0) -> dict:
    key = jax.random.key(seed)
    ks = jax.random.split(key, 16)
    w_cache = min(WINDOW, PAST_LEN)
    nrm = jax.random.normal
    return {
        'x_prompt': nrm(ks[0], (BATCH, SEQ, D_MODEL), jnp.float32),
        'x_sample': nrm(ks[1], (DEC_BATCH, DEC_SEQ, D_MODEL), jnp.float32),
        'cache_k_win': nrm(ks[2], (DEPTH, DEC_BATCH, w_cache, N_KV_HEADS, HEAD_DIM), jnp.float32),
        'cache_v_win': nrm(ks[3], (DEPTH, DEC_BATCH, w_cache, N_KV_HEADS, HEAD_DIM), jnp.float32),
        'state_gla': 0.1 * nrm(ks[4], (DEPTH, DEC_BATCH, GLA_HEADS, GLA_DK, GLA_DV), jnp.float32),
        'meta_tokens': nrm(ks[5], (N_META, D_MODEL), jnp.float32),
        'rel_bias': 0.5 * nrm(ks[6], (N_BUCKETS, N_HEADS), jnp.float32),
        'norm_pre': 1.0 + 0.1 * nrm(ks[7], (DEPTH, D_MODEL), jnp.float32),
        'norm_post': 1.0 + 0.1 * nrm(ks[8], (DEPTH, D_MODEL), jnp.float32),
        'w_in': nrm(ks[9], (DEPTH, D_MODEL, IN_WIDTH), jnp.float32) * D_MODEL ** -0.5,
        'w_a2': nrm(ks[10], (DEPTH, GLA_RANK, GLA_KEY_WIDTH), jnp.float32) * GLA_RANK ** -0.5,
        'b_a': 0.1 * nrm(ks[11], (DEPTH, GLA_KEY_WIDTH), jnp.float32),
        'attn_sinks': nrm(ks[12], (DEPTH, N_HEADS), jnp.float32),
        'gla_norm': 1.0 + 0.1 * nrm(ks[13], (DEPTH, GLA_DV), jnp.float32),
        'w_out': nrm(ks[14], (DEPTH, MIX_WIDTH, D_MODEL), jnp.float32) * MIX_WIDTH ** -0.5,
    }


def reference(x_prompt, x_sample, cache_k_win, cache_v_win, state_gla, meta_tokens, rel_bias,
              norm_pre, norm_post, w_in, w_a2, b_a, attn_sinks, gla_norm, w_out):
    B = x_prompt.shape[0]
    meta = jnp.broadcast_to(meta_tokens[None].astype(x_prompt.dtype), (B, N_META, D_MODEL))
    xp = jnp.concatenate([meta, x_prompt], axis=1)
    xs = x_sample
    kwp, vwp, sgp, kws, vws, sgs = [], [], [], [], [], []
    for l in range(DEPTH):
        h = rmsnorm(xp, norm_pre[l])
        h = jnp.pad(h, ((0, 0), (FRONT_PAD, 0), (0, 0)))
        q_a, k_a, v_a, gate_a, q_g, k_g, v_g, gate_g, log_f = project(h, w_in[l], w_a2[l], b_a[l])
        att_o = swa_prompt(q_a, k_a, v_a, attn_sinks[l], rel_bias)
        gla_o, s_fin = gla_prompt(q_g, k_g, v_g, log_f)
        o = merge(att_o, gate_a, gla_o, gate_g, gla_norm[l], w_out[l])[:, FRONT_PAD:]
        xp = xp + rmsnorm(o, norm_post[l])
        kwp.append(k_a[:, -WINDOW:])
        vwp.append(v_a[:, -WINDOW:])
        sgp.append(s_fin)
        h = rmsnorm(xs, norm_pre[l])
        q_a, k_a, v_a, gate_a, q_g, k_g, v_g, gate_g, log_f = project(h, w_in[l], w_a2[l], b_a[l])
        att_o, k_new, v_new = swa_sample(q_a, k_a, v_a, cache_k_win[l], cache_v_win[l], attn_sinks[l], rel_bias)
        to_bhtd = lambda t: t.astype(jnp.float32).transpose(0, 2, 1, 3)
        gla_o, s_new = gla_chunk(to_bhtd(q_g), to_bhtd(k_g), to_bhtd(v_g), to_bhtd(log_f),
                                 state_gla[l].astype(jnp.float32))
        gla_o = gla_o.transpose(0, 2, 1, 3)
        o = merge(att_o, gate_a, gla_o, gate_g, gla_norm[l], w_out[l])
        xs = xs + rmsnorm(o, norm_post[l])
        kws.append(k_new)
        vws.append(v_new)
        sgs.append(s_new)
    y_prompt = xp[:, N_META:]
    y_sample = xs
    return (y_prompt, y_sample, jnp.stack(kwp), jnp.stack(vwp), jnp.stack(sgp),
            jnp.stack(kws), jnp.stack(vws), jnp.stack(sgs))
```

```python
import functools
import math

import jax
import jax.numpy as jnp
from jax import lax
from jax.experimental import pallas as pl
from jax.experimental.pallas import tpu as pltpu

F32 = jnp.float32
BF16 = jnp.bfloat16

D_MODEL = 2048
N_META = 16
ATT_WIDTH = 1024
HEAD_DIM = 64
N_HEADS = 16
N_KV_HEADS = 4
GQA_GROUP = 4
KV_WIDTH = 256
WINDOW = 128
N_BUCKETS = 32
MAX_DISTANCE = 128
GLA_WIDTH = 1024
GLA_HEADS = 4
GLA_DV = 256
GLA_DK = 128
GLA_KEY_WIDTH = 512
GLA_RANK = 16
GLA_NORMALIZER = 16.0
GLA_CHUNK = 64
FRONT_PAD = WINDOW - N_META
EPS = 1e-6
PAST_LEN = 8192

IN_SIZES = (ATT_WIDTH, KV_WIDTH, KV_WIDTH, ATT_WIDTH,
            GLA_KEY_WIDTH, GLA_KEY_WIDTH, GLA_WIDTH, GLA_WIDTH, GLA_RANK)

N_MAIN = 5632
TN = 512
COL_QA = 0
COL_GATE_A = 1024
COL_QG = 2048
COL_KG = 2560
COL_VG = 3072
COL_GATE_G = 4096
COL_KA = 5120
COL_VA = 5376
LANES = 128
NEG = -1e30

VMEM_LIMIT = 48 * 1024 * 1024


def _sigmoid(x):
    return 1.0 / (1.0 + jnp.exp(-x))


def _log_sigmoid(x):
    return jnp.minimum(x, 0.0) - jnp.log1p(jnp.exp(-jnp.abs(x)))


def _nt_dot(a, b):
    return lax.dot_general(a, b, (((1,), (1,)), ((), ())), preferred_element_type=F32)


def _tn_dot(a, b):
    return lax.dot_general(a, b, (((0,), (0,)), ((), ())), preferred_element_type=F32)


def _in_proj_kernel(x_ref, nw_ref, w_ref, wz_ref, wa2_ref, ba_ref,
                    p_ref, kv_ref, lf_ref, hn_ref, *, nj):
    j = pl.program_id(1)

    @pl.when(j == 0)
    def _():
        x = x_ref[...]
        var = jnp.mean(x * x, axis=-1, keepdims=True)
        hn = (x * lax.rsqrt(var + EPS) * nw_ref[...]).astype(BF16)
        hn_ref[...] = hn
        z = jnp.dot(hn, wz_ref[...], preferred_element_type=F32)
        logits = jnp.dot(z.astype(BF16), wa2_ref[...], preferred_element_type=F32) + ba_ref[...]
        lf_ref[...] = _log_sigmoid(logits) * (1.0 / GLA_NORMALIZER)

    acc = jnp.dot(hn_ref[...], w_ref[...], preferred_element_type=F32)
    p_ref[...] = acc.astype(BF16)

    @pl.when(j == nj - 1)
    def _():
        kv_ref[...] = acc


def _in_proj(x, norm_w, w_main, w_z, w_a2p, b_a, tm):
    m = x.shape[0]
    nj = N_MAIN // TN
    assert m % tm == 0 and COL_KA == (nj - 1) * TN
    return pl.pallas_call(
        functools.partial(_in_proj_kernel, nj=nj),
        grid=(m // tm, nj),
        in_specs=[
            pl.BlockSpec((tm, D_MODEL), lambda i, j: (i, 0)),
            pl.BlockSpec((1, D_MODEL), lambda i, j: (0, 0)),
            pl.BlockSpec((D_MODEL, TN), lambda i, j: (0, j)),
            pl.BlockSpec((D_MODEL, LANES), lambda i, j: (0, 0)),
            pl.BlockSpec((LANES, GLA_KEY_WIDTH), lambda i, j: (0, 0)),
            pl.BlockSpec((1, GLA_KEY_WIDTH), lambda i, j: (0, 0)),
        ],
        out_specs=[
            pl.BlockSpec((tm, TN), lambda i, j: (i, j)),
            pl.BlockSpec((tm, 2 * KV_WIDTH), lambda i, j: (i, 0)),
            pl.BlockSpec((tm, GLA_KEY_WIDTH), lambda i, j: (i, 0)),
        ],
        out_shape=[
            jax.ShapeDtypeStruct((m, N_MAIN), BF16),
            jax.ShapeDtypeStruct((m, 2 * KV_WIDTH), F32),
            jax.ShapeDtypeStruct((m, GLA_KEY_WIDTH), F32),
        ],
        scratch_shapes=[pltpu.VMEM((tm, D_MODEL), BF16)],
        compiler_params=pltpu.CompilerParams(
            dimension_semantics=("arbitrary", "arbitrary"), vmem_limit_bytes=VMEM_LIMIT),
        name="in_proj",
    )(x, norm_w, w_main, w_z, w_a2p, b_a)


def _t5_bucket(dist):
    max_exact = N_BUCKETS // 2
    d = jnp.maximum(dist, 0)
    ratio = jnp.maximum(d, max_exact).astype(F32) / max_exact
    large = max_exact + (jnp.log(ratio) / math.log(MAX_DISTANCE / max_exact)
                         * (N_BUCKETS - max_exact)).astype(jnp.int32)
    large = jnp.minimum(large, N_BUCKETS - 1)
    return jnp.where(d < max_exact, d, large)


def _swa_prompt_kernel(rb_ref, sink_ref, q_ref, gate_ref, kp_ref, kc_ref, vp_ref, vc_ref,
                       o_ref, tab_ref):
    b = pl.program_id(0)
    n = pl.program_id(1)

    @pl.when((b == 0) & (n == 0))
    def _build_bias_table():
        qi = lax.broadcasted_iota(jnp.int32, (WINDOW, 2 * WINDOW), 0)
        kj = lax.broadcasted_iota(jnp.int32, (WINDOW, 2 * WINDOW), 1)
        dist = qi - kj + WINDOW
        valid = (dist >= 0) & (dist < WINDOW)
        bucket = _t5_bucket(dist)
        for hd in range(N_HEADS):
            tab_ref[hd] = jnp.full((WINDOW, 2 * WINDOW), NEG, F32)

        def body(bk, carry):
            hit = (bucket == bk) & valid
            for hd in range(N_HEADS):
                tab_ref[hd] = jnp.where(hit, rb_ref[bk, hd], tab_ref[hd])
            return carry

        lax.fori_loop(0, N_BUCKETS, body, 0)

    kk = jnp.concatenate([kp_ref[...], kc_ref[...]], axis=0)
    vv = jnp.concatenate([vp_ref[...], vc_ref[...]], axis=0)
    kj = lax.broadcasted_iota(jnp.int32, (1, 2 * WINDOW), 1)
    kadd = jnp.where((n - 1) * WINDOW + kj - FRONT_PAD >= 0, 0.0, NEG)
    lane_head = lax.shift_right_logical(lax.broadcasted_iota(jnp.int32, (1, KV_WIDTH), 1), 6)
    scale = HEAD_DIM ** -0.5
    for g in range(GQA_GROUP):
        qg = q_ref[:, g * KV_WIDTH:(g + 1) * KV_WIDTH].astype(F32)
        acc = jnp.zeros((WINDOW, KV_WIDTH), F32)
        for h in range(N_KV_HEADS):
            hd = h * GQA_GROUP + g
            qm = jnp.where(lane_head == h, qg, 0.0).astype(BF16)
            s = _nt_dot(qm, kk) * scale + tab_ref[hd] + kadd
            sink = sink_ref[hd]
            m = jnp.maximum(jnp.max(s, axis=-1, keepdims=True), sink)
            e = jnp.exp(s - m)
            den = jnp.sum(e, axis=-1, keepdims=True) + jnp.exp(sink - m)
            p = (e * (1.0 / den)).astype(BF16)
            o = jnp.dot(p, vv, preferred_element_type=F32)
            acc = jnp.where(lane_head == h, o, acc)
        gate = gate_ref[:, g * KV_WIDTH:(g + 1) * KV_WIDTH].astype(F32)
        o_ref[:, g * KV_WIDTH:(g + 1) * KV_WIDTH] = (acc * (gate * _sigmoid(gate))).astype(BF16)


def _swa_prompt(p_all, rel_bias, sinks, n_batch, n_blocks):
    def row(b, n):
        return b * n_blocks + n

    def prev(b, n):
        return b * n_blocks + jnp.maximum(n - 1, 0)

    kblk = COL_KA // KV_WIDTH
    vblk = COL_VA // KV_WIDTH
    out_rows = n_batch * (n_blocks - 1) * WINDOW
    return pl.pallas_call(
        _swa_prompt_kernel,
        grid=(n_batch, n_blocks),
        in_specs=[
            pl.BlockSpec(memory_space=pltpu.SMEM),
            pl.BlockSpec(memory_space=pltpu.SMEM),
            pl.BlockSpec((WINDOW, ATT_WIDTH), lambda b, n: (row(b, n), COL_QA // ATT_WIDTH)),
            pl.BlockSpec((WINDOW, ATT_WIDTH), lambda b, n: (row(b, n), COL_GATE_A // ATT_WIDTH)),
            pl.BlockSpec((WINDOW, KV_WIDTH), lambda b, n: (prev(b, n), kblk)),
            pl.BlockSpec((WINDOW, KV_WIDTH), lambda b, n: (row(b, n), kblk)),
            pl.BlockSpec((WINDOW, KV_WIDTH), lambda b, n: (prev(b, n), vblk)),
            pl.BlockSpec((WINDOW, KV_WIDTH), lambda b, n: (row(b, n), vblk)),
        ],
        out_specs=pl.BlockSpec(
            (WINDOW, ATT_WIDTH), lambda b, n: (b * (n_blocks - 1) + jnp.maximum(n - 1, 0), 0)),
        out_shape=jax.ShapeDtypeStruct((out_rows, ATT_WIDTH), BF16),
        scratch_shapes=[pltpu.VMEM((N_HEADS, WINDOW, 2 * WINDOW), F32)],
        compiler_params=pltpu.CompilerParams(
            dimension_semantics=("arbitrary", "arbitrary"), vmem_limit_bytes=VMEM_LIMIT),
        name="swa_prompt",
    )(rel_bias, sinks, p_all, p_all, p_all, p_all, p_all, p_all)


def _diag_scores(q3, k3, b3, width):
    nb = q3.shape[0]
    n_heads = q3.shape[2] // GLA_DK
    r_io = lax.broadcasted_iota(jnp.int32, (nb, 8, width), 0)
    t_io = lax.broadcasted_iota(jnp.int32, (nb, 8, width), 1)
    l_io = lax.broadcasted_iota(jnp.int32, (nb, 8, width), 2)
    s_of_lane = l_io - 8 * r_io
    outs = [jnp.zeros((nb, 8, width), F32) for _ in range(n_heads)]
    for s in range(8):
        bs = jnp.broadcast_to(b3[:, s:s + 1, :], b3.shape)
        ks = jnp.broadcast_to(k3[:, s:s + 1, :], k3.shape)
        e = jnp.exp(jnp.minimum(b3 - bs, 0.0))
        p = q3 * e * ks
        for h in range(n_heads):
            col = jnp.sum(p[:, :, h * GLA_DK:(h + 1) * GLA_DK], axis=-1, keepdims=True)
            outs[h] = jnp.where(s_of_lane == s, col, outs[h])
    causal = (s_of_lane >= 0) & (s_of_lane <= t_io) & (s_of_lane < 8)
    return [jnp.where(causal, o, 0.0) for o in outs]


def _gla_finish(o, gn, gate):
    var = jnp.mean(o * o, axis=-1, keepdims=True)
    on = o * lax.rsqrt(var + EPS) * gn
    return (on * (gate * _sigmoid(gate))).astype(BF16)


def _decay_column(row):
    eye = (lax.broadcasted_iota(jnp.int32, (GLA_DK, GLA_DK), 0)
           == lax.broadcasted_iota(jnp.int32, (GLA_DK, GLA_DK), 1))
    return jnp.sum(jnp.where(eye, jnp.broadcast_to(row, (GLA_DK, GLA_DK)), 0.0), axis=1, keepdims=True)


def _gla_prompt_kernel(qg_ref, kg_ref, vg_ref, gate_ref, lf_ref, gn_ref, o_ref, s_ref):
    c = pl.program_id(0)
    n_batch = qg_ref.shape[0]
    C = GLA_CHUNK

    @pl.when(c == 0)
    def _():
        s_ref[...] = jnp.zeros(s_ref.shape, F32)

    row = lax.broadcasted_iota(jnp.int32, (C, 1), 0)
    valid = (c * C + row) >= FRONT_PAD
    tril = (lax.broadcasted_iota(jnp.int32, (C, C), 0)
            >= lax.broadcasted_iota(jnp.int32, (C, C), 1)).astype(F32).astype(BF16)
    t_io = lax.broadcasted_iota(jnp.int32, (C, C), 0)
    s_io = lax.broadcasted_iota(jnp.int32, (C, C), 1)
    gn = gn_ref[...]

    for b in range(n_batch):
        q = qg_ref[b].astype(F32) * (GLA_DK ** -0.5)
        k = jnp.where(valid, kg_ref[b].astype(F32), 0.0)
        g = jnp.where(valid, lf_ref[b], 0.0)
        g1 = g.astype(BF16)
        r1 = g - g1.astype(F32)
        g2 = r1.astype(BF16)
        g3 = (r1 - g2.astype(F32)).astype(BF16)
        bc = (jnp.dot(tril, g1, preferred_element_type=F32)
              + jnp.dot(tril, g2, preferred_element_type=F32)
              + jnp.dot(tril, g3, preferred_element_type=F32))
        b_last = bc[C - 1:C, :]
        qt = (q * jnp.exp(bc)).astype(BF16)
        kt = (k * jnp.exp(b_last - bc)).astype(BF16)

        levels = []
        for half in (32, 16, 8):
            nblk = C // (2 * half)
            ref = jnp.concatenate(
                [jnp.broadcast_to(bc[i * 2 * half + half - 1:i * 2 * half + half, :], (2 * half, bc.shape[1]))
                 for i in range(nblk)], axis=0)
            shift = int(math.log2(half))
            upper = (lax.shift_right_logical(row, shift) & 1) == 1
            e = jnp.exp(jnp.where(upper, bc - ref, ref - bc))
            ql = jnp.where(upper, q * e, 0.0).astype(BF16)
            kl = jnp.where(upper, 0.0, k * e).astype(BF16)
            same = lax.shift_right_logical(t_io, shift + 1) == lax.shift_right_logical(s_io, shift + 1)
            levels.append((ql, kl, same))

        diag = _diag_scores(q.reshape(C // 8, 8, -1), k.reshape(C // 8, 8, -1), bc.reshape(C // 8, 8, -1), C)

        for h in range(GLA_HEADS):
            ks = slice(h * GLA_DK, (h + 1) * GLA_DK)
            vs = slice(h * GLA_DV, (h + 1) * GLA_DV)
            a = diag[h].reshape(C, C)
            for ql, kl, same in levels:
                a = a + jnp.where(same, _nt_dot(ql[:, ks], kl[:, ks]), 0.0)
            v = vg_ref[b, :, vs]
            st = s_ref[b, h]
            o = (jnp.dot(a.astype(BF16), v, preferred_element_type=F32)
                 + jnp.dot(qt[:, ks], st.astype(BF16), preferred_element_type=F32))
            s_ref[b, h] = _decay_column(jnp.exp(b_last[:, ks])) * st + _tn_dot(kt[:, ks], v)
            o_ref[b, :, vs] = _gla_finish(o, gn, gate_ref[b, :, vs].astype(F32))


def _gla_prompt(p3, lf3, gla_norm_row, n_chunks):
    n_batch = p3.shape[0]
    skip = (FRONT_PAD + N_META) // GLA_CHUNK
    out_len = (n_chunks - skip) * GLA_CHUNK
    C = GLA_CHUNK
    return pl.pallas_call(
        _gla_prompt_kernel,
        grid=(n_chunks,),
        in_specs=[
            pl.BlockSpec((n_batch, C, GLA_KEY_WIDTH), lambda c: (0, c, COL_QG // GLA_KEY_WIDTH)),
            pl.BlockSpec((n_batch, C, GLA_KEY_WIDTH), lambda c: (0, c, COL_KG // GLA_KEY_WIDTH)),
            pl.BlockSpec((n_batch, C, GLA_WIDTH), lambda c: (0, c, COL_VG // GLA_WIDTH)),
            pl.BlockSpec((n_batch, C, GLA_WIDTH), lambda c: (0, c, COL_GATE_G // GLA_WIDTH)),
            pl.BlockSpec((n_batch, C, GLA_KEY_WIDTH), lambda c: (0, c, 0)),
            pl.BlockSpec((1, GLA_DV), lambda c: (0, 0)),
        ],
        out_specs=[
            pl.BlockSpec((n_batch, C, GLA_WIDTH), lambda c: (0, jnp.maximum(c - skip, 0), 0)),
            pl.BlockSpec((n_batch, GLA_HEADS, GLA_DK, GLA_DV), lambda c: (0, 0, 0, 0)),
        ],
        out_shape=[
            jax.ShapeDtypeStruct((n_batch, out_len, GLA_WIDTH), BF16),
            jax.ShapeDtypeStruct((n_batch, GLA_HEADS, GLA_DK, GLA_DV), F32),
        ],
        compiler_params=pltpu.CompilerParams(
            dimension_semantics=("arbitrary",), vmem_limit_bytes=VMEM_LIMIT),
        name="gla_prompt",
    )(p3, p3, p3, p3, lf3, gla_norm_row)


def _out_proj_kernel(ma_ref, mg_ref, wa_ref, wg_ref, x_ref, nw_ref, y_ref):
    o = (jnp.dot(ma_ref[...], wa_ref[...], preferred_element_type=F32)
         + jnp.dot(mg_ref[...], wg_ref[...], preferred_element_type=F32))
    var = jnp.mean(o * o, axis=-1, keepdims=True)
    y_ref[...] = x_ref[...] + o * lax.rsqrt(var + EPS) * nw_ref[...]


def _out_proj(mixed_a, mixed_g, w_out_a, w_out_g, x, norm_w, tm):
    m = x.shape[0]
    assert m % tm == 0
    return pl.pallas_call(
        _out_proj_kernel,
        grid=(m // tm,),
        in_specs=[
            pl.BlockSpec((tm, ATT_WIDTH), lambda i: (i, 0)),
            pl.BlockSpec((tm, GLA_WIDTH), lambda i: (i, 0)),
            pl.BlockSpec((ATT_WIDTH, D_MODEL), lambda i: (0, 0)),
            pl.BlockSpec((GLA_WIDTH, D_MODEL), lambda i: (0, 0)),
            pl.BlockSpec((tm, D_MODEL), lambda i: (i, 0)),
            pl.BlockSpec((1, D_MODEL), lambda i: (0, 0)),
        ],
        out_specs=pl.BlockSpec((tm, D_MODEL), lambda i: (i, 0)),
        out_shape=jax.ShapeDtypeStruct((m, D_MODEL), F32),
        compiler_params=pltpu.CompilerParams(
            dimension_semantics=("arbitrary",), vmem_limit_bytes=VMEM_LIMIT),
        name="out_proj",
    )(mixed_a, mixed_g, w_out_a, w_out_g, x, norm_w)


SWA_SAMPLE_SEQS = 8


def _swa_sample_kernel(rb_ref, sink_ref, q_ref, gate_ref, kvn_ref, ck_ref, cv_ref,
                       o_ref, ko_ref, vo_ref, tabc_ref, tabn_ref, sinkcol_ref, *, t_new):
    step = pl.program_id(0)
    W = ck_ref.shape[1]
    rows = N_HEADS * t_new

    @pl.when(step == 0)
    def _build_tables():
        r = lax.broadcasted_iota(jnp.int32, (rows, W), 0)
        qi = r & (t_new - 1)
        kj = lax.broadcasted_iota(jnp.int32, (rows, W), 1)
        dist_c = qi + W - kj
        valid_c = (dist_c >= 0) & (dist_c < WINDOW)
        bucket_c = _t5_bucket(dist_c)
        rn = lax.broadcasted_iota(jnp.int32, (rows, t_new), 0)
        qn = rn & (t_new - 1)
        jn = lax.broadcasted_iota(jnp.int32, (rows, t_new), 1)
        dist_n = qn - jn
        valid_n = (dist_n >= 0) & (dist_n < WINDOW)
        bucket_n = _t5_bucket(dist_n)
        head_c = lax.shift_right_logical(r, int(math.log2(t_new)))
        head_n = lax.shift_right_logical(rn, int(math.log2(t_new)))
        head_s = lax.shift_right_logical(lax.broadcasted_iota(jnp.int32, (rows, 1), 0), int(math.log2(t_new)))
        tabc_ref[...] = jnp.full((rows, W), NEG, F32)
        tabn_ref[...] = jnp.full((rows, t_new), NEG, F32)
        sinkcol_ref[...] = jnp.zeros((rows, 1), F32)
        for g in range(GQA_GROUP):
            for h in range(N_KV_HEADS):
                hd = h * GQA_GROUP + g
                blk = g * N_KV_HEADS + h
                sinkcol_ref[...] = jnp.where(head_s == blk, sink_ref[hd], sinkcol_ref[...])

        def body(bk, carry):
            for g in range(GQA_GROUP):
                for h in range(N_KV_HEADS):
                    hd = h * GQA_GROUP + g
                    blk = g * N_KV_HEADS + h
                    val = rb_ref[bk, hd]
                    tabc_ref[...] = jnp.where((bucket_c == bk) & valid_c & (head_c == blk), val, tabc_ref[...])
                    tabn_ref[...] = jnp.where((bucket_n == bk) & valid_n & (head_n == blk), val, tabn_ref[...])
            return carry

        lax.fori_loop(0, N_BUCKETS, body, 0)

    lane_head = lax.shift_right_logical(lax.broadcasted_iota(jnp.int32, (1, KV_WIDTH), 1), 6)
    scale = HEAD_DIM ** -0.5
    sink = sinkcol_ref[...]
    for s in range(q_ref.shape[0]):
        q = q_ref[s].astype(F32)
        pieces = []
        for g in range(GQA_GROUP):
            qg = q[:, g * KV_WIDTH:(g + 1) * KV_WIDTH]
            for h in range(N_KV_HEADS):
                pieces.append(jnp.where(lane_head == h, qg, 0.0))
        lhs = jnp.concatenate(pieces, axis=0).astype(BF16)
        kc = ck_ref[s]
        vc = cv_ref[s]
        kn = kvn_ref[s, :, :KV_WIDTH]
        vn = kvn_ref[s, :, KV_WIDTH:]
        sc = _nt_dot(lhs, kc.astype(BF16)) * scale + tabc_ref[...]
        sn = _nt_dot(lhs, kn.astype(BF16)) * scale + tabn_ref[...]
        m = jnp.maximum(jnp.maximum(jnp.max(sc, axis=-1, keepdims=True),
                                    jnp.max(sn, axis=-1, keepdims=True)), sink)
        ec = jnp.exp(sc - m)
        en = jnp.exp(sn - m)
        den = (jnp.sum(ec, axis=-1, keepdims=True) + jnp.sum(en, axis=-1, keepdims=True)
               + jnp.exp(sink - m))
        inv = 1.0 / den
        o = (jnp.dot((ec * inv).astype(BF16), vc.astype(BF16), preferred_element_type=F32)
             + jnp.dot((en * inv).astype(BF16), vn.astype(BF16), preferred_element_type=F32))
        outs = []
        for g in range(GQA_GROUP):
            acc = jnp.zeros((t_new, KV_WIDTH), F32)
            for h in range(N_KV_HEADS):
                blk = g * N_KV_HEADS + h
                acc = jnp.where(lane_head == h, o[blk * t_new:(blk + 1) * t_new, :], acc)
            outs.append(acc)
        att = jnp.concatenate(outs, axis=1)
        gate = gate_ref[s].astype(F32)
        o_ref[s] = (att * (gate * _sigmoid(gate))).astype(BF16)
        ko_ref[s, :W - t_new, :] = kc[t_new:, :]
        ko_ref[s, W - t_new:, :] = kn
        vo_ref[s, :W - t_new, :] = vc[t_new:, :]
        vo_ref[s, W - t_new:, :] = vn


def _swa_sample(p3, kv3, cache_k, cache_v, rel_bias, sinks):
    db, t_new, _ = p3.shape
    w = cache_k.shape[1]
    assert t_new & (t_new - 1) == 0 and w >= t_new and w <= WINDOW
    sb = SWA_SAMPLE_SEQS
    rows = N_HEADS * t_new
    return pl.pallas_call(
        functools.partial(_swa_sample_kernel, t_new=t_new),
        grid=(db // sb,),
        in_specs=[
            pl.BlockSpec(memory_space=pltpu.SMEM),
            pl.BlockSpec(memory_space=pltpu.SMEM),
            pl.BlockSpec((sb, t_new, ATT_WIDTH), lambda i: (i, 0, COL_QA // ATT_WIDTH)),
            pl.BlockSpec((sb, t_new, ATT_WIDTH), lambda i: (i, 0, COL_GATE_A // ATT_WIDTH)),
            pl.BlockSpec((sb, t_new, 2 * KV_WIDTH), lambda i: (i, 0, 0)),
            pl.BlockSpec((sb, w, KV_WIDTH), lambda i: (i, 0, 0)),
            pl.BlockSpec((sb, w, KV_WIDTH), lambda i: (i, 0, 0)),
        ],
        out_specs=[
            pl.BlockSpec((sb, t_new, ATT_WIDTH), lambda i: (i, 0, 0)),
            pl.BlockSpec((sb, w, KV_WIDTH), lambda i: (i, 0, 0)),
            pl.BlockSpec((sb, w, KV_WIDTH), lambda i: (i, 0, 0)),
        ],
        out_shape=[
            jax.ShapeDtypeStruct((db, t_new, ATT_WIDTH), BF16),
            jax.ShapeDtypeStruct((db, w, KV_WIDTH), F32),
            jax.ShapeDtypeStruct((db, w, KV_WIDTH), F32),
        ],
        scratch_shapes=[
            pltpu.VMEM((rows, w), F32),
            pltpu.VMEM((rows, t_new), F32),
            pltpu.VMEM((rows, 1), F32),
        ],
        compiler_params=pltpu.CompilerParams(
            dimension_semantics=("arbitrary",), vmem_limit_bytes=VMEM_LIMIT),
        name="swa_sample",
    )(rel_bias, sinks, p3, p3, kv3, cache_k, cache_v)


GLA_SAMPLE_SEQS = 4


def _cumsum_rows8(g):
    row = lax.broadcasted_iota(jnp.int32, g.shape, 0)
    out = g
    for sh in (1, 2, 4):
        out = out + jnp.where(row >= sh, pltpu.roll(out, sh, axis=0), 0.0)
    return out


def _gla_sample_kernel(qg_ref, kg_ref, vg_ref, gate_ref, lf_ref, gn_ref, st_ref, o_ref, so_ref):
    T = qg_ref.shape[1]
    gn = gn_ref[...]
    t_col = lax.broadcasted_iota(jnp.int32, (T, 1), 0)
    zpad = jnp.zeros((8, GLA_DK), F32)
    for s in range(qg_ref.shape[0]):
        q = qg_ref[s].astype(F32) * (GLA_DK ** -0.5)
        k = kg_ref[s].astype(F32)
        bc = _cumsum_rows8(lf_ref[s])
        b_last = bc[T - 1:T, :]
        qt = q * jnp.exp(bc)
        kt = k * jnp.exp(b_last - bc)
        v_all = vg_ref[s].astype(F32)
        cols = [[None] * T for _ in range(GLA_HEADS)]
        for j in range(T):
            bs = jnp.broadcast_to(bc[j:j + 1, :], bc.shape)
            ks_ = jnp.broadcast_to(k[j:j + 1, :], k.shape)
            p = q * jnp.exp(jnp.minimum(bc - bs, 0.0)) * ks_
            for h in range(GLA_HEADS):
                col = jnp.sum(p[:, h * GLA_DK:(h + 1) * GLA_DK], axis=-1, keepdims=True)
                cols[h][j] = jnp.where(t_col >= j, col, 0.0)
        for h in range(GLA_HEADS):
            ks = slice(h * GLA_DK, (h + 1) * GLA_DK)
            vs = slice(h * GLA_DV, (h + 1) * GLA_DV)
            v = v_all[:, vs]
            st = st_ref[s, h]
            qt16 = jnp.concatenate([qt[:, ks], zpad], axis=0).astype(BF16)
            o = jnp.dot(qt16, st.astype(BF16), preferred_element_type=F32)[:T, :]
            for j in range(T):
                o = o + cols[h][j] * v[j:j + 1, :]
            kt16 = jnp.concatenate([kt[:, ks], zpad], axis=0).astype(BF16)
            v16 = jnp.concatenate([v, jnp.zeros((8, GLA_DV), F32)], axis=0).astype(BF16)
            so_ref[s, h] = _decay_column(jnp.exp(b_last[:, ks])) * st + _tn_dot(kt16, v16)
            o_ref[s, :, vs] = _gla_finish(o, gn, gate_ref[s, :, vs].astype(F32))


def _gla_sample(p3, lf3, state, gla_norm_row):
    db, t_new, _ = p3.shape
    assert t_new == 8
    sb = GLA_SAMPLE_SEQS
    return pl.pallas_call(
        _gla_sample_kernel,
        grid=(db // sb,),
        in_specs=[
            pl.BlockSpec((sb, t_new, GLA_KEY_WIDTH), lambda i: (i, 0, COL_QG // GLA_KEY_WIDTH)),
            pl.BlockSpec((sb, t_new, GLA_KEY_WIDTH), lambda i: (i, 0, COL_KG // GLA_KEY_WIDTH)),
            pl.BlockSpec((sb, t_new, GLA_WIDTH), lambda i: (i, 0, COL_VG // GLA_WIDTH)),
            pl.BlockSpec((sb, t_new, GLA_WIDTH), lambda i: (i, 0, COL_GATE_G // GLA_WIDTH)),
            pl.BlockSpec((sb, t_new, GLA_KEY_WIDTH), lambda i: (i, 0, 0)),
            pl.BlockSpec((1, GLA_DV), lambda i: (0, 0)),
            pl.BlockSpec((sb, GLA_HEADS, GLA_DK, GLA_DV), lambda i: (i, 0, 0, 0)),
        ],
        out_specs=[
            pl.BlockSpec((sb, t_new, GLA_WIDTH), lambda i: (i, 0, 0)),
            pl.BlockSpec((sb, GLA_HEADS, GLA_DK, GLA_DV), lambda i: (i, 0, 0, 0)),
        ],
        out_shape=[
            jax.ShapeDtypeStruct((db, t_new, GLA_WIDTH), BF16),
            jax.ShapeDtypeStruct((db, GLA_HEADS, GLA_DK, GLA_DV), F32),
        ],
        compiler_params=pltpu.CompilerParams(
            dimension_semantics=("arbitrary",), vmem_limit_bytes=VMEM_LIMIT),
        name="gla_sample",
    )(p3, p3, p3, p3, lf3, gla_norm_row, state)


def _group_major(w, axis):
    shape = w.shape
    if axis == 1:
        w4 = w.reshape(shape[0], N_KV_HEADS, GQA_GROUP, HEAD_DIM)
        return w4.transpose(0, 2, 1, 3).reshape(shape)
    w4 = w.reshape(N_KV_HEADS, GQA_GROUP, HEAD_DIM, shape[1])
    return w4.transpose(1, 0, 2, 3).reshape(shape)


def _prep_weights(w_in_l, w_a2_l, w_out_l):
    offs = [0]
    for sz in IN_SIZES:
        offs.append(offs[-1] + sz)
    parts = [w_in_l[:, offs[i]:offs[i + 1]] for i in range(len(IN_SIZES))]
    q_a, k_a, v_a, gate_a, q_g, k_g, v_g, gate_g, z_g = parts
    w_main = jnp.concatenate(
        [_group_major(q_a, 1), _group_major(gate_a, 1), q_g, k_g, v_g, gate_g, k_a, v_a], axis=1).astype(BF16)
    w_z = jnp.pad(z_g, ((0, 0), (0, LANES - GLA_RANK))).astype(BF16)
    w_a2p = jnp.pad(w_a2_l, ((0, LANES - GLA_RANK), (0, 0))).astype(BF16)
    w_out_a = _group_major(w_out_l[:ATT_WIDTH], 0).astype(BF16)
    w_out_g = w_out_l[ATT_WIDTH:].astype(BF16)
    return w_main, w_z, w_a2p, w_out_a, w_out_g


def kernel(x_prompt, x_sample, cache_k_win, cache_v_win, state_gla, meta_tokens, rel_bias,
           norm_pre, norm_post, w_in, w_a2, b_a, attn_sinks, gla_norm, w_out):
    n_batch, seq, _ = x_prompt.shape
    db, t_new, _ = x_sample.shape
    depth = w_in.shape[0]
    assert depth == 1
    l = 0
    lp = seq + WINDOW
    n_blocks = lp // WINDOW
    n_chunks = lp // GLA_CHUNK
    w_cache = cache_k_win.shape[2]

    w_main, w_z, w_a2p, w_out_a, w_out_g = _prep_weights(w_in[l], w_a2[l], w_out[l])
    npre = norm_pre[l][None, :]
    npost = norm_post[l][None, :]
    ba = b_a[l][None, :]
    gn = gla_norm[l][None, :]
    sinks = attn_sinks[l]

    meta = jnp.broadcast_to(meta_tokens[None].astype(x_prompt.dtype), (n_batch, N_META, D_MODEL))
    xp_full = jnp.concatenate(
        [jnp.zeros((n_batch, FRONT_PAD, D_MODEL), x_prompt.dtype), meta, x_prompt], axis=1)
    p_all, kv_all, lf_all = _in_proj(xp_full.reshape(n_batch * lp, D_MODEL), npre, w_main, w_z, w_a2p, ba, tm=768)
    mixed_a = _swa_prompt(p_all, rel_bias, sinks, n_batch, n_blocks)
    mixed_g, s_fin = _gla_prompt(p_all.reshape(n_batch, lp, N_MAIN),
                                 lf_all.reshape(n_batch, lp, GLA_KEY_WIDTH), gn, n_chunks)
    y_prompt = _out_proj(mixed_a, mixed_g.reshape(n_batch * seq, GLA_WIDTH), w_out_a, w_out_g,
                         x_prompt.reshape(n_batch * seq, D_MODEL), npost, tm=256)
    y_prompt = y_prompt.reshape(n_batch, seq, D_MODEL)
    kv_p = kv_all.reshape(n_batch, lp, 2 * KV_WIDTH)[:, -WINDOW:]
    k_win_p = kv_p[..., :KV_WIDTH].reshape(1, n_batch, WINDOW, N_KV_HEADS, HEAD_DIM)
    v_win_p = kv_p[..., KV_WIDTH:].reshape(1, n_batch, WINDOW, N_KV_HEADS, HEAD_DIM)
    gla_p = s_fin[None]

    ps, kvs, lfs = _in_proj(x_sample.reshape(db * t_new, D_MODEL), npre, w_main, w_z, w_a2p, ba, tm=512)
    ps3 = ps.reshape(db, t_new, N_MAIN)
    mixed_as, k_win_s, v_win_s = _swa_sample(
        ps3, kvs.reshape(db, t_new, 2 * KV_WIDTH),
        cache_k_win[l].reshape(db, w_cache, KV_WIDTH), cache_v_win[l].reshape(db, w_cache, KV_WIDTH),
        rel_bias, sinks)
    mixed_gs, s_new = _gla_sample(ps3, lfs.reshape(db, t_new, GLA_KEY_WIDTH), state_gla[l], gn)
    y_sample = _out_proj(mixed_as.reshape(db * t_new, ATT_WIDTH), mixed_gs.reshape(db * t_new, GLA_WIDTH),
                         w_out_a, w_out_g, x_sample.reshape(db * t_new, D_MODEL), npost, tm=256)
    y_sample = y_sample.reshape(db, t_new, D_MODEL)
    k_win_s = k_win_s.reshape(1, db, w_cache, N_KV_HEADS, HEAD_DIM)
    v_win_s = v_win_s.reshape(1, db, w_cache, N_KV_HEADS, HEAD_DIM)

    return (y_prompt, y_sample, k_win_p, v_win_p, gla_p, k_win_s, v_win_s, s_new[None])
```

```python
import functools
import math

import jax
import jax.numpy as jnp
from jax import lax
from jax.experimental import pallas as pl
from jax.experimental.pallas import tpu as pltpu

F32 = jnp.float32
BF16 = jnp.bfloat16

D_MODEL = 2048
N_META = 16
ATT_WIDTH = 1024
HEAD_DIM = 64
N_HEADS = 16
N_KV_HEADS = 4
GQA_GROUP = 4
KV_WIDTH = 256
WINDOW = 128
N_BUCKETS = 32
MAX_DISTANCE = 128
GLA_WIDTH = 1024
GLA_HEADS = 4
GLA_DV = 256
GLA_DK = 128
GLA_KEY_WIDTH = 512
GLA_RANK = 16
GLA_NORMALIZER = 16.0
GLA_CHUNK = 64
FRONT_PAD = WINDOW - N_META
EPS = 1e-6
PAST_LEN = 8192

IN_SIZES = (ATT_WIDTH, KV_WIDTH, KV_WIDTH, ATT_WIDTH,
            GLA_KEY_WIDTH, GLA_KEY_WIDTH, GLA_WIDTH, GLA_WIDTH, GLA_RANK)

N_MAIN = 5632
TN = 512
COL_QA = 0
COL_GATE_A = 1024
COL_QG = 2048
COL_KG = 2560
COL_VG = 3072
COL_GATE_G = 4096
COL_KA = 5120
COL_VA = 5376
LANES = 128
NEG = -1e30

VMEM_LIMIT = 48 * 1024 * 1024


def _sigmoid(x):
    return 1.0 / (1.0 + jnp.exp(-x))


def _log_sigmoid(x):
    return jnp.minimum(x, 0.0) - jnp.log1p(jnp.exp(-jnp.abs(x)))


def _nt_dot(a, b):
    return lax.dot_general(a, b, (((1,), (1,)), ((), ())), preferred_element_type=F32)


def _tn_dot(a, b):
    return lax.dot_general(a, b, (((0,), (0,)), ((), ())), preferred_element_type=F32)


def _in_proj_kernel(x_ref, nw_ref, w_ref, wz_ref, wa2_ref, ba_ref,
                    p_ref, kv_ref, lf_ref, hn_ref, *, nj):
    j = pl.program_id(1)

    @pl.when(j == 0)
    def _():
        x = x_ref[...]
        var = jnp.mean(x * x, axis=-1, keepdims=True)
        hn = (x * lax.rsqrt(var + EPS) * nw_ref[...]).astype(BF16)
        hn_ref[...] = hn
        z = jnp.dot(hn, wz_ref[...], preferred_element_type=F32)
        logits = jnp.dot(z.astype(BF16), wa2_ref[...], preferred_element_type=F32) + ba_ref[...]
        lf_ref[...] = _log_sigmoid(logits) * (1.0 / GLA_NORMALIZER)

    acc = jnp.dot(hn_ref[...], w_ref[...], preferred_element_type=F32)
    p_ref[...] = acc.astype(BF16)

    @pl.when(j == nj - 1)
    def _():
        kv_ref[...] = acc


def _in_proj(x, norm_w, w_main, w_z, w_a2p, b_a, tm):
    m = x.shape[0]
    nj = N_MAIN // TN
    assert m % tm == 0 and COL_KA == (nj - 1) * TN
    return pl.pallas_call(
        functools.partial(_in_proj_kernel, nj=nj),
        grid=(m // tm, nj),
        in_specs=[
            pl.BlockSpec((tm, D_MODEL), lambda i, j: (i, 0)),
            pl.BlockSpec((1, D_MODEL), lambda i, j: (0, 0)),
            pl.BlockSpec((D_MODEL, TN), lambda i, j: (0, j)),
            pl.BlockSpec((D_MODEL, LANES), lambda i, j: (0, 0)),
            pl.BlockSpec((LANES, GLA_KEY_WIDTH), lambda i, j: (0, 0)),
            pl.BlockSpec((1, GLA_KEY_WIDTH), lambda i, j: (0, 0)),
        ],
        out_specs=[
            pl.BlockSpec((tm, TN), lambda i, j: (i, j)),
            pl.BlockSpec((tm, 2 * KV_WIDTH), lambda i, j: (i, 0)),
            pl.BlockSpec((tm, GLA_KEY_WIDTH), lambda i, j: (i, 0)),
        ],
        out_shape=[
            jax.ShapeDtypeStruct((m, N_MAIN), BF16),
            jax.ShapeDtypeStruct((m, 2 * KV_WIDTH), F32),
            jax.ShapeDtypeStruct((m, GLA_KEY_WIDTH), F32),
        ],
        scratch_shapes=[pltpu.VMEM((tm, D_MODEL), BF16)],
        compiler_params=pltpu.CompilerParams(
            dimension_semantics=("arbitrary", "arbitrary"), vmem_limit_bytes=VMEM_LIMIT),
        name="in_proj",
    )(x, norm_w, w_main, w_z, w_a2p, b_a)


def _t5_bucket(dist):
    max_exact = N_BUCKETS // 2
    d = jnp.maximum(dist, 0)
    ratio = jnp.maximum(d, max_exact).astype(F32) / max_exact
    large = max_exact + (jnp.log(ratio) / math.log(MAX_DISTANCE / max_exact)
                         * (N_BUCKETS - max_exact)).astype(jnp.int32)
    large = jnp.minimum(large, N_BUCKETS - 1)
    return jnp.where(d < max_exact, d, large)


def _swa_prompt_kernel(rb_ref, sink_ref, q_ref, gate_ref, kp_ref, kc_ref, vp_ref, vc_ref,
                       o_ref, tab_ref):
    b = pl.program_id(0)
    n = pl.program_id(1)
    W2 = 2 * WINDOW
    NL = N_KV_HEADS * W2

    @pl.when((b == 0) & (n == 0))
    def _build_bias_tables():
        qi = lax.broadcasted_iota(jnp.int32, (WINDOW, W2), 0)
        kj = lax.broadcasted_iota(jnp.int32, (WINDOW, W2), 1)
        dist = qi - kj + WINDOW
        valid = (dist >= 0) & (dist < WINDOW)
        bucket = _t5_bucket(dist)
        for g in range(GQA_GROUP):
            tab_ref[2, g] = jnp.full((WINDOW, NL), NEG, F32)

        def body(bk, carry):
            hit = (bucket == bk) & valid
            for g in range(GQA_GROUP):
                for h in range(N_KV_HEADS):
                    cur = tab_ref[2, g, :, h * W2:(h + 1) * W2]
                    tab_ref[2, g, :, h * W2:(h + 1) * W2] = jnp.where(hit, rb_ref[bk, h * GQA_GROUP + g], cur)
            return carry

        lax.fori_loop(0, N_BUCKETS, body, 0)
        for v in range(2):
            kvalid = (v - 1) * WINDOW + kj - FRONT_PAD >= 0
            for g in range(GQA_GROUP):
                for h in range(N_KV_HEADS):
                    tab_ref[v, g, :, h * W2:(h + 1) * W2] = jnp.where(
                        kvalid, tab_ref[2, g, :, h * W2:(h + 1) * W2], NEG)

    kk = jnp.concatenate([kp_ref[...], kc_ref[...]], axis=0)
    vv = jnp.concatenate([vp_ref[...], vc_ref[...]], axis=0)
    lane_head = lax.shift_right_logical(lax.broadcasted_iota(jnp.int32, (1, KV_WIDTH), 1), 6)
    zero = jnp.zeros((), BF16)
    kkm = jnp.concatenate([jnp.where(lane_head == h, kk, zero) for h in range(N_KV_HEADS)], axis=0)
    vvm = jnp.concatenate([jnp.where(lane_head == h, vv, zero) for h in range(N_KV_HEADS)], axis=0)
    tv = jnp.minimum(n, 2)
    scale = HEAD_DIM ** -0.5
    scores = []
    for g in range(GQA_GROUP):
        qg = (q_ref[:, g * KV_WIDTH:(g + 1) * KV_WIDTH].astype(F32) * scale).astype(BF16)
        scores.append(_nt_dot(qg, kkm) + tab_ref[tv, g])
    for g in range(GQA_GROUP):
        probs = []
        for h in range(N_KV_HEADS):
            s = scores[g][:, h * W2:(h + 1) * W2]
            sink = sink_ref[h * GQA_GROUP + g]
            m = jnp.maximum(jnp.max(s, axis=-1, keepdims=True), sink)
            e = jnp.exp(s - m)
            den = jnp.sum(e, axis=-1, keepdims=True) + jnp.exp(sink - m)
            probs.append((e * (1.0 / den)).astype(BF16))
        p = jnp.concatenate(probs, axis=1)
        o = jnp.dot(p, vvm, preferred_element_type=F32)
        gate = gate_ref[:, g * KV_WIDTH:(g + 1) * KV_WIDTH].astype(F32)
        o_ref[:, g * KV_WIDTH:(g + 1) * KV_WIDTH] = (o * (gate * _sigmoid(gate))).astype(BF16)


def _swa_prompt(p_all, rel_bias, sinks, n_batch, n_blocks):
    def row(b, n):
        return b * n_blocks + n

    def prev(b, n):
        return b * n_blocks + jnp.maximum(n - 1, 0)

    kblk = COL_KA // KV_WIDTH
    vblk = COL_VA // KV_WIDTH
    out_rows = n_batch * (n_blocks - 1) * WINDOW
    return pl.pallas_call(
        _swa_prompt_kernel,
        grid=(n_batch, n_blocks),
        in_specs=[
            pl.BlockSpec(memory_space=pltpu.SMEM),
            pl.BlockSpec(memory_space=pltpu.SMEM),
            pl.BlockSpec((WINDOW, ATT_WIDTH), lambda b, n: (row(b, n), COL_QA // ATT_WIDTH)),
            pl.BlockSpec((WINDOW, ATT_WIDTH), lambda b, n: (row(b, n), COL_GATE_A // ATT_WIDTH)),
            pl.BlockSpec((WINDOW, KV_WIDTH), lambda b, n: (prev(b, n), kblk)),
            pl.BlockSpec((WINDOW, KV_WIDTH), lambda b, n: (row(b, n), kblk)),
            pl.BlockSpec((WINDOW, KV_WIDTH), lambda b, n: (prev(b, n), vblk)),
            pl.BlockSpec((WINDOW, KV_WIDTH), lambda b, n: (row(b, n), vblk)),
        ],
        out_specs=pl.BlockSpec(
            (WINDOW, ATT_WIDTH), lambda b, n: (b * (n_blocks - 1) + jnp.maximum(n - 1, 0), 0)),
        out_shape=jax.ShapeDtypeStruct((out_rows, ATT_WIDTH), BF16),
        scratch_shapes=[pltpu.VMEM((3, GQA_GROUP, WINDOW, N_KV_HEADS * 2 * WINDOW), F32)],
        compiler_params=pltpu.CompilerParams(
            dimension_semantics=("arbitrary", "arbitrary"), vmem_limit_bytes=VMEM_LIMIT),
        name="swa_prompt",
    )(rel_bias, sinks, p_all, p_all, p_all, p_all, p_all, p_all)


def _diag_scores(q3, k3, b3, width):
    nb = q3.shape[0]
    n_heads = q3.shape[2] // GLA_DK
    r_io = lax.broadcasted_iota(jnp.int32, (nb, 8, width), 0)
    t_io = lax.broadcasted_iota(jnp.int32, (nb, 8, width), 1)
    l_io = lax.broadcasted_iota(jnp.int32, (nb, 8, width), 2)
    s_of_lane = l_io - 8 * r_io
    outs = [jnp.zeros((nb, 8, width), F32) for _ in range(n_heads)]
    for s in range(8):
        bs = jnp.broadcast_to(b3[:, s:s + 1, :], b3.shape)
        ks = jnp.broadcast_to(k3[:, s:s + 1, :], k3.shape)
        e = jnp.exp(jnp.minimum(b3 - bs, 0.0))
        p = q3 * e * ks
        for h in range(n_heads):
            col = jnp.sum(p[:, :, h * GLA_DK:(h + 1) * GLA_DK], axis=-1, keepdims=True)
            outs[h] = jnp.where(s_of_lane == s, col, outs[h])
    causal = (s_of_lane >= 0) & (s_of_lane <= t_io) & (s_of_lane < 8)
    return [jnp.where(causal, o, 0.0) for o in outs]


def _gla_finish(o, gn, gate):
    var = jnp.mean(o * o, axis=-1, keepdims=True)
    on = o * lax.rsqrt(var + EPS) * gn
    return (on * (gate * _sigmoid(gate))).astype(BF16)


def _decay_column(row):
    eye = (lax.broadcasted_iota(jnp.int32, (GLA_DK, GLA_DK), 0)
           == lax.broadcasted_iota(jnp.int32, (GLA_DK, GLA_DK), 1))
    return jnp.sum(jnp.where(eye, jnp.broadcast_to(row, (GLA_DK, GLA_DK)), 0.0), axis=1, keepdims=True)


def _gla_prompt_kernel(qg_ref, kg_ref, vg_ref, gate_ref, lf_ref, gn_ref, o_ref, s_ref):
    c = pl.program_id(0)
    n_batch = qg_ref.shape[0]
    C = GLA_CHUNK

    @pl.when(c == 0)
    def _():
        s_ref[...] = jnp.zeros(s_ref.shape, F32)

    row = lax.broadcasted_iota(jnp.int32, (C, 1), 0)
    valid = (c * C + row) >= FRONT_PAD
    tril = (lax.broadcasted_iota(jnp.int32, (C, C), 0)
            >= lax.broadcasted_iota(jnp.int32, (C, C), 1)).astype(F32).astype(BF16)
    t_io = lax.broadcasted_iota(jnp.int32, (C, C), 0)
    s_io = lax.broadcasted_iota(jnp.int32, (C, C), 1)
    gn = gn_ref[...]

    for b in range(n_batch):
        q = qg_ref[b].astype(F32) * (GLA_DK ** -0.5)
        k = jnp.where(valid, kg_ref[b].astype(F32), 0.0)
        g = jnp.where(valid, lf_ref[b], 0.0)
        g1 = g.astype(BF16)
        r1 = g - g1.astype(F32)
        g2 = r1.astype(BF16)
        g3 = (r1 - g2.astype(F32)).astype(BF16)
        bc = (jnp.dot(tril, g1, preferred_element_type=F32)
              + jnp.dot(tril, g2, preferred_element_type=F32)
              + jnp.dot(tril, g3, preferred_element_type=F32))
        b_last = bc[C - 1:C, :]
        qt = (q * jnp.exp(bc)).astype(BF16)
        kt = (k * jnp.exp(b_last - bc)).astype(BF16)

        levels = []
        for half in (32, 16, 8):
            nblk = C // (2 * half)
            ref = jnp.concatenate(
                [jnp.broadcast_to(bc[i * 2 * half + half - 1:i * 2 * half + half, :], (2 * half, bc.shape[1]))
                 for i in range(nblk)], axis=0)
            shift = int(math.log2(half))
            upper = (lax.shift_right_logical(row, shift) & 1) == 1
            e = jnp.exp(jnp.where(upper, bc - ref, ref - bc))
            ql = jnp.where(upper, q * e, 0.0).astype(BF16)
            kl = jnp.where(upper, 0.0, k * e).astype(BF16)
            same = lax.shift_right_logical(t_io, shift + 1) == lax.shift_right_logical(s_io, shift + 1)
            levels.append((ql, kl, same))

        diag = _diag_scores(q.reshape(C // 8, 8, -1), k.reshape(C // 8, 8, -1), bc.reshape(C // 8, 8, -1), C)

        for h in range(GLA_HEADS):
            ks = slice(h * GLA_DK, (h + 1) * GLA_DK)
            vs = slice(h * GLA_DV, (h + 1) * GLA_DV)
            a = diag[h].reshape(C, C)
            for ql, kl, same in levels:
                a = a + jnp.where(same, _nt_dot(ql[:, ks], kl[:, ks]), 0.0)
            v = vg_ref[b, :, vs]
            st = s_ref[b, h]
            o = (jnp.dot(a.astype(BF16), v, preferred_element_type=F32)
                 + jnp.dot(qt[:, ks], st.astype(BF16), preferred_element_type=F32))
            s_ref[b, h] = _decay_column(jnp.exp(b_last[:, ks])) * st + _tn_dot(kt[:, ks], v)
            o_ref[b, :, vs] = _gla_finish(o, gn, gate_ref[b, :, vs].astype(F32))


def _gla_prompt(p3, lf3, gla_norm_row, n_chunks):
    n_batch = p3.shape[0]
    skip = (FRONT_PAD + N_META) // GLA_CHUNK
    out_len = (n_chunks - skip) * GLA_CHUNK
    C = GLA_CHUNK
    return pl.pallas_call(
        _gla_prompt_kernel,
        grid=(n_chunks,),
        in_specs=[
            pl.BlockSpec((n_batch, C, GLA_KEY_WIDTH), lambda c: (0, c, COL_QG // GLA_KEY_WIDTH)),
            pl.BlockSpec((n_batch, C, GLA_KEY_WIDTH), lambda c: (0, c, COL_KG // GLA_KEY_WIDTH)),
            pl.BlockSpec((n_batch, C, GLA_WIDTH), lambda c: (0, c, COL_VG // GLA_WIDTH)),
            pl.BlockSpec((n_batch, C, GLA_WIDTH), lambda c: (0, c, COL_GATE_G // GLA_WIDTH)),
            pl.BlockSpec((n_batch, C, GLA_KEY_WIDTH), lambda c: (0, c, 0)),
            pl.BlockSpec((1, GLA_DV), lambda c: (0, 0)),
        ],
        out_specs=[
            pl.BlockSpec((n_batch, C, GLA_WIDTH), lambda c: (0, jnp.maximum(c - skip, 0), 0)),
            pl.BlockSpec((n_batch, GLA_HEADS, GLA_DK, GLA_DV), lambda c: (0, 0, 0, 0)),
        ],
        out_shape=[
            jax.ShapeDtypeStruct((n_batch, out_len, GLA_WIDTH), BF16),
            jax.ShapeDtypeStruct((n_batch, GLA_HEADS, GLA_DK, GLA_DV), F32),
        ],
        compiler_params=pltpu.CompilerParams(
            dimension_semantics=("arbitrary",), vmem_limit_bytes=VMEM_LIMIT),
        name="gla_prompt",
    )(p3, p3, p3, p3, lf3, gla_norm_row)


def _out_proj_kernel(ma_ref, mg_ref, wa_ref, wg_ref, x_ref, nw_ref, y_ref):
    o = (jnp.dot(ma_ref[...], wa_ref[...], preferred_element_type=F32)
         + jnp.dot(mg_ref[...], wg_ref[...], preferred_element_type=F32))
    var = jnp.mean(o * o, axis=-1, keepdims=True)
    y_ref[...] = x_ref[...] + o * lax.rsqrt(var + EPS) * nw_ref[...]


def _out_proj(mixed_a, mixed_g, w_out_a, w_out_g, x, norm_w, tm):
    m = x.shape[0]
    assert m % tm == 0
    return pl.pallas_call(
        _out_proj_kernel,
        grid=(m // tm,),
        in_specs=[
            pl.BlockSpec((tm, ATT_WIDTH), lambda i: (i, 0)),
            pl.BlockSpec((tm, GLA_WIDTH), lambda i: (i, 0)),
            pl.BlockSpec((ATT_WIDTH, D_MODEL), lambda i: (0, 0)),
            pl.BlockSpec((GLA_WIDTH, D_MODEL), lambda i: (0, 0)),
            pl.BlockSpec((tm, D_MODEL), lambda i: (i, 0)),
            pl.BlockSpec((1, D_MODEL), lambda i: (0, 0)),
        ],
        out_specs=pl.BlockSpec((tm, D_MODEL), lambda i: (i, 0)),
        out_shape=jax.ShapeDtypeStruct((m, D_MODEL), F32),
        compiler_params=pltpu.CompilerParams(
            dimension_semantics=("arbitrary",), vmem_limit_bytes=VMEM_LIMIT),
        name="out_proj",
    )(mixed_a, mixed_g, w_out_a, w_out_g, x, norm_w)


SWA_SAMPLE_SEQS = 8


def _swa_sample_kernel(rb_ref, sink_ref, q_ref, gate_ref, kvn_ref, ck_ref, cv_ref,
                       o_ref, ko_ref, vo_ref, tabc_ref, tabn_ref, sinkcol_ref, *, t_new):
    step = pl.program_id(0)
    W = ck_ref.shape[1]
    rows = N_HEADS * t_new

    @pl.when(step == 0)
    def _build_tables():
        r = lax.broadcasted_iota(jnp.int32, (rows, W), 0)
        qi = r & (t_new - 1)
        kj = lax.broadcasted_iota(jnp.int32, (rows, W), 1)
        dist_c = qi + W - kj
        valid_c = (dist_c >= 0) & (dist_c < WINDOW)
        bucket_c = _t5_bucket(dist_c)
        rn = lax.broadcasted_iota(jnp.int32, (rows, t_new), 0)
        qn = rn & (t_new - 1)
        jn = lax.broadcasted_iota(jnp.int32, (rows, t_new), 1)
        dist_n = qn - jn
        valid_n = (dist_n >= 0) & (dist_n < WINDOW)
        bucket_n = _t5_bucket(dist_n)
        head_c = lax.shift_right_logical(r, int(math.log2(t_new)))
        head_n = lax.shift_right_logical(rn, int(math.log2(t_new)))
        head_s = lax.shift_right_logical(lax.broadcasted_iota(jnp.int32, (rows, 1), 0), int(math.log2(t_new)))
        tabc_ref[...] = jnp.full((rows, W), NEG, F32)
        tabn_ref[...] = jnp.full((rows, t_new), NEG, F32)
        sinkcol_ref[...] = jnp.zeros((rows, 1), F32)
        for g in range(GQA_GROUP):
            for h in range(N_KV_HEADS):
                hd = h * GQA_GROUP + g
                blk = g * N_KV_HEADS + h
                sinkcol_ref[...] = jnp.where(head_s == blk, sink_ref[hd], sinkcol_ref[...])

        def body(bk, carry):
            for g in range(GQA_GROUP):
                for h in range(N_KV_HEADS):
                    hd = h * GQA_GROUP + g
                    blk = g * N_KV_HEADS + h
                    val = rb_ref[bk, hd]
                    tabc_ref[...] = jnp.where((bucket_c == bk) & valid_c & (head_c == blk), val, tabc_ref[...])
                    tabn_ref[...] = jnp.where((bucket_n == bk) & valid_n & (head_n == blk), val, tabn_ref[...])
            return carry

        lax.fori_loop(0, N_BUCKETS, body, 0)

    lane_head = lax.shift_right_logical(lax.broadcasted_iota(jnp.int32, (1, KV_WIDTH), 1), 6)
    scale = HEAD_DIM ** -0.5
    sink = sinkcol_ref[...]
    for s in range(q_ref.shape[0]):
        q = q_ref[s].astype(F32)
        pieces = []
        for g in range(GQA_GROUP):
            qg = q[:, g * KV_WIDTH:(g + 1) * KV_WIDTH]
            for h in range(N_KV_HEADS):
                pieces.append(jnp.where(lane_head == h, qg, 0.0))
        lhs = jnp.concatenate(pieces, axis=0).astype(BF16)
        kc = ck_ref[s]
        vc = cv_ref[s]
        kn = kvn_ref[s, :, :KV_WIDTH]
        vn = kvn_ref[s, :, KV_WIDTH:]
        sc = _nt_dot(lhs, kc.astype(BF16)) * scale + tabc_ref[...]
        sn = _nt_dot(lhs, kn.astype(BF16)) * scale + tabn_ref[...]
        m = jnp.maximum(jnp.maximum(jnp.max(sc, axis=-1, keepdims=True),
                                    jnp.max(sn, axis=-1, keepdims=True)), sink)
        ec = jnp.exp(sc - m)
        en = jnp.exp(sn - m)
        den = (jnp.sum(ec, axis=-1, keepdims=True) + jnp.sum(en, axis=-1, keepdims=True)
               + jnp.exp(sink - m))
        inv = 1.0 / den
        o = (jnp.dot((ec * inv).astype(BF16), vc.astype(BF16), preferred_element_type=F32)
             + jnp.dot((en * inv).astype(BF16), vn.astype(BF16), preferred_element_type=F32))
        outs = []
        for g in range(GQA_GROUP):
            acc = jnp.zeros((t_new, KV_WIDTH), F32)
            for h in range(N_KV_HEADS):
                blk = g * N_KV_HEADS + h
                acc = jnp.where(lane_head == h, o[blk * t_new:(blk + 1) * t_new, :], acc)
            outs.append(acc)
        att = jnp.concatenate(outs, axis=1)
        gate = gate_ref[s].astype(F32)
        o_ref[s] = (att * (gate * _sigmoid(gate))).astype(BF16)
        ko_ref[s, :W - t_new, :] = kc[t_new:, :]
        ko_ref[s, W - t_new:, :] = kn
        vo_ref[s, :W - t_new, :] = vc[t_new:, :]
        vo_ref[s, W - t_new:, :] = vn


def _swa_sample(p3, kv3, cache_k, cache_v, rel_bias, sinks):
    db, t_new, _ = p3.shape
    w = cache_k.shape[1]
    assert t_new & (t_new - 1) == 0 and w >= t_new and w <= WINDOW
    sb = SWA_SAMPLE_SEQS
    rows = N_HEADS * t_new
    return pl.pallas_call(
        functools.partial(_swa_sample_kernel, t_new=t_new),
        grid=(db // sb,),
        in_specs=[
            pl.BlockSpec(memory_space=pltpu.SMEM),
            pl.BlockSpec(memory_space=pltpu.SMEM),
            pl.BlockSpec((sb, t_new, ATT_WIDTH), lambda i: (i, 0, COL_QA // ATT_WIDTH)),
            pl.BlockSpec((sb, t_new, ATT_WIDTH), lambda i: (i, 0, COL_GATE_A // ATT_WIDTH)),
            pl.BlockSpec((sb, t_new, 2 * KV_WIDTH), lambda i: (i, 0, 0)),
            pl.BlockSpec((sb, w, KV_WIDTH), lambda i: (i, 0, 0)),
            pl.BlockSpec((sb, w, KV_WIDTH), lambda i: (i, 0, 0)),
        ],
        out_specs=[
            pl.BlockSpec((sb, t_new, ATT_WIDTH), lambda i: (i, 0, 0)),
            pl.BlockSpec((sb, w, KV_WIDTH), lambda i: (i, 0, 0)),
            pl.BlockSpec((sb, w, KV_WIDTH), lambda i: (i, 0, 0)),
        ],
        out_shape=[
            jax.ShapeDtypeStruct((db, t_new, ATT_WIDTH), BF16),
            jax.ShapeDtypeStruct((db, w, KV_WIDTH), F32),
            jax.ShapeDtypeStruct((db, w, KV_WIDTH), F32),
        ],
        scratch_shapes=[
            pltpu.VMEM((rows, w), F32),
            pltpu.VMEM((rows, t_new), F32),
            pltpu.VMEM((rows, 1), F32),
        ],
        compiler_params=pltpu.CompilerParams(
            dimension_semantics=("arbitrary",), vmem_limit_bytes=VMEM_LIMIT),
        name="swa_sample",
    )(rel_bias, sinks, p3, p3, kv3, cache_k, cache_v)


GLA_SAMPLE_SEQS = 4


def _cumsum_rows8(g):
    row = lax.broadcasted_iota(jnp.int32, g.shape, 0)
    out = g
    for sh in (1, 2, 4):
        out = out + jnp.where(row >= sh, pltpu.roll(out, sh, axis=0), 0.0)
    return out


def _gla_sample_kernel(qg_ref, kg_ref, vg_ref, gate_ref, lf_ref, gn_ref, st_ref, o_ref, so_ref):
    T = qg_ref.shape[1]
    gn = gn_ref[...]
    t_col = lax.broadcasted_iota(jnp.int32, (T, 1), 0)
    zpad = jnp.zeros((8, GLA_DK), F32)
    for s in range(qg_ref.shape[0]):
        q = qg_ref[s].astype(F32) * (GLA_DK ** -0.5)
        k = kg_ref[s].astype(F32)
        bc = _cumsum_rows8(lf_ref[s])
        b_last = bc[T - 1:T, :]
        qt = q * jnp.exp(bc)
        kt = k * jnp.exp(b_last - bc)
        v_all = vg_ref[s].astype(F32)
        cols = [[None] * T for _ in range(GLA_HEADS)]
        for j in range(T):
            bs = jnp.broadcast_to(bc[j:j + 1, :], bc.shape)
            ks_ = jnp.broadcast_to(k[j:j + 1, :], k.shape)
            p = q * jnp.exp(jnp.minimum(bc - bs, 0.0)) * ks_
            for h in range(GLA_HEADS):
                col = jnp.sum(p[:, h * GLA_DK:(h + 1) * GLA_DK], axis=-1, keepdims=True)
                cols[h][j] = jnp.where(t_col >= j, col, 0.0)
        for h in range(GLA_HEADS):
            ks = slice(h * GLA_DK, (h + 1) * GLA_DK)
            vs = slice(h * GLA_DV, (h + 1) * GLA_DV)
            v = v_all[:, vs]
            st = st_ref[s, h]
            qt16 = jnp.concatenate([qt[:, ks], zpad], axis=0).astype(BF16)
            o = jnp.dot(qt16, st.astype(BF16), preferred_element_type=F32)[:T, :]
            for j in range(T):
                o = o + cols[h][j] * v[j:j + 1, :]
            kt16 = jnp.concatenate([kt[:, ks], zpad], axis=0).astype(BF16)
            v16 = jnp.concatenate([v, jnp.zeros((8, GLA_DV), F32)], axis=0).astype(BF16)
            so_ref[s, h] = _decay_column(jnp.exp(b_last[:, ks])) * st + _tn_dot(kt16, v16)
            o_ref[s, :, vs] = _gla_finish(o, gn, gate_ref[s, :, vs].astype(F32))


def _gla_sample(p3, lf3, state, gla_norm_row):
    db, t_new, _ = p3.shape
    assert t_new == 8
    sb = GLA_SAMPLE_SEQS
    return pl.pallas_call(
        _gla_sample_kernel,
        grid=(db // sb,),
        in_specs=[
            pl.BlockSpec((sb, t_new, GLA_KEY_WIDTH), lambda i: (i, 0, COL_QG // GLA_KEY_WIDTH)),
            pl.BlockSpec((sb, t_new, GLA_KEY_WIDTH), lambda i: (i, 0, COL_KG // GLA_KEY_WIDTH)),
            pl.BlockSpec((sb, t_new, GLA_WIDTH), lambda i: (i, 0, COL_VG // GLA_WIDTH)),
            pl.BlockSpec((sb, t_new, GLA_WIDTH), lambda i: (i, 0, COL_GATE_G // GLA_WIDTH)),
            pl.BlockSpec((sb, t_new, GLA_KEY_WIDTH), lambda i: (i, 0, 0)),
            pl.BlockSpec((1, GLA_DV), lambda i: (0, 0)),
            pl.BlockSpec((sb, GLA_HEADS, GLA_DK, GLA_DV), lambda i: (i, 0, 0, 0)),
        ],
        out_specs=[
            pl.BlockSpec((sb, t_new, GLA_WIDTH), lambda i: (i, 0, 0)),
            pl.BlockSpec((sb, GLA_HEADS, GLA_DK, GLA_DV), lambda i: (i, 0, 0, 0)),
        ],
        out_shape=[
            jax.ShapeDtypeStruct((db, t_new, GLA_WIDTH), BF16),
            jax.ShapeDtypeStruct((db, GLA_HEADS, GLA_DK, GLA_DV), F32),
        ],
        compiler_params=pltpu.CompilerParams(
            dimension_semantics=("arbitrary",), vmem_limit_bytes=VMEM_LIMIT),
        name="gla_sample",
    )(p3, p3, p3, p3, lf3, gla_norm_row, state)


def _group_major(w, axis):
    shape = w.shape
    if axis == 1:
        w4 = w.reshape(shape[0], N_KV_HEADS, GQA_GROUP, HEAD_DIM)
        return w4.transpose(0, 2, 1, 3).reshape(shape)
    w4 = w.reshape(N_KV_HEADS, GQA_GROUP, HEAD_DIM, shape[1])
    return w4.transpose(1, 0, 2, 3).reshape(shape)


def _prep_weights(w_in_l, w_a2_l, w_out_l):
    offs = [0]
    for sz in IN_SIZES:
        offs.append(offs[-1] + sz)
    parts = [w_in_l[:, offs[i]:offs[i + 1]] for i in range(len(IN_SIZES))]
    q_a, k_a, v_a, gate_a, q_g, k_g, v_g, gate_g, z_g = parts
    w_main = jnp.concatenate(
        [_group_major(q_a, 1), _group_major(gate_a, 1), q_g, k_g, v_g, gate_g, k_a, v_a], axis=1).astype(BF16)
    w_z = jnp.pad(z_g, ((0, 0), (0, LANES - GLA_RANK))).astype(BF16)
    w_a2p = jnp.pad(w_a2_l, ((0, LANES - GLA_RANK), (0, 0))).astype(BF16)
    w_out_a = _group_major(w_out_l[:ATT_WIDTH], 0).astype(BF16)
    w_out_g = w_out_l[ATT_WIDTH:].astype(BF16)
    return w_main, w_z, w_a2p, w_out_a, w_out_g


def kernel(x_prompt, x_sample, cache_k_win, cache_v_win, state_gla, meta_tokens, rel_bias,
           norm_pre, norm_post, w_in, w_a2, b_a, attn_sinks, gla_norm, w_out):
    n_batch, seq, _ = x_prompt.shape
    db, t_new, _ = x_sample.shape
    depth = w_in.shape[0]
    assert depth == 1
    l = 0
    lp = seq + WINDOW
    n_blocks = lp // WINDOW
    n_chunks = lp // GLA_CHUNK
    w_cache = cache_k_win.shape[2]

    w_main, w_z, w_a2p, w_out_a, w_out_g = _prep_weights(w_in[l], w_a2[l], w_out[l])
    npre = norm_pre[l][None, :]
    npost = norm_post[l][None, :]
    ba = b_a[l][None, :]
    gn = gla_norm[l][None, :]
    sinks = attn_sinks[l]

    meta = jnp.broadcast_to(meta_tokens[None].astype(x_prompt.dtype), (n_batch, N_META, D_MODEL))
    xp_full = jnp.concatenate(
        [jnp.zeros((n_batch, FRONT_PAD, D_MODEL), x_prompt.dtype), meta, x_prompt], axis=1)
    p_all, kv_all, lf_all = _in_proj(xp_full.reshape(n_batch * lp, D_MODEL), npre, w_main, w_z, w_a2p, ba, tm=768)
    mixed_a = _swa_prompt(p_all, rel_bias, sinks, n_batch, n_blocks)
    mixed_g, s_fin = _gla_prompt(p_all.reshape(n_batch, lp, N_MAIN),
                                 lf_all.reshape(n_batch, lp, GLA_KEY_WIDTH), gn, n_chunks)
    y_prompt = _out_proj(mixed_a, mixed_g.reshape(n_batch * seq, GLA_WIDTH), w_out_a, w_out_g,
                         x_prompt.reshape(n_batch * seq, D_MODEL), npost, tm=256)
    y_prompt = y_prompt.reshape(n_batch, seq, D_MODEL)
    kv_p = kv_all.reshape(n_batch, lp, 2 * KV_WIDTH)[:, -WINDOW:]
    k_win_p = kv_p[..., :KV_WIDTH].reshape(1, n_batch, WINDOW, N_KV_HEADS, HEAD_DIM)
    v_win_p = kv_p[..., KV_WIDTH:].reshape(1, n_batch, WINDOW, N_KV_HEADS, HEAD_DIM)
    gla_p = s_fin[None]

    ps, kvs, lfs = _in_proj(x_sample.reshape(db * t_new, D_MODEL), npre, w_main, w_z, w_a2p, ba, tm=512)
    ps3 = ps.reshape(db, t_new, N_MAIN)
    mixed_as, k_win_s, v_win_s = _swa_sample(
        ps3, kvs.reshape(db, t_new, 2 * KV_WIDTH),
        cache_k_win[l].reshape(db, w_cache, KV_WIDTH), cache_v_win[l].reshape(db, w_cache, KV_WIDTH),
        rel_bias, sinks)
    mixed_gs, s_new = _gla_sample(ps3, lfs.reshape(db, t_new, GLA_KEY_WIDTH), state_gla[l], gn)
    y_sample = _out_proj(mixed_as.reshape(db * t_new, ATT_WIDTH), mixed_gs.reshape(db * t_new, GLA_WIDTH),
                         w_out_a, w_out_g, x_sample.reshape(db * t_new, D_MODEL), npost, tm=256)
    y_sample = y_sample.reshape(db, t_new, D_MODEL)
    k_win_s = k_win_s.reshape(1, db, w_cache, N_KV_HEADS, HEAD_DIM)
    v_win_s = v_win_s.reshape(1, db, w_cache, N_KV_HEADS, HEAD_DIM)

    return (y_prompt, y_sample, k_win_p, v_win_p, gla_p, k_win_s, v_win_s, s_new[None])
```

```python
import functools
import math

import jax
import jax.numpy as jnp
from jax import lax
from jax.experimental import pallas as pl
from jax.experimental.pallas import tpu as pltpu

F32 = jnp.float32
BF16 = jnp.bfloat16

D_MODEL = 2048
N_META = 16
ATT_WIDTH = 1024
HEAD_DIM = 64
N_HEADS = 16
N_KV_HEADS = 4
GQA_GROUP = 4
KV_WIDTH = 256
WINDOW = 128
N_BUCKETS = 32
MAX_DISTANCE = 128
GLA_WIDTH = 1024
GLA_HEADS = 4
GLA_DV = 256
GLA_DK = 128
GLA_KEY_WIDTH = 512
GLA_RANK = 16
GLA_NORMALIZER = 16.0
GLA_CHUNK = 64
FRONT_PAD = WINDOW - N_META
EPS = 1e-6
PAST_LEN = 8192

IN_SIZES = (ATT_WIDTH, KV_WIDTH, KV_WIDTH, ATT_WIDTH,
            GLA_KEY_WIDTH, GLA_KEY_WIDTH, GLA_WIDTH, GLA_WIDTH, GLA_RANK)

N_MAIN = 5632
TN = 512
COL_QA = 0
COL_GATE_A = 1024
COL_QG = 2048
COL_KG = 2560
COL_VG = 3072
COL_GATE_G = 4096
COL_KA = 5120
COL_VA = 5376
LANES = 128
NEG = -1e30

VMEM_LIMIT = 48 * 1024 * 1024


def _sigmoid(x):
    return 1.0 / (1.0 + jnp.exp(-x))


def _log_sigmoid(x):
    return jnp.minimum(x, 0.0) - jnp.log1p(jnp.exp(-jnp.abs(x)))


def _nt_dot(a, b):
    return lax.dot_general(a, b, (((1,), (1,)), ((), ())), preferred_element_type=F32)


def _tn_dot(a, b):
    return lax.dot_general(a, b, (((0,), (0,)), ((), ())), preferred_element_type=F32)


def _group_major_pieces(src0, dst0, n_rows):
    pieces = []
    for r in range(dst0, dst0 + n_rows, HEAD_DIM):
        g, h = divmod(r // HEAD_DIM, N_KV_HEADS)
        pieces.append((src0 + (h * GQA_GROUP + g) * HEAD_DIM, r - dst0, HEAD_DIM))
    return pieces


def _load_rows_as_bf16(src_hbm, chunk_pieces, stage_ref, sem, dst_ref, chunk_rows):
    def copies(c):
        return [pltpu.make_async_copy(src_hbm.at[pl.ds(src, n), :], stage_ref.at[c % 2, pl.ds(dst, n), :],
                                      sem.at[c % 2]) for src, dst, n in chunk_pieces[c]]

    for cp in copies(0):
        cp.start()
    for c in range(len(chunk_pieces)):
        if c + 1 < len(chunk_pieces):
            for cp in copies(c + 1):
                cp.start()
        for cp in copies(c):
            cp.wait()
        dst_ref[c * chunk_rows:(c + 1) * chunk_rows, :] = stage_ref[c % 2].astype(BF16)


W_CHUNK = 256
IN_PROJ_TM = 256


def _in_proj_weight_pieces():
    src_off = dict(zip(("q_a", "k_a", "v_a", "gate_a", "q_g", "k_g", "v_g", "gate_g"),
                       (sum(IN_SIZES[:i]) for i in range(8))))
    chunks = []
    for name in ("q_a", "gate_a"):
        for d in range(0, ATT_WIDTH, W_CHUNK):
            chunks.append(_group_major_pieces(src_off[name], d, W_CHUNK))
    for name, width in (("q_g", GLA_KEY_WIDTH), ("k_g", GLA_KEY_WIDTH), ("v_g", GLA_WIDTH),
                        ("gate_g", GLA_WIDTH), ("k_a", KV_WIDTH), ("v_a", KV_WIDTH)):
        for d in range(0, width, W_CHUNK):
            chunks.append([(src_off[name] + d, 0, W_CHUNK)])
    return chunks


def _in_proj_kernel(xm_hbm, xp_hbm, xs_hbm, wt_hbm, nw_ref, wa2_ref, ba_ref,
                    p_ref, kv_ref, lf_ref, ps_ref, kvs_ref, lfs_ref,
                    wbf_ref, wz_ref, stage_ref, zstage_ref, xbuf_ref, wsem, zsem, xsem, *, n_p, n_s):
    i = pl.program_id(0)
    tm = xbuf_ref.shape[1]
    n_steps = 1 + n_p + n_s

    def x_copy(step, slot, fn):
        @pl.when(step == 0)
        def _():
            fn(pltpu.make_async_copy(xm_hbm, xbuf_ref.at[slot], xsem.at[slot]))

        @pl.when((step >= 1) & (step <= n_p))
        def _():
            fn(pltpu.make_async_copy(xp_hbm.at[pl.ds((step - 1) * tm, tm), :], xbuf_ref.at[slot], xsem.at[slot]))

        @pl.when(step > n_p)
        def _():
            fn(pltpu.make_async_copy(xs_hbm.at[pl.ds((step - 1 - n_p) * tm, tm), :], xbuf_ref.at[slot],
                                     xsem.at[slot]))

    slot = i % 2

    @pl.when(i == 0)
    def _first_step():
        x_copy(i, slot, lambda cp: cp.start())

        z_copy = pltpu.make_async_copy(wt_hbm.at[pl.ds(N_MAIN, GLA_RANK), :], zstage_ref, zsem.at[0])
        z_copy.start()
        wz_ref[...] = jnp.zeros(wz_ref.shape, BF16)
        z_copy.wait()
        wz_ref[:GLA_RANK, :] = zstage_ref[...].astype(BF16)
        _load_rows_as_bf16(wt_hbm, _in_proj_weight_pieces(), stage_ref, wsem, wbf_ref, W_CHUNK)

    @pl.when(i + 1 < n_steps)
    def _prefetch():
        x_copy(i + 1, 1 - slot, lambda cp: cp.start())

    x_copy(i, slot, lambda cp: cp.wait())

    x = xbuf_ref[slot]
    var = jnp.mean(x * x, axis=-1, keepdims=True)
    hn = (x * lax.rsqrt(var + EPS) * nw_ref[...]).astype(BF16)
    z = _nt_dot(hn, wz_ref[...])
    logits = jnp.dot(z.astype(BF16), wa2_ref[...], preferred_element_type=F32) + ba_ref[...]
    lf = _log_sigmoid(logits) * (1.0 / GLA_NORMALIZER)

    def project(p_out, kv_out, lf_out):
        lf_out[...] = lf
        for j in range(N_MAIN // TN):
            acc = _nt_dot(hn, wbf_ref[j * TN:(j + 1) * TN, :])
            p_out[:, j * TN:(j + 1) * TN] = acc.astype(BF16)
            if j * TN == COL_KA:
                kv_out[...] = acc

    @pl.when(i <= n_p)
    def _():
        project(p_ref, kv_ref, lf_ref)

    @pl.when(i > n_p)
    def _():
        project(ps_ref, kvs_ref, lfs_ref)


def _in_proj(x_meta, x_prompt, x_sample, w_in_t, norm_w, w_a2p, b_a):
    tm = IN_PROJ_TM
    n_p = x_prompt.shape[0] // tm
    n_s = x_sample.shape[0] // tm
    assert x_meta.shape[0] == tm and x_prompt.shape[0] % tm == 0 and x_sample.shape[0] % tm == 0
    assert COL_KA % TN == 0 and len(_in_proj_weight_pieces()) * W_CHUNK == N_MAIN
    m_main = (1 + n_p) * tm
    m_s = n_s * tm

    def main_blk(i):
        return jnp.minimum(i, n_p)

    def s_blk(i):
        return jnp.maximum(i - 1 - n_p, 0)

    def const(i):
        return (0, 0)

    any_spec = pl.BlockSpec(memory_space=pl.ANY)
    buffers = (D_MODEL * N_MAIN * 2 + 2 * D_MODEL * W_CHUNK * 4 + 2 * tm * D_MODEL * 4
               + 2 * 2 * tm * N_MAIN * 2 + 2 * 2 * 2 * tm * GLA_KEY_WIDTH * 4
               + D_MODEL * LANES * 2 + D_MODEL * GLA_RANK * 4 + 2 * LANES * GLA_KEY_WIDTH * 2)
    temps = tm * D_MODEL * (4 + 4 + 2) + 2 * tm * TN * 4 + 2 * tm * GLA_KEY_WIDTH * 4
    vmem_bytes = buffers + temps
    return pl.pallas_call(
        functools.partial(_in_proj_kernel, n_p=n_p, n_s=n_s),
        grid=(1 + n_p + n_s,),
        in_specs=[
            any_spec, any_spec, any_spec, any_spec,
            pl.BlockSpec((1, D_MODEL), const),
            pl.BlockSpec((LANES, GLA_KEY_WIDTH), const),
            pl.BlockSpec((1, GLA_KEY_WIDTH), const),
        ],
        out_specs=[
            pl.BlockSpec((tm, N_MAIN), lambda i: (main_blk(i), 0)),
            pl.BlockSpec((tm, 2 * KV_WIDTH), lambda i: (main_blk(i), 0)),
            pl.BlockSpec((tm, GLA_KEY_WIDTH), lambda i: (main_blk(i), 0)),
            pl.BlockSpec((tm, N_MAIN), lambda i: (s_blk(i), 0)),
            pl.BlockSpec((tm, 2 * KV_WIDTH), lambda i: (s_blk(i), 0)),
            pl.BlockSpec((tm, GLA_KEY_WIDTH), lambda i: (s_blk(i), 0)),
        ],
        out_shape=[
            jax.ShapeDtypeStruct((m_main, N_MAIN), BF16),
            jax.ShapeDtypeStruct((m_main, 2 * KV_WIDTH), F32),
            jax.ShapeDtypeStruct((m_main, GLA_KEY_WIDTH), F32),
            jax.ShapeDtypeStruct((m_s, N_MAIN), BF16),
            jax.ShapeDtypeStruct((m_s, 2 * KV_WIDTH), F32),
            jax.ShapeDtypeStruct((m_s, GLA_KEY_WIDTH), F32),
        ],
        scratch_shapes=[
            pltpu.VMEM((N_MAIN, D_MODEL), BF16),
            pltpu.VMEM((LANES, D_MODEL), BF16),
            pltpu.VMEM((2, W_CHUNK, D_MODEL), F32),
            pltpu.VMEM((GLA_RANK, D_MODEL), F32),
            pltpu.VMEM((2, tm, D_MODEL), F32),
            pltpu.SemaphoreType.DMA((2,)),
            pltpu.SemaphoreType.DMA((1,)),
            pltpu.SemaphoreType.DMA((2,)),
        ],
        compiler_params=pltpu.CompilerParams(
            dimension_semantics=("arbitrary",), vmem_limit_bytes=vmem_bytes),
        name="in_proj",
    )(x_meta, x_prompt, x_sample, w_in_t, norm_w, w_a2p, b_a)


def _t5_bucket(dist):
    max_exact = N_BUCKETS // 2
    d = jnp.maximum(dist, 0)
    ratio = jnp.maximum(d, max_exact).astype(F32) / max_exact
    val = jnp.log(ratio) / math.log(MAX_DISTANCE / max_exact) * (N_BUCKETS - max_exact)
    large = jnp.full(dist.shape, max_exact, jnp.int32)
    for step in range(1, N_BUCKETS - max_exact):
        large = large + (val >= step).astype(jnp.int32)
    return jnp.where(d < max_exact, d, large)


def _swa_prompt_kernel(rb_ref, sink_ref, q_ref, gate_ref, kp_ref, kc_ref, vp_ref, vc_ref,
                       o_ref, tab_ref):
    b = pl.program_id(0)
    n = pl.program_id(1)
    W2 = 2 * WINDOW
    NL = N_KV_HEADS * W2

    @pl.when((b == 0) & (n == 0))
    def _build_bias_tables():
        qi = lax.broadcasted_iota(jnp.int32, (WINDOW, W2), 0)
        kj = lax.broadcasted_iota(jnp.int32, (WINDOW, W2), 1)
        dist = qi - kj + WINDOW
        valid = (dist >= 0) & (dist < WINDOW)
        bucket = _t5_bucket(dist)
        for g in range(GQA_GROUP):
            tab_ref[2, g] = jnp.full((WINDOW, NL), NEG, F32)

        def body(bk, carry):
            hit = (bucket == bk) & valid
            for g in range(GQA_GROUP):
                for h in range(N_KV_HEADS):
                    cur = tab_ref[2, g, :, h * W2:(h + 1) * W2]
                    tab_ref[2, g, :, h * W2:(h + 1) * W2] = jnp.where(hit, rb_ref[bk, h * GQA_GROUP + g], cur)
            return carry

        lax.fori_loop(0, N_BUCKETS, body, 0)
        for v in range(2):
            kvalid = (v - 1) * WINDOW + kj - FRONT_PAD >= 0
            for g in range(GQA_GROUP):
                for h in range(N_KV_HEADS):
                    tab_ref[v, g, :, h * W2:(h + 1) * W2] = jnp.where(
                        kvalid, tab_ref[2, g, :, h * W2:(h + 1) * W2], NEG)

    kk = jnp.concatenate([kp_ref[...], kc_ref[...]], axis=0)
    vv = jnp.concatenate([vp_ref[...], vc_ref[...]], axis=0)
    lane_head = lax.shift_right_logical(lax.broadcasted_iota(jnp.int32, (1, KV_WIDTH), 1), 6)
    zero = jnp.zeros((), BF16)
    kkm = jnp.concatenate([jnp.where(lane_head == h, kk, zero) for h in range(N_KV_HEADS)], axis=0)
    vvm = jnp.concatenate([jnp.where(lane_head == h, vv, zero) for h in range(N_KV_HEADS)], axis=0)
    tv = jnp.minimum(n, 2)
    scale = HEAD_DIM ** -0.5
    scores = []
    for g in range(GQA_GROUP):
        qg = (q_ref[:, g * KV_WIDTH:(g + 1) * KV_WIDTH].astype(F32) * scale).astype(BF16)
        scores.append(_nt_dot(qg, kkm) + tab_ref[tv, g])
    for g in range(GQA_GROUP):
        probs = []
        for h in range(N_KV_HEADS):
            s = scores[g][:, h * W2:(h + 1) * W2]
            sink = sink_ref[h * GQA_GROUP + g]
            m = jnp.maximum(jnp.max(s, axis=-1, keepdims=True), sink)
            e = jnp.exp(s - m)
            den = jnp.sum(e, axis=-1, keepdims=True) + jnp.exp(sink - m)
            probs.append((e * (1.0 / den)).astype(BF16))
        p = jnp.concatenate(probs, axis=1)
        o = jnp.dot(p, vvm, preferred_element_type=F32)
        gate = gate_ref[:, g * KV_WIDTH:(g + 1) * KV_WIDTH].astype(F32)
        o_ref[:, g * KV_WIDTH:(g + 1) * KV_WIDTH] = (o * (gate * _sigmoid(gate))).astype(BF16)


def _swa_prompt(p_all, rel_bias, sinks, n_batch, n_blocks, meta_blk):
    def real(b, m):
        return meta_blk + 1 + b * (n_blocks - 1) + m

    def row(b, n):
        return jnp.where(n == 0, meta_blk, real(b, n - 1))

    def prev(b, n):
        return jnp.where(n <= 1, meta_blk, real(b, n - 2))

    kblk = COL_KA // KV_WIDTH
    vblk = COL_VA // KV_WIDTH
    out_rows = n_batch * (n_blocks - 1) * WINDOW
    return pl.pallas_call(
        _swa_prompt_kernel,
        grid=(n_batch, n_blocks),
        in_specs=[
            pl.BlockSpec(memory_space=pltpu.SMEM),
            pl.BlockSpec(memory_space=pltpu.SMEM),
            pl.BlockSpec((WINDOW, ATT_WIDTH), lambda b, n: (row(b, n), COL_QA // ATT_WIDTH)),
            pl.BlockSpec((WINDOW, ATT_WIDTH), lambda b, n: (row(b, n), COL_GATE_A // ATT_WIDTH)),
            pl.BlockSpec((WINDOW, KV_WIDTH), lambda b, n: (prev(b, n), kblk)),
            pl.BlockSpec((WINDOW, KV_WIDTH), lambda b, n: (row(b, n), kblk)),
            pl.BlockSpec((WINDOW, KV_WIDTH), lambda b, n: (prev(b, n), vblk)),
            pl.BlockSpec((WINDOW, KV_WIDTH), lambda b, n: (row(b, n), vblk)),
        ],
        out_specs=pl.BlockSpec(
            (WINDOW, ATT_WIDTH), lambda b, n: (b * (n_blocks - 1) + jnp.maximum(n - 1, 0), 0)),
        out_shape=jax.ShapeDtypeStruct((out_rows, ATT_WIDTH), BF16),
        scratch_shapes=[pltpu.VMEM((3, GQA_GROUP, WINDOW, N_KV_HEADS * 2 * WINDOW), F32)],
        compiler_params=pltpu.CompilerParams(
            dimension_semantics=("arbitrary", "arbitrary"), vmem_limit_bytes=VMEM_LIMIT),
        name="swa_prompt",
    )(rel_bias, sinks, p_all, p_all, p_all, p_all, p_all, p_all)


def _diag_scores(q3, k3, b3, width):
    nb = q3.shape[0]
    n_heads = q3.shape[2] // GLA_DK
    r_io = lax.broadcasted_iota(jnp.int32, (nb, 8, width), 0)
    t_io = lax.broadcasted_iota(jnp.int32, (nb, 8, width), 1)
    l_io = lax.broadcasted_iota(jnp.int32, (nb, 8, width), 2)
    s_of_lane = l_io - 8 * r_io
    outs = [jnp.zeros((nb, 8, width), F32) for _ in range(n_heads)]
    for s in range(8):
        bs = jnp.broadcast_to(b3[:, s:s + 1, :], b3.shape)
        ks = jnp.broadcast_to(k3[:, s:s + 1, :], k3.shape)
        e = jnp.exp(jnp.minimum(b3 - bs, 0.0))
        p = q3 * e * ks
        for h in range(n_heads):
            col = jnp.sum(p[:, :, h * GLA_DK:(h + 1) * GLA_DK], axis=-1, keepdims=True)
            outs[h] = jnp.where(s_of_lane == s, col, outs[h])
    causal = (s_of_lane >= 0) & (s_of_lane <= t_io) & (s_of_lane < 8)
    return [jnp.where(causal, o, 0.0) for o in outs]


def _gla_finish(o, gn, gate):
    var = jnp.mean(o * o, axis=-1, keepdims=True)
    on = o * lax.rsqrt(var + EPS) * gn
    return (on * (gate * _sigmoid(gate))).astype(BF16)


def _decay_column(row):
    eye = (lax.broadcasted_iota(jnp.int32, (GLA_DK, GLA_DK), 0)
           == lax.broadcasted_iota(jnp.int32, (GLA_DK, GLA_DK), 1))
    return jnp.sum(jnp.where(eye, jnp.broadcast_to(row, (GLA_DK, GLA_DK)), 0.0), axis=1, keepdims=True)


def _gla_prompt_kernel(*refs):
    o_ref, s_ref = refs[-2:]
    gn_ref = refs[-3]
    n_batch = (len(refs) - 3) // 5
    qg_refs, kg_refs, vg_refs, gate_refs, lf_refs = (refs[j:5 * n_batch:5] for j in range(5))
    c = pl.program_id(0)
    C = GLA_CHUNK

    @pl.when(c == 0)
    def _():
        s_ref[...] = jnp.zeros(s_ref.shape, F32)

    row = lax.broadcasted_iota(jnp.int32, (C, 1), 0)
    valid = (c * C + row) >= FRONT_PAD
    tril = (lax.broadcasted_iota(jnp.int32, (C, C), 0)
            >= lax.broadcasted_iota(jnp.int32, (C, C), 1)).astype(F32).astype(BF16)
    t_io = lax.broadcasted_iota(jnp.int32, (C, C), 0)
    s_io = lax.broadcasted_iota(jnp.int32, (C, C), 1)
    gn = gn_ref[...]

    for b in range(n_batch):
        q = qg_refs[b][...].astype(F32) * (GLA_DK ** -0.5)
        k = jnp.where(valid, kg_refs[b][...].astype(F32), 0.0)
        g = jnp.where(valid, lf_refs[b][...], 0.0)
        g1 = g.astype(BF16)
        r1 = g - g1.astype(F32)
        g2 = r1.astype(BF16)
        g3 = (r1 - g2.astype(F32)).astype(BF16)
        bc = (jnp.dot(tril, g1, preferred_element_type=F32)
              + jnp.dot(tril, g2, preferred_element_type=F32)
              + jnp.dot(tril, g3, preferred_element_type=F32))
        b_last = bc[C - 1:C, :]
        qt = (q * jnp.exp(bc)).astype(BF16)
        kt = (k * jnp.exp(b_last - bc)).astype(BF16)

        levels = []
        for half in (32, 16, 8):
            nblk = C // (2 * half)
            ref = jnp.concatenate(
                [jnp.broadcast_to(bc[i * 2 * half + half - 1:i * 2 * half + half, :], (2 * half, bc.shape[1]))
                 for i in range(nblk)], axis=0)
            shift = int(math.log2(half))
            upper = (lax.shift_right_logical(row, shift) & 1) == 1
            e = jnp.exp(jnp.where(upper, bc - ref, ref - bc))
            ql = jnp.where(upper, q * e, 0.0).astype(BF16)
            kl = jnp.where(upper, 0.0, k * e).astype(BF16)
            same = lax.shift_right_logical(t_io, shift + 1) == lax.shift_right_logical(s_io, shift + 1)
            levels.append((ql, kl, same))

        diag = _diag_scores(q.reshape(C // 8, 8, -1), k.reshape(C // 8, 8, -1), bc.reshape(C // 8, 8, -1), C)

        for h in range(GLA_HEADS):
            ks = slice(h * GLA_DK, (h + 1) * GLA_DK)
            vs = slice(h * GLA_DV, (h + 1) * GLA_DV)
            a = diag[h].reshape(C, C)
            for ql, kl, same in levels:
                a = a + jnp.where(same, _nt_dot(ql[:, ks], kl[:, ks]), 0.0)
            v = vg_refs[b][:, vs]
            st = s_ref[b, h]
            o = (jnp.dot(a.astype(BF16), v, preferred_element_type=F32)
                 + jnp.dot(qt[:, ks], st.astype(BF16), preferred_element_type=F32))
            s_ref[b, h] = _decay_column(jnp.exp(b_last[:, ks])) * st + _tn_dot(kt[:, ks], v)
            o_ref[b, :, vs] = _gla_finish(o, gn, gate_refs[b][:, vs].astype(F32))


def _gla_prompt(p_all, lf_all, gla_norm_row, n_batch, n_chunks, meta_blk):
    skip = (FRONT_PAD + N_META) // GLA_CHUNK
    out_len = (n_chunks - skip) * GLA_CHUNK
    C = GLA_CHUNK
    per_blk = WINDOW // C

    def chunk(b):
        def f(c):
            return jnp.where(c < skip, meta_blk * per_blk + c, (meta_blk + 1) * per_blk + b * (n_chunks - skip) + c - skip)
        return f

    in_specs = []
    args = []
    for b in range(n_batch):
        f = chunk(b)
        in_specs += [
            pl.BlockSpec((C, GLA_KEY_WIDTH), lambda c, f=f: (f(c), COL_QG // GLA_KEY_WIDTH)),
            pl.BlockSpec((C, GLA_KEY_WIDTH), lambda c, f=f: (f(c), COL_KG // GLA_KEY_WIDTH)),
            pl.BlockSpec((C, GLA_WIDTH), lambda c, f=f: (f(c), COL_VG // GLA_WIDTH)),
            pl.BlockSpec((C, GLA_WIDTH), lambda c, f=f: (f(c), COL_GATE_G // GLA_WIDTH)),
            pl.BlockSpec((C, GLA_KEY_WIDTH), lambda c, f=f: (f(c), 0)),
        ]
        args += [p_all, p_all, p_all, p_all, lf_all]
    return pl.pallas_call(
        _gla_prompt_kernel,
        grid=(n_chunks,),
        in_specs=in_specs + [pl.BlockSpec((1, GLA_DV), lambda c: (0, 0))],
        out_specs=[
            pl.BlockSpec((n_batch, C, GLA_WIDTH), lambda c: (0, jnp.maximum(c - skip, 0), 0)),
            pl.BlockSpec((n_batch, GLA_HEADS, GLA_DK, GLA_DV), lambda c: (0, 0, 0, 0)),
        ],
        out_shape=[
            jax.ShapeDtypeStruct((n_batch, out_len, GLA_WIDTH), BF16),
            jax.ShapeDtypeStruct((n_batch, GLA_HEADS, GLA_DK, GLA_DV), F32),
        ],
        compiler_params=pltpu.CompilerParams(
            dimension_semantics=("arbitrary",), vmem_limit_bytes=VMEM_LIMIT),
        name="gla_prompt",
    )(*args, gla_norm_row)


OUT_PROJ_TM = 256
WO_CHUNK = 512


def _out_proj_kernel(map_ref, mgp_ref, xp_ref, mas_ref, mgs_ref, xs_ref, nw_ref, w_hbm,
                     yp_ref, ys_ref, wbf_ref, stage_ref, wsem, *, n_p):
    i = pl.program_id(0)

    @pl.when(i == 0)
    def _load_weight():
        chunks = [_group_major_pieces(0, d, WO_CHUNK) for d in range(0, ATT_WIDTH, WO_CHUNK)]
        chunks += [[(d, 0, WO_CHUNK)] for d in range(ATT_WIDTH, D_MODEL, WO_CHUNK)]
        _load_rows_as_bf16(w_hbm, chunks, stage_ref, wsem, wbf_ref, WO_CHUNK)

    def finish(ma_ref, mg_ref, x_ref, y_ref):
        o = (jnp.dot(ma_ref[...], wbf_ref[:ATT_WIDTH, :], preferred_element_type=F32)
             + jnp.dot(mg_ref[...], wbf_ref[ATT_WIDTH:, :], preferred_element_type=F32))
        var = jnp.mean(o * o, axis=-1, keepdims=True)
        y_ref[...] = x_ref[...] + o * lax.rsqrt(var + EPS) * nw_ref[...]

    @pl.when(i < n_p)
    def _():
        finish(map_ref, mgp_ref, xp_ref, yp_ref)

    @pl.when(i >= n_p)
    def _():
        finish(mas_ref, mgs_ref, xs_ref, ys_ref)


def _out_proj(ma_p, mg_p, x_p, ma_s, mg_s, x_s, w_out_l, norm_w):
    tm = OUT_PROJ_TM
    n_p = x_p.shape[0] // tm
    n_s = x_s.shape[0] // tm
    assert x_p.shape[0] % tm == 0 and x_s.shape[0] % tm == 0 and ATT_WIDTH + GLA_WIDTH == D_MODEL

    def p_blk(i):
        return (jnp.minimum(i, n_p - 1), 0)

    def s_blk(i):
        return (jnp.maximum(i - n_p, 0), 0)

    row_bytes = ATT_WIDTH * 2 + GLA_WIDTH * 2 + 2 * D_MODEL * 4
    vmem_bytes = (D_MODEL * D_MODEL * 2 + 2 * WO_CHUNK * D_MODEL * 4 + 2 * 2 * tm * row_bytes
                  + 3 * tm * D_MODEL * 4)
    return pl.pallas_call(
        functools.partial(_out_proj_kernel, n_p=n_p),
        grid=(n_p + n_s,),
        in_specs=[
            pl.BlockSpec((tm, ATT_WIDTH), p_blk),
            pl.BlockSpec((tm, GLA_WIDTH), p_blk),
            pl.BlockSpec((tm, D_MODEL), p_blk),
            pl.BlockSpec((tm, ATT_WIDTH), s_blk),
            pl.BlockSpec((tm, GLA_WIDTH), s_blk),
            pl.BlockSpec((tm, D_MODEL), s_blk),
            pl.BlockSpec((1, D_MODEL), lambda i: (0, 0)),
            pl.BlockSpec(memory_space=pl.ANY),
        ],
        out_specs=[
            pl.BlockSpec((tm, D_MODEL), p_blk),
            pl.BlockSpec((tm, D_MODEL), s_blk),
        ],
        out_shape=[
            jax.ShapeDtypeStruct(x_p.shape, F32),
            jax.ShapeDtypeStruct(x_s.shape, F32),
        ],
        scratch_shapes=[
            pltpu.VMEM((D_MODEL, D_MODEL), BF16),
            pltpu.VMEM((2, WO_CHUNK, D_MODEL), F32),
            pltpu.SemaphoreType.DMA((2,)),
        ],
        compiler_params=pltpu.CompilerParams(
            dimension_semantics=("arbitrary",), vmem_limit_bytes=vmem_bytes),
        name="out_proj",
    )(ma_p, mg_p, x_p, ma_s, mg_s, x_s, norm_w, w_out_l)


SWA_SAMPLE_SEQS = 8


def _swa_sample_kernel(rb_ref, sink_ref, q_ref, gate_ref, kvn_ref, ck_ref, cv_ref,
                       o_ref, ko_ref, vo_ref, tabc_ref, tabn_ref, sinkcol_ref, *, t_new):
    step = pl.program_id(0)
    W = ck_ref.shape[1]
    rows = N_HEADS * t_new

    @pl.when(step == 0)
    def _build_tables():
        r = lax.broadcasted_iota(jnp.int32, (rows, W), 0)
        qi = r & (t_new - 1)
        kj = lax.broadcasted_iota(jnp.int32, (rows, W), 1)
        dist_c = qi + W - kj
        valid_c = (dist_c >= 0) & (dist_c < WINDOW)
        bucket_c = _t5_bucket(dist_c)
        rn = lax.broadcasted_iota(jnp.int32, (rows, t_new), 0)
        qn = rn & (t_new - 1)
        jn = lax.broadcasted_iota(jnp.int32, (rows, t_new), 1)
        dist_n = qn - jn
        valid_n = (dist_n >= 0) & (dist_n < WINDOW)
        bucket_n = _t5_bucket(dist_n)
        head_c = lax.shift_right_logical(r, int(math.log2(t_new)))
        head_n = lax.shift_right_logical(rn, int(math.log2(t_new)))
        head_s = lax.shift_right_logical(lax.broadcasted_iota(jnp.int32, (rows, 1), 0), int(math.log2(t_new)))
        tabc_ref[...] = jnp.full((rows, W), NEG, F32)
        tabn_ref[...] = jnp.full((rows, t_new), NEG, F32)
        sinkcol_ref[...] = jnp.zeros((rows, 1), F32)
        for g in range(GQA_GROUP):
            for h in range(N_KV_HEADS):
                hd = h * GQA_GROUP + g
                blk = g * N_KV_HEADS + h
                sinkcol_ref[...] = jnp.where(head_s == blk, sink_ref[hd], sinkcol_ref[...])

        def body(bk, carry):
            for g in range(GQA_GROUP):
                for h in range(N_KV_HEADS):
                    hd = h * GQA_GROUP + g
                    blk = g * N_KV_HEADS + h
                    val = rb_ref[bk, hd]
                    tabc_ref[...] = jnp.where((bucket_c == bk) & valid_c & (head_c == blk), val, tabc_ref[...])
                    tabn_ref[...] = jnp.where((bucket_n == bk) & valid_n & (head_n == blk), val, tabn_ref[...])
            return carry

        lax.fori_loop(0, N_BUCKETS, body, 0)

    lane_head = lax.shift_right_logical(lax.broadcasted_iota(jnp.int32, (1, KV_WIDTH), 1), 6)
    scale = HEAD_DIM ** -0.5
    sink = sinkcol_ref[...]
    for s in range(q_ref.shape[0]):
        q = q_ref[s].astype(F32)
        pieces = []
        for g in range(GQA_GROUP):
            qg = q[:, g * KV_WIDTH:(g + 1) * KV_WIDTH]
            for h in range(N_KV_HEADS):
                pieces.append(jnp.where(lane_head == h, qg, 0.0))
        lhs = jnp.concatenate(pieces, axis=0).astype(BF16)
        kc = ck_ref[s]
        vc = cv_ref[s]
        kn = kvn_ref[s, :, :KV_WIDTH]
        vn = kvn_ref[s, :, KV_WIDTH:]
        sc = _nt_dot(lhs, kc.astype(BF16)) * scale + tabc_ref[...]
        sn = _nt_dot(lhs, kn.astype(BF16)) * scale + tabn_ref[...]
        m = jnp.maximum(jnp.maximum(jnp.max(sc, axis=-1, keepdims=True),
                                    jnp.max(sn, axis=-1, keepdims=True)), sink)
        ec = jnp.exp(sc - m)
        en = jnp.exp(sn - m)
        den = (jnp.sum(ec, axis=-1, keepdims=True) + jnp.sum(en, axis=-1, keepdims=True)
               + jnp.exp(sink - m))
        inv = 1.0 / den
        o = (jnp.dot((ec * inv).astype(BF16), vc.astype(BF16), preferred_element_type=F32)
             + jnp.dot((en * inv).astype(BF16), vn.astype(BF16), preferred_element_type=F32))
        outs = []
        for g in range(GQA_GROUP):
            acc = jnp.zeros((t_new, KV_WIDTH), F32)
            for h in range(N_KV_HEADS):
                blk = g * N_KV_HEADS + h
                acc = jnp.where(lane_head == h, o[blk * t_new:(blk + 1) * t_new, :], acc)
            outs.append(acc)
        att = jnp.concatenate(outs, axis=1)
        gate = gate_ref[s].astype(F32)
        o_ref[s] = (att * (gate * _sigmoid(gate))).astype(BF16)
        ko_ref[s, :W - t_new, :] = kc[t_new:, :]
        ko_ref[s, W - t_new:, :] = kn
        vo_ref[s, :W - t_new, :] = vc[t_new:, :]
        vo_ref[s, W - t_new:, :] = vn


def _swa_sample(p3, kv3, cache_k, cache_v, rel_bias, sinks):
    db, t_new, _ = p3.shape
    w = cache_k.shape[1]
    assert t_new & (t_new - 1) == 0 and w >= t_new and w <= WINDOW
    sb = SWA_SAMPLE_SEQS
    rows = N_HEADS * t_new
    return pl.pallas_call(
        functools.partial(_swa_sample_kernel, t_new=t_new),
        grid=(db // sb,),
        in_specs=[
            pl.BlockSpec(memory_space=pltpu.SMEM),
            pl.BlockSpec(memory_space=pltpu.SMEM),
            pl.BlockSpec((sb, t_new, ATT_WIDTH), lambda i: (i, 0, COL_QA // ATT_WIDTH)),
            pl.BlockSpec((sb, t_new, ATT_WIDTH), lambda i: (i, 0, COL_GATE_A // ATT_WIDTH)),
            pl.BlockSpec((sb, t_new, 2 * KV_WIDTH), lambda i: (i, 0, 0)),
            pl.BlockSpec((sb, w, KV_WIDTH), lambda i: (i, 0, 0)),
            pl.BlockSpec((sb, w, KV_WIDTH), lambda i: (i, 0, 0)),
        ],
        out_specs=[
            pl.BlockSpec((sb, t_new, ATT_WIDTH), lambda i: (i, 0, 0)),
            pl.BlockSpec((sb, w, KV_WIDTH), lambda i: (i, 0, 0)),
            pl.BlockSpec((sb, w, KV_WIDTH), lambda i: (i, 0, 0)),
        ],
        out_shape=[
            jax.ShapeDtypeStruct((db, t_new, ATT_WIDTH), BF16),
            jax.ShapeDtypeStruct((db, w, KV_WIDTH), F32),
            jax.ShapeDtypeStruct((db, w, KV_WIDTH), F32),
        ],
        scratch_shapes=[
            pltpu.VMEM((rows, w), F32),
            pltpu.VMEM((rows, t_new), F32),
            pltpu.VMEM((rows, 1), F32),
        ],
        compiler_params=pltpu.CompilerParams(
            dimension_semantics=("arbitrary",), vmem_limit_bytes=VMEM_LIMIT),
        name="swa_sample",
    )(rel_bias, sinks, p3, p3, kv3, cache_k, cache_v)


GLA_SAMPLE_SEQS = 4


def _cumsum_rows8(g):
    row = lax.broadcasted_iota(jnp.int32, g.shape, 0)
    out = g
    for sh in (1, 2, 4):
        out = out + jnp.where(row >= sh, pltpu.roll(out, sh, axis=0), 0.0)
    return out


def _gla_sample_kernel(qg_ref, kg_ref, vg_ref, gate_ref, lf_ref, gn_ref, st_ref, o_ref, so_ref):
    T = qg_ref.shape[1]
    gn = gn_ref[...]
    t_col = lax.broadcasted_iota(jnp.int32, (T, 1), 0)
    zpad = jnp.zeros((8, GLA_DK), F32)
    for s in range(qg_ref.shape[0]):
        q = qg_ref[s].astype(F32) * (GLA_DK ** -0.5)
        k = kg_ref[s].astype(F32)
        bc = _cumsum_rows8(lf_ref[s])
        b_last = bc[T - 1:T, :]
        qt = q * jnp.exp(bc)
        kt = k * jnp.exp(b_last - bc)
        v_all = vg_ref[s].astype(F32)
        cols = [[None] * T for _ in range(GLA_HEADS)]
        for j in range(T):
            bs = jnp.broadcast_to(bc[j:j + 1, :], bc.shape)
            ks_ = jnp.broadcast_to(k[j:j + 1, :], k.shape)
            p = q * jnp.exp(jnp.minimum(bc - bs, 0.0)) * ks_
            for h in range(GLA_HEADS):
                col = jnp.sum(p[:, h * GLA_DK:(h + 1) * GLA_DK], axis=-1, keepdims=True)
                cols[h][j] = jnp.where(t_col >= j, col, 0.0)
        for h in range(GLA_HEADS):
            ks = slice(h * GLA_DK, (h + 1) * GLA_DK)
            vs = slice(h * GLA_DV, (h + 1) * GLA_DV)
            v = v_all[:, vs]
            st = st_ref[s, h]
            qt16 = jnp.concatenate([qt[:, ks], zpad], axis=0).astype(BF16)
            o = jnp.dot(qt16, st.astype(BF16), preferred_element_type=F32)[:T, :]
            for j in range(T):
                o = o + cols[h][j] * v[j:j + 1, :]
            kt16 = jnp.concatenate([kt[:, ks], zpad], axis=0).astype(BF16)
            v16 = jnp.concatenate([v, jnp.zeros((8, GLA_DV), F32)], axis=0).astype(BF16)
            so_ref[s, h] = _decay_column(jnp.exp(b_last[:, ks])) * st + _tn_dot(kt16, v16)
            o_ref[s, :, vs] = _gla_finish(o, gn, gate_ref[s, :, vs].astype(F32))


def _gla_sample(p3, lf3, state, gla_norm_row):
    db, t_new, _ = p3.shape
    assert t_new == 8
    sb = GLA_SAMPLE_SEQS
    return pl.pallas_call(
        _gla_sample_kernel,
        grid=(db // sb,),
        in_specs=[
            pl.BlockSpec((sb, t_new, GLA_KEY_WIDTH), lambda i: (i, 0, COL_QG // GLA_KEY_WIDTH)),
            pl.BlockSpec((sb, t_new, GLA_KEY_WIDTH), lambda i: (i, 0, COL_KG // GLA_KEY_WIDTH)),
            pl.BlockSpec((sb, t_new, GLA_WIDTH), lambda i: (i, 0, COL_VG // GLA_WIDTH)),
            pl.BlockSpec((sb, t_new, GLA_WIDTH), lambda i: (i, 0, COL_GATE_G // GLA_WIDTH)),
            pl.BlockSpec((sb, t_new, GLA_KEY_WIDTH), lambda i: (i, 0, 0)),
            pl.BlockSpec((1, GLA_DV), lambda i: (0, 0)),
            pl.BlockSpec((sb, GLA_HEADS, GLA_DK, GLA_DV), lambda i: (i, 0, 0, 0)),
        ],
        out_specs=[
            pl.BlockSpec((sb, t_new, GLA_WIDTH), lambda i: (i, 0, 0)),
            pl.BlockSpec((sb, GLA_HEADS, GLA_DK, GLA_DV), lambda i: (i, 0, 0, 0)),
        ],
        out_shape=[
            jax.ShapeDtypeStruct((db, t_new, GLA_WIDTH), BF16),
            jax.ShapeDtypeStruct((db, GLA_HEADS, GLA_DK, GLA_DV), F32),
        ],
        compiler_params=pltpu.CompilerParams(
            dimension_semantics=("arbitrary",), vmem_limit_bytes=VMEM_LIMIT),
        name="gla_sample",
    )(p3, p3, p3, p3, lf3, gla_norm_row, state)


def kernel(x_prompt, x_sample, cache_k_win, cache_v_win, state_gla, meta_tokens, rel_bias,
           norm_pre, norm_post, w_in, w_a2, b_a, attn_sinks, gla_norm, w_out):
    n_batch, seq, _ = x_prompt.shape
    db, t_new, _ = x_sample.shape
    depth = w_in.shape[0]
    assert depth == 1
    l = 0
    lp = seq + WINDOW
    n_blocks = lp // WINDOW
    n_chunks = lp // GLA_CHUNK
    w_cache = cache_k_win.shape[2]

    w_in_t = w_in[l].T
    assert sum(IN_SIZES[:-1]) == N_MAIN
    w_a2p = jnp.pad(w_a2[l], ((0, LANES - GLA_RANK), (0, 0))).astype(BF16)
    npre = norm_pre[l][None, :]
    npost = norm_post[l][None, :]
    ba = b_a[l][None, :]
    gn = gla_norm[l][None, :]
    sinks = attn_sinks[l]

    tm = IN_PROJ_TM
    x_meta = jnp.concatenate([jnp.zeros((tm - N_META, D_MODEL), x_prompt.dtype),
                              meta_tokens.astype(x_prompt.dtype)], axis=0)
    p_all, kv_all, lf_all, ps, kvs, lfs = _in_proj(
        x_meta, x_prompt.reshape(n_batch * seq, D_MODEL), x_sample.reshape(db * t_new, D_MODEL),
        w_in_t, npre, w_a2p, ba)
    meta_blk = tm // WINDOW - 1

    mixed_a = _swa_prompt(p_all, rel_bias, sinks, n_batch, n_blocks, meta_blk)
    mixed_g, s_fin = _gla_prompt(p_all, lf_all, gn, n_batch, n_chunks, meta_blk)
    kv_p = jnp.stack([kv_all[tm + (b + 1) * seq - WINDOW:tm + (b + 1) * seq] for b in range(n_batch)])
    k_win_p = kv_p[..., :KV_WIDTH].reshape(1, n_batch, WINDOW, N_KV_HEADS, HEAD_DIM)
    v_win_p = kv_p[..., KV_WIDTH:].reshape(1, n_batch, WINDOW, N_KV_HEADS, HEAD_DIM)
    gla_p = s_fin[None]

    ps3 = ps.reshape(db, t_new, N_MAIN)
    mixed_as, k_win_s, v_win_s = _swa_sample(
        ps3, kvs.reshape(db, t_new, 2 * KV_WIDTH),
        cache_k_win[l].reshape(db, w_cache, KV_WIDTH), cache_v_win[l].reshape(db, w_cache, KV_WIDTH),
        rel_bias, sinks)
    mixed_gs, s_new = _gla_sample(ps3, lfs.reshape(db, t_new, GLA_KEY_WIDTH), state_gla[l], gn)

    y_prompt, y_sample = _out_proj(
        mixed_a, mixed_g.reshape(n_batch * seq, GLA_WIDTH), x_prompt.reshape(n_batch * seq, D_MODEL),
        mixed_as.reshape(db * t_new, ATT_WIDTH), mixed_gs.reshape(db * t_new, GLA_WIDTH),
        x_sample.reshape(db * t_new, D_MODEL), w_out[l], npost)
    y_prompt = y_prompt.reshape(n_batch, seq, D_MODEL)
    y_sample = y_sample.reshape(db, t_new, D_MODEL)
    k_win_s = k_win_s.reshape(1, db, w_cache, N_KV_HEADS, HEAD_DIM)
    v_win_s = v_win_s.reshape(1, db, w_cache, N_KV_HEADS, HEAD_DIM)

    return (y_prompt, y_sample, k_win_p, v_win_p, gla_p, k_win_s, v_win_s, s_new[None])
```

```python
import functools
import math

import jax
import jax.numpy as jnp
from jax import lax
from jax.experimental import pallas as pl
from jax.experimental.pallas import tpu as pltpu

F32 = jnp.float32
BF16 = jnp.bfloat16

D_MODEL = 2048
N_META = 16
ATT_WIDTH = 1024
HEAD_DIM = 64
N_HEADS = 16
N_KV_HEADS = 4
GQA_GROUP = 4
KV_WIDTH = 256
WINDOW = 128
N_BUCKETS = 32
MAX_DISTANCE = 128
GLA_WIDTH = 1024
GLA_HEADS = 4
GLA_DV = 256
GLA_DK = 128
GLA_KEY_WIDTH = 512
GLA_RANK = 16
GLA_NORMALIZER = 16.0
GLA_CHUNK = 64
FRONT_PAD = WINDOW - N_META
EPS = 1e-6
PAST_LEN = 8192

IN_SIZES = (ATT_WIDTH, KV_WIDTH, KV_WIDTH, ATT_WIDTH,
            GLA_KEY_WIDTH, GLA_KEY_WIDTH, GLA_WIDTH, GLA_WIDTH, GLA_RANK)

N_MAIN = 5632
TN = 512
COL_QA = 0
COL_GATE_A = 1024
COL_QG = 2048
COL_KG = 2560
COL_VG = 3072
COL_GATE_G = 4096
COL_KA = 5120
COL_VA = 5376
LANES = 128
NEG = -1e30

VMEM_LIMIT = 48 * 1024 * 1024


def _sigmoid(x):
    return 1.0 / (1.0 + jnp.exp(-x))


def _log_sigmoid(x):
    return jnp.minimum(x, 0.0) - jnp.log1p(jnp.exp(-jnp.abs(x)))


def _nt_dot(a, b):
    return lax.dot_general(a, b, (((1,), (1,)), ((), ())), preferred_element_type=F32)


def _tn_dot(a, b):
    return lax.dot_general(a, b, (((0,), (0,)), ((), ())), preferred_element_type=F32)


def _group_major_pieces(src0, dst0, n_rows):
    pieces = []
    for r in range(dst0, dst0 + n_rows, HEAD_DIM):
        g, h = divmod(r // HEAD_DIM, N_KV_HEADS)
        pieces.append((src0 + (h * GQA_GROUP + g) * HEAD_DIM, r - dst0, HEAD_DIM))
    return pieces


def _load_rows_as_bf16(src_hbm, chunk_pieces, stage_ref, sem, dst_ref, chunk_rows):
    def copies(c):
        return [pltpu.make_async_copy(src_hbm.at[pl.ds(src, n), :], stage_ref.at[c % 2, pl.ds(dst, n), :],
                                      sem.at[c % 2]) for src, dst, n in chunk_pieces[c]]

    for cp in copies(0):
        cp.start()
    for c in range(len(chunk_pieces)):
        if c + 1 < len(chunk_pieces):
            for cp in copies(c + 1):
                cp.start()
        for cp in copies(c):
            cp.wait()
        dst_ref[c * chunk_rows:(c + 1) * chunk_rows, :] = stage_ref[c % 2].astype(BF16)


W_CHUNK = 256
IN_PROJ_TM = 256


def _in_proj_weight_pieces():
    src_off = dict(zip(("q_a", "k_a", "v_a", "gate_a", "q_g", "k_g", "v_g", "gate_g"),
                       (sum(IN_SIZES[:i]) for i in range(8))))
    chunks = []
    for name in ("q_a", "gate_a"):
        for d in range(0, ATT_WIDTH, W_CHUNK):
            chunks.append(_group_major_pieces(src_off[name], d, W_CHUNK))
    for name, width in (("q_g", GLA_KEY_WIDTH), ("k_g", GLA_KEY_WIDTH), ("v_g", GLA_WIDTH),
                        ("gate_g", GLA_WIDTH), ("k_a", KV_WIDTH), ("v_a", KV_WIDTH)):
        for d in range(0, width, W_CHUNK):
            chunks.append([(src_off[name] + d, 0, W_CHUNK)])
    return chunks


def _in_proj_kernel(xm_hbm, xp_hbm, xs_hbm, wt_hbm, nw_ref, wa2_ref, ba_ref,
                    p_ref, kv_ref, lf_ref, ps_ref, kvs_ref, lfs_ref,
                    wbf_ref, wz_ref, stage_ref, zstage_ref, xbuf_ref, wsem, zsem, xsem, *, n_p, n_s):
    i = pl.program_id(0)
    tm = xbuf_ref.shape[1]
    n_steps = 1 + n_p + n_s

    def x_copy(step, slot, fn):
        @pl.when(step == 0)
        def _():
            fn(pltpu.make_async_copy(xm_hbm, xbuf_ref.at[slot], xsem.at[slot]))

        @pl.when((step >= 1) & (step <= n_p))
        def _():
            fn(pltpu.make_async_copy(xp_hbm.at[pl.ds((step - 1) * tm, tm), :], xbuf_ref.at[slot], xsem.at[slot]))

        @pl.when(step > n_p)
        def _():
            fn(pltpu.make_async_copy(xs_hbm.at[pl.ds((step - 1 - n_p) * tm, tm), :], xbuf_ref.at[slot],
                                     xsem.at[slot]))

    slot = i % 2

    @pl.when(i == 0)
    def _first_step():
        x_copy(i, slot, lambda cp: cp.start())

        z_copy = pltpu.make_async_copy(wt_hbm.at[pl.ds(N_MAIN, GLA_RANK), :], zstage_ref, zsem.at[0])
        z_copy.start()
        wz_ref[...] = jnp.zeros(wz_ref.shape, BF16)
        z_copy.wait()
        wz_ref[:GLA_RANK, :] = zstage_ref[...].astype(BF16)
        _load_rows_as_bf16(wt_hbm, _in_proj_weight_pieces(), stage_ref, wsem, wbf_ref, W_CHUNK)

    @pl.when(i + 1 < n_steps)
    def _prefetch():
        x_copy(i + 1, 1 - slot, lambda cp: cp.start())

    x_copy(i, slot, lambda cp: cp.wait())

    x = xbuf_ref[slot]
    var = jnp.mean(x * x, axis=-1, keepdims=True)
    hn = (x * lax.rsqrt(var + EPS) * nw_ref[...]).astype(BF16)
    z = _nt_dot(hn, wz_ref[...])
    logits = jnp.dot(z.astype(BF16), wa2_ref[...], preferred_element_type=F32) + ba_ref[...]
    lf = _log_sigmoid(logits) * (1.0 / GLA_NORMALIZER)

    def project(p_out, kv_out, lf_out):
        lf_out[...] = lf
        for j in range(N_MAIN // TN):
            acc = _nt_dot(hn, wbf_ref[j * TN:(j + 1) * TN, :])
            p_out[:, j * TN:(j + 1) * TN] = acc.astype(BF16)
            if j * TN == COL_KA:
                kv_out[...] = acc

    @pl.when(i <= n_p)
    def _():
        project(p_ref, kv_ref, lf_ref)

    @pl.when(i > n_p)
    def _():
        project(ps_ref, kvs_ref, lfs_ref)


def _in_proj(x_meta, x_prompt, x_sample, w_in_t, norm_w, w_a2p, b_a):
    tm = IN_PROJ_TM
    n_p = x_prompt.shape[0] // tm
    n_s = x_sample.shape[0] // tm
    assert x_meta.shape[0] == tm and x_prompt.shape[0] % tm == 0 and x_sample.shape[0] % tm == 0
    assert COL_KA % TN == 0 and len(_in_proj_weight_pieces()) * W_CHUNK == N_MAIN
    m_main = (1 + n_p) * tm
    m_s = n_s * tm

    def main_blk(i):
        return jnp.minimum(i, n_p)

    def s_blk(i):
        return jnp.maximum(i - 1 - n_p, 0)

    def const(i):
        return (0, 0)

    any_spec = pl.BlockSpec(memory_space=pl.ANY)
    buffers = (D_MODEL * N_MAIN * 2 + 2 * D_MODEL * W_CHUNK * 4 + 2 * tm * D_MODEL * 4
               + 2 * 2 * tm * N_MAIN * 2 + 2 * 2 * 2 * tm * GLA_KEY_WIDTH * 4
               + D_MODEL * LANES * 2 + D_MODEL * GLA_RANK * 4 + 2 * LANES * GLA_KEY_WIDTH * 2)
    temps = tm * D_MODEL * (4 + 4 + 2) + 2 * tm * TN * 4 + 2 * tm * GLA_KEY_WIDTH * 4
    vmem_bytes = buffers + temps
    return pl.pallas_call(
        functools.partial(_in_proj_kernel, n_p=n_p, n_s=n_s),
        grid=(1 + n_p + n_s,),
        in_specs=[
            any_spec, any_spec, any_spec, any_spec,
            pl.BlockSpec((1, D_MODEL), const),
            pl.BlockSpec((LANES, GLA_KEY_WIDTH), const),
            pl.BlockSpec((1, GLA_KEY_WIDTH), const),
        ],
        out_specs=[
            pl.BlockSpec((tm, N_MAIN), lambda i: (main_blk(i), 0)),
            pl.BlockSpec((tm, 2 * KV_WIDTH), lambda i: (main_blk(i), 0)),
            pl.BlockSpec((tm, GLA_KEY_WIDTH), lambda i: (main_blk(i), 0)),
            pl.BlockSpec((tm, N_MAIN), lambda i: (s_blk(i), 0)),
            pl.BlockSpec((tm, 2 * KV_WIDTH), lambda i: (s_blk(i), 0)),
            pl.BlockSpec((tm, GLA_KEY_WIDTH), lambda i: (s_blk(i), 0)),
        ],
        out_shape=[
            jax.ShapeDtypeStruct((m_main, N_MAIN), BF16),
            jax.ShapeDtypeStruct((m_main, 2 * KV_WIDTH), F32),
            jax.ShapeDtypeStruct((m_main, GLA_KEY_WIDTH), F32),
            jax.ShapeDtypeStruct((m_s, N_MAIN), BF16),
            jax.ShapeDtypeStruct((m_s, 2 * KV_WIDTH), F32),
            jax.ShapeDtypeStruct((m_s, GLA_KEY_WIDTH), F32),
        ],
        scratch_shapes=[
            pltpu.VMEM((N_MAIN, D_MODEL), BF16),
            pltpu.VMEM((LANES, D_MODEL), BF16),
            pltpu.VMEM((2, W_CHUNK, D_MODEL), F32),
            pltpu.VMEM((GLA_RANK, D_MODEL), F32),
            pltpu.VMEM((2, tm, D_MODEL), F32),
            pltpu.SemaphoreType.DMA((2,)),
            pltpu.SemaphoreType.DMA((1,)),
            pltpu.SemaphoreType.DMA((2,)),
        ],
        compiler_params=pltpu.CompilerParams(
            dimension_semantics=("arbitrary",), vmem_limit_bytes=vmem_bytes),
        name="in_proj",
    )(x_meta, x_prompt, x_sample, w_in_t, norm_w, w_a2p, b_a)


def _t5_bucket(dist):
    max_exact = N_BUCKETS // 2
    d = jnp.maximum(dist, 0)
    ratio = jnp.maximum(d, max_exact).astype(F32) / max_exact
    val = jnp.log(ratio) / math.log(MAX_DISTANCE / max_exact) * (N_BUCKETS - max_exact)
    large = jnp.full(dist.shape, max_exact, jnp.int32)
    for step in range(1, N_BUCKETS - max_exact):
        large = large + (val >= step).astype(jnp.int32)
    return jnp.where(d < max_exact, d, large)


def _swa_prompt_kernel(rb_ref, sink_ref, q_ref, gate_ref, kp_ref, kc_ref, vp_ref, vc_ref,
                       o_ref, tab_ref):
    b = pl.program_id(0)
    n = pl.program_id(1)
    W2 = 2 * WINDOW
    NL = N_KV_HEADS * W2

    @pl.when((b == 0) & (n == 0))
    def _build_bias_tables():
        qi = lax.broadcasted_iota(jnp.int32, (WINDOW, W2), 0)
        kj = lax.broadcasted_iota(jnp.int32, (WINDOW, W2), 1)
        dist = qi - kj + WINDOW
        valid = (dist >= 0) & (dist < WINDOW)
        bucket = _t5_bucket(dist)
        for g in range(GQA_GROUP):
            tab_ref[2, g] = jnp.full((WINDOW, NL), NEG, F32)

        def body(bk, carry):
            hit = (bucket == bk) & valid
            for g in range(GQA_GROUP):
                for h in range(N_KV_HEADS):
                    cur = tab_ref[2, g, :, h * W2:(h + 1) * W2]
                    tab_ref[2, g, :, h * W2:(h + 1) * W2] = jnp.where(hit, rb_ref[bk, h * GQA_GROUP + g], cur)
            return carry

        lax.fori_loop(0, N_BUCKETS, body, 0)
        for v in range(2):
            kvalid = (v - 1) * WINDOW + kj - FRONT_PAD >= 0
            for g in range(GQA_GROUP):
                for h in range(N_KV_HEADS):
                    tab_ref[v, g, :, h * W2:(h + 1) * W2] = jnp.where(
                        kvalid, tab_ref[2, g, :, h * W2:(h + 1) * W2], NEG)

    kk = jnp.concatenate([kp_ref[...], kc_ref[...]], axis=0)
    vv = jnp.concatenate([vp_ref[...], vc_ref[...]], axis=0)
    lane_head = lax.shift_right_logical(lax.broadcasted_iota(jnp.int32, (1, KV_WIDTH), 1), 6)
    zero = jnp.zeros((), BF16)
    kkm = jnp.concatenate([jnp.where(lane_head == h, kk, zero) for h in range(N_KV_HEADS)], axis=0)
    vvm = jnp.concatenate([jnp.where(lane_head == h, vv, zero) for h in range(N_KV_HEADS)], axis=0)
    tv = jnp.minimum(n, 2)
    scale = HEAD_DIM ** -0.5
    scores = []
    for g in range(GQA_GROUP):
        qg = (q_ref[:, g * KV_WIDTH:(g + 1) * KV_WIDTH].astype(F32) * scale).astype(BF16)
        scores.append(_nt_dot(qg, kkm) + tab_ref[tv, g])
    for g in range(GQA_GROUP):
        probs = []
        for h in range(N_KV_HEADS):
            s = scores[g][:, h * W2:(h + 1) * W2]
            sink = sink_ref[h * GQA_GROUP + g]
            m = jnp.maximum(jnp.max(s, axis=-1, keepdims=True), sink)
            e = jnp.exp(s - m)
            den = jnp.sum(e, axis=-1, keepdims=True) + jnp.exp(sink - m)
            probs.append((e * (1.0 / den)).astype(BF16))
        p = jnp.concatenate(probs, axis=1)
        o = jnp.dot(p, vvm, preferred_element_type=F32)
        gate = gate_ref[:, g * KV_WIDTH:(g + 1) * KV_WIDTH].astype(F32)
        o_ref[:, g * KV_WIDTH:(g + 1) * KV_WIDTH] = (o * (gate * _sigmoid(gate))).astype(BF16)


def _swa_prompt(p_all, rel_bias, sinks, n_batch, n_blocks, meta_blk):
    def real(b, m):
        return meta_blk + 1 + b * (n_blocks - 1) + m

    def row(b, n):
        return jnp.where(n == 0, meta_blk, real(b, n - 1))

    def prev(b, n):
        return jnp.where(n <= 1, meta_blk, real(b, n - 2))

    kblk = COL_KA // KV_WIDTH
    vblk = COL_VA // KV_WIDTH
    out_rows = n_batch * (n_blocks - 1) * WINDOW
    return pl.pallas_call(
        _swa_prompt_kernel,
        grid=(n_batch, n_blocks),
        in_specs=[
            pl.BlockSpec(memory_space=pltpu.SMEM),
            pl.BlockSpec(memory_space=pltpu.SMEM),
            pl.BlockSpec((WINDOW, ATT_WIDTH), lambda b, n: (row(b, n), COL_QA // ATT_WIDTH)),
            pl.BlockSpec((WINDOW, ATT_WIDTH), lambda b, n: (row(b, n), COL_GATE_A // ATT_WIDTH)),
            pl.BlockSpec((WINDOW, KV_WIDTH), lambda b, n: (prev(b, n), kblk)),
            pl.BlockSpec((WINDOW, KV_WIDTH), lambda b, n: (row(b, n), kblk)),
            pl.BlockSpec((WINDOW, KV_WIDTH), lambda b, n: (prev(b, n), vblk)),
            pl.BlockSpec((WINDOW, KV_WIDTH), lambda b, n: (row(b, n), vblk)),
        ],
        out_specs=pl.BlockSpec(
            (WINDOW, ATT_WIDTH), lambda b, n: (b * (n_blocks - 1) + jnp.maximum(n - 1, 0), 0)),
        out_shape=jax.ShapeDtypeStruct((out_rows, ATT_WIDTH), BF16),
        scratch_shapes=[pltpu.VMEM((3, GQA_GROUP, WINDOW, N_KV_HEADS * 2 * WINDOW), F32)],
        compiler_params=pltpu.CompilerParams(
            dimension_semantics=("arbitrary", "arbitrary"), vmem_limit_bytes=VMEM_LIMIT),
        name="swa_prompt",
    )(rel_bias, sinks, p_all, p_all, p_all, p_all, p_all, p_all)


def _diag_scores(q3, k3, b3, width):
    nb = q3.shape[0]
    n_heads = q3.shape[2] // GLA_DK
    r_io = lax.broadcasted_iota(jnp.int32, (nb, 8, width), 0)
    t_io = lax.broadcasted_iota(jnp.int32, (nb, 8, width), 1)
    l_io = lax.broadcasted_iota(jnp.int32, (nb, 8, width), 2)
    s_of_lane = l_io - 8 * r_io
    outs = [jnp.zeros((nb, 8, width), F32) for _ in range(n_heads)]
    for s in range(8):
        bs = jnp.broadcast_to(b3[:, s:s + 1, :], b3.shape)
        ks = jnp.broadcast_to(k3[:, s:s + 1, :], k3.shape)
        e = jnp.exp(jnp.minimum(b3 - bs, 0.0))
        p = q3 * e * ks
        for h in range(n_heads):
            col = jnp.sum(p[:, :, h * GLA_DK:(h + 1) * GLA_DK], axis=-1, keepdims=True)
            outs[h] = jnp.where(s_of_lane == s, col, outs[h])
    causal = (s_of_lane >= 0) & (s_of_lane <= t_io) & (s_of_lane < 8)
    return [jnp.where(causal, o, 0.0) for o in outs]


def _gla_finish(o, gn, gate):
    var = jnp.mean(o * o, axis=-1, keepdims=True)
    on = o * lax.rsqrt(var + EPS) * gn
    return (on * (gate * _sigmoid(gate))).astype(BF16)


def _decay_column(row):
    eye = (lax.broadcasted_iota(jnp.int32, (GLA_DK, GLA_DK), 0)
           == lax.broadcasted_iota(jnp.int32, (GLA_DK, GLA_DK), 1))
    return jnp.sum(jnp.where(eye, jnp.broadcast_to(row, (GLA_DK, GLA_DK)), 0.0), axis=1, keepdims=True)


def _gla_prompt_kernel(*refs):
    o_ref, s_ref = refs[-2:]
    gn_ref = refs[-3]
    n_batch = (len(refs) - 3) // 5
    qg_refs, kg_refs, vg_refs, gate_refs, lf_refs = (refs[j:5 * n_batch:5] for j in range(5))
    c = pl.program_id(0)
    C = GLA_CHUNK

    @pl.when(c == 0)
    def _():
        s_ref[...] = jnp.zeros(s_ref.shape, F32)

    row = lax.broadcasted_iota(jnp.int32, (C, 1), 0)
    valid = (c * C + row) >= FRONT_PAD
    tril = (lax.broadcasted_iota(jnp.int32, (C, C), 0)
            >= lax.broadcasted_iota(jnp.int32, (C, C), 1)).astype(F32).astype(BF16)
    t_io = lax.broadcasted_iota(jnp.int32, (C, C), 0)
    s_io = lax.broadcasted_iota(jnp.int32, (C, C), 1)
    n_levels = int(math.log2(C))
    level_masks = [(t_io > s_io) & (lax.shift_right_logical(t_io ^ s_io, sh) == 1) for sh in range(n_levels)]
    sub_io = lax.broadcasted_iota(jnp.int32, (C // 8, 8, 1), 1)
    gn = gn_ref[...]

    heads = [(b, h) for b in range(n_batch) for h in range(GLA_HEADS)]

    def ksl(h):
        return slice(h * GLA_DK, (h + 1) * GLA_DK)

    def vsl(h):
        return slice(h * GLA_DV, (h + 1) * GLA_DV)

    qs, kks, bcs = [], [], []
    for b in range(n_batch):
        qs.append(qg_refs[b][...].astype(F32) * (GLA_DK ** -0.5))
        kks.append(jnp.where(valid, kg_refs[b][...].astype(F32), 0.0))
        g = jnp.where(valid, lf_refs[b][...], 0.0)
        g1 = g.astype(BF16)
        r1 = g - g1.astype(F32)
        g2 = r1.astype(BF16)
        g3 = (r1 - g2.astype(F32)).astype(BF16)
        bcs.append(jnp.dot(tril, g1, preferred_element_type=F32)
                   + jnp.dot(tril, g2, preferred_element_type=F32)
                   + jnp.dot(tril, g3, preferred_element_type=F32))

    qts, kts, yss, qks, b_lasts = [], [], [], [], []
    for b in range(n_batch):
        q, k, bc = qs[b], kks[b], bcs[b]
        b_last = bc[C - 1:C, :]
        b_lasts.append(b_last)
        qts.append((q * jnp.exp(bc)).astype(BF16))
        kts.append((k * jnp.exp(b_last - bc)).astype(BF16))
        b3 = bc.reshape(C // 8, 8, -1)
        ys = []
        for sh in range(n_levels):
            half = 1 << sh
            if half >= 8:
                ref = jnp.concatenate(
                    [jnp.broadcast_to(bc[i + half - 1:i + half, :], (2 * half, bc.shape[1]))
                     for i in range(0, C, 2 * half)], axis=0)
            elif half == 4:
                ref = jnp.broadcast_to(b3[:, 3:4, :], b3.shape).reshape(bc.shape)
            elif half == 2:
                ref = jnp.where(sub_io >= 4, jnp.broadcast_to(b3[:, 5:6, :], b3.shape),
                                jnp.broadcast_to(b3[:, 1:2, :], b3.shape)).reshape(bc.shape)
            else:
                ref = jnp.where((row & 1) == 1, pltpu.roll(bc, 1, axis=0), bc)
            upper = (lax.shift_right_logical(row, sh) & 1) == 1
            ys.append((jnp.where(upper, q, k) * jnp.exp(-jnp.abs(bc - ref))).astype(BF16))
        yss.append(ys)
        qks.append(q * k)

    level_prods, inters = {}, {}
    for b, h in heads:
        level_prods[b, h] = [_nt_dot(yss[b][sh][:, ksl(h)], yss[b][sh][:, ksl(h)]) for sh in range(n_levels)]
    for b, h in heads:
        st = s_ref[b, h]
        inters[b, h] = jnp.dot(qts[b][:, ksl(h)], st.astype(BF16), preferred_element_type=F32)
        s_ref[b, h] = (_decay_column(jnp.exp(b_lasts[b][:, ksl(h)])) * st
                       + _tn_dot(kts[b][:, ksl(h)], vg_refs[b][:, vsl(h)]))

    for b, h in heads:
        a = jnp.where(t_io == s_io, jnp.sum(qks[b][:, ksl(h)], axis=-1, keepdims=True), 0.0)
        for sh in range(n_levels):
            a = jnp.where(level_masks[sh], level_prods[b, h][sh], a)
        o = jnp.dot(a.astype(BF16), vg_refs[b][:, vsl(h)], preferred_element_type=F32) + inters[b, h]
        o_ref[b, :, vsl(h)] = _gla_finish(o, gn, gate_refs[b][:, vsl(h)].astype(F32))


def _gla_prompt(p_all, lf_all, gla_norm_row, n_batch, n_chunks, meta_blk):
    skip = (FRONT_PAD + N_META) // GLA_CHUNK
    out_len = (n_chunks - skip) * GLA_CHUNK
    C = GLA_CHUNK
    per_blk = WINDOW // C

    def chunk(b):
        def f(c):
            return jnp.where(c < skip, meta_blk * per_blk + c, (meta_blk + 1) * per_blk + b * (n_chunks - skip) + c - skip)
        return f

    in_specs = []
    args = []
    for b in range(n_batch):
        f = chunk(b)
        in_specs += [
            pl.BlockSpec((C, GLA_KEY_WIDTH), lambda c, f=f: (f(c), COL_QG // GLA_KEY_WIDTH)),
            pl.BlockSpec((C, GLA_KEY_WIDTH), lambda c, f=f: (f(c), COL_KG // GLA_KEY_WIDTH)),
            pl.BlockSpec((C, GLA_WIDTH), lambda c, f=f: (f(c), COL_VG // GLA_WIDTH)),
            pl.BlockSpec((C, GLA_WIDTH), lambda c, f=f: (f(c), COL_GATE_G // GLA_WIDTH)),
            pl.BlockSpec((C, GLA_KEY_WIDTH), lambda c, f=f: (f(c), 0)),
        ]
        args += [p_all, p_all, p_all, p_all, lf_all]
    return pl.pallas_call(
        _gla_prompt_kernel,
        grid=(n_chunks,),
        in_specs=in_specs + [pl.BlockSpec((1, GLA_DV), lambda c: (0, 0))],
        out_specs=[
            pl.BlockSpec((n_batch, C, GLA_WIDTH), lambda c: (0, jnp.maximum(c - skip, 0), 0)),
            pl.BlockSpec((n_batch, GLA_HEADS, GLA_DK, GLA_DV), lambda c: (0, 0, 0, 0)),
        ],
        out_shape=[
            jax.ShapeDtypeStruct((n_batch, out_len, GLA_WIDTH), BF16),
            jax.ShapeDtypeStruct((n_batch, GLA_HEADS, GLA_DK, GLA_DV), F32),
        ],
        compiler_params=pltpu.CompilerParams(
            dimension_semantics=("arbitrary",), vmem_limit_bytes=VMEM_LIMIT),
        name="gla_prompt",
    )(*args, gla_norm_row)


OUT_PROJ_TM = 256
WO_CHUNK = 512


def _out_proj_kernel(map_ref, mgp_ref, xp_ref, mas_ref, mgs_ref, xs_ref, nw_ref, w_hbm,
                     yp_ref, ys_ref, wbf_ref, stage_ref, wsem, *, n_p):
    i = pl.program_id(0)

    @pl.when(i == 0)
    def _load_weight():
        chunks = [_group_major_pieces(0, d, WO_CHUNK) for d in range(0, ATT_WIDTH, WO_CHUNK)]
        chunks += [[(d, 0, WO_CHUNK)] for d in range(ATT_WIDTH, D_MODEL, WO_CHUNK)]
        _load_rows_as_bf16(w_hbm, chunks, stage_ref, wsem, wbf_ref, WO_CHUNK)

    def finish(ma_ref, mg_ref, x_ref, y_ref):
        o = (jnp.dot(ma_ref[...], wbf_ref[:ATT_WIDTH, :], preferred_element_type=F32)
             + jnp.dot(mg_ref[...], wbf_ref[ATT_WIDTH:, :], preferred_element_type=F32))
        var = jnp.mean(o * o, axis=-1, keepdims=True)
        y_ref[...] = x_ref[...] + o * lax.rsqrt(var + EPS) * nw_ref[...]

    @pl.when(i < n_p)
    def _():
        finish(map_ref, mgp_ref, xp_ref, yp_ref)

    @pl.when(i >= n_p)
    def _():
        finish(mas_ref, mgs_ref, xs_ref, ys_ref)


def _out_proj(ma_p, mg_p, x_p, ma_s, mg_s, x_s, w_out_l, norm_w):
    tm = OUT_PROJ_TM
    n_p = x_p.shape[0] // tm
    n_s = x_s.shape[0] // tm
    assert x_p.shape[0] % tm == 0 and x_s.shape[0] % tm == 0 and ATT_WIDTH + GLA_WIDTH == D_MODEL

    def p_blk(i):
        return (jnp.minimum(i, n_p - 1), 0)

    def s_blk(i):
        return (jnp.maximum(i - n_p, 0), 0)

    row_bytes = ATT_WIDTH * 2 + GLA_WIDTH * 2 + 2 * D_MODEL * 4
    vmem_bytes = (D_MODEL * D_MODEL * 2 + 2 * WO_CHUNK * D_MODEL * 4 + 2 * 2 * tm * row_bytes
                  + 3 * tm * D_MODEL * 4)
    return pl.pallas_call(
        functools.partial(_out_proj_kernel, n_p=n_p),
        grid=(n_p + n_s,),
        in_specs=[
            pl.BlockSpec((tm, ATT_WIDTH), p_blk),
            pl.BlockSpec((tm, GLA_WIDTH), p_blk),
            pl.BlockSpec((tm, D_MODEL), p_blk),
            pl.BlockSpec((tm, ATT_WIDTH), s_blk),
            pl.BlockSpec((tm, GLA_WIDTH), s_blk),
            pl.BlockSpec((tm, D_MODEL), s_blk),
            pl.BlockSpec((1, D_MODEL), lambda i: (0, 0)),
            pl.BlockSpec(memory_space=pl.ANY),
        ],
        out_specs=[
            pl.BlockSpec((tm, D_MODEL), p_blk),
            pl.BlockSpec((tm, D_MODEL), s_blk),
        ],
        out_shape=[
            jax.ShapeDtypeStruct(x_p.shape, F32),
            jax.ShapeDtypeStruct(x_s.shape, F32),
        ],
        scratch_shapes=[
            pltpu.VMEM((D_MODEL, D_MODEL), BF16),
            pltpu.VMEM((2, WO_CHUNK, D_MODEL), F32),
            pltpu.SemaphoreType.DMA((2,)),
        ],
        compiler_params=pltpu.CompilerParams(
            dimension_semantics=("arbitrary",), vmem_limit_bytes=vmem_bytes),
        name="out_proj",
    )(ma_p, mg_p, x_p, ma_s, mg_s, x_s, norm_w, w_out_l)


SWA_SAMPLE_SEQS = 8


def _swa_sample_kernel(rb_ref, sink_ref, q_ref, gate_ref, kvn_ref, ck_ref, cv_ref,
                       o_ref, ko_ref, vo_ref, tabc_ref, tabn_ref, sinkcol_ref, *, t_new):
    step = pl.program_id(0)
    W = ck_ref.shape[1]
    rows = N_HEADS * t_new

    @pl.when(step == 0)
    def _build_tables():
        r = lax.broadcasted_iota(jnp.int32, (rows, W), 0)
        qi = r & (t_new - 1)
        kj = lax.broadcasted_iota(jnp.int32, (rows, W), 1)
        dist_c = qi + W - kj
        valid_c = (dist_c >= 0) & (dist_c < WINDOW)
        bucket_c = _t5_bucket(dist_c)
        rn = lax.broadcasted_iota(jnp.int32, (rows, t_new), 0)
        qn = rn & (t_new - 1)
        jn = lax.broadcasted_iota(jnp.int32, (rows, t_new), 1)
        dist_n = qn - jn
        valid_n = (dist_n >= 0) & (dist_n < WINDOW)
        bucket_n = _t5_bucket(dist_n)
        head_c = lax.shift_right_logical(r, int(math.log2(t_new)))
        head_n = lax.shift_right_logical(rn, int(math.log2(t_new)))
        head_s = lax.shift_right_logical(lax.broadcasted_iota(jnp.int32, (rows, 1), 0), int(math.log2(t_new)))
        tabc_ref[...] = jnp.full((rows, W), NEG, F32)
        tabn_ref[...] = jnp.full((rows, t_new), NEG, F32)
        sinkcol_ref[...] = jnp.zeros((rows, 1), F32)
        for g in range(GQA_GROUP):
            for h in range(N_KV_HEADS):
                hd = h * GQA_GROUP + g
                blk = g * N_KV_HEADS + h
                sinkcol_ref[...] = jnp.where(head_s == blk, sink_ref[hd], sinkcol_ref[...])

        def body(bk, carry):
            for g in range(GQA_GROUP):
                for h in range(N_KV_HEADS):
                    hd = h * GQA_GROUP + g
                    blk = g * N_KV_HEADS + h
                    val = rb_ref[bk, hd]
                    tabc_ref[...] = jnp.where((bucket_c == bk) & valid_c & (head_c == blk), val, tabc_ref[...])
                    tabn_ref[...] = jnp.where((bucket_n == bk) & valid_n & (head_n == blk), val, tabn_ref[...])
            return carry

        lax.fori_loop(0, N_BUCKETS, body, 0)

    lane_head = lax.shift_right_logical(lax.broadcasted_iota(jnp.int32, (1, KV_WIDTH), 1), 6)
    scale = HEAD_DIM ** -0.5
    sink = sinkcol_ref[...]
    seqs = range(q_ref.shape[0])
    scores = []
    for s in seqs:
        q = q_ref[s].astype(F32) * scale
        pieces = []
        for g in range(GQA_GROUP):
            qg = q[:, g * KV_WIDTH:(g + 1) * KV_WIDTH]
            for h in range(N_KV_HEADS):
                pieces.append(jnp.where(lane_head == h, qg, 0.0))
        lhs = jnp.concatenate(pieces, axis=0).astype(BF16)
        sc = _nt_dot(lhs, ck_ref[s].astype(BF16)) + tabc_ref[...]
        sn = _nt_dot(lhs, kvn_ref[s, :, :KV_WIDTH].astype(BF16)) + tabn_ref[...]
        scores.append((sc, sn))
    probs = []
    for s in seqs:
        sc, sn = scores[s]
        m = jnp.maximum(jnp.maximum(jnp.max(sc, axis=-1, keepdims=True),
                                    jnp.max(sn, axis=-1, keepdims=True)), sink)
        ec = jnp.exp(sc - m)
        en = jnp.exp(sn - m)
        den = (jnp.sum(ec, axis=-1, keepdims=True) + jnp.sum(en, axis=-1, keepdims=True)
               + jnp.exp(sink - m))
        inv = 1.0 / den
        probs.append(((ec * inv).astype(BF16), (en * inv).astype(BF16)))
    for s in seqs:
        pc, pn = probs[s]
        o = (jnp.dot(pc, cv_ref[s].astype(BF16), preferred_element_type=F32)
             + jnp.dot(pn, kvn_ref[s, :, KV_WIDTH:].astype(BF16), preferred_element_type=F32))
        outs = []
        for g in range(GQA_GROUP):
            acc = jnp.zeros((t_new, KV_WIDTH), F32)
            for h in range(N_KV_HEADS):
                blk = g * N_KV_HEADS + h
                acc = jnp.where(lane_head == h, o[blk * t_new:(blk + 1) * t_new, :], acc)
            outs.append(acc)
        att = jnp.concatenate(outs, axis=1)
        gate = gate_ref[s].astype(F32)
        o_ref[s] = (att * (gate * _sigmoid(gate))).astype(BF16)
        ko_ref[s, :W - t_new, :] = ck_ref[s, t_new:, :]
        ko_ref[s, W - t_new:, :] = kvn_ref[s, :, :KV_WIDTH]
        vo_ref[s, :W - t_new, :] = cv_ref[s, t_new:, :]
        vo_ref[s, W - t_new:, :] = kvn_ref[s, :, KV_WIDTH:]


def _swa_sample(p3, kv3, cache_k, cache_v, rel_bias, sinks):
    db, t_new, _ = p3.shape
    w = cache_k.shape[1]
    assert t_new & (t_new - 1) == 0 and w >= t_new and w <= WINDOW
    sb = SWA_SAMPLE_SEQS
    rows = N_HEADS * t_new
    return pl.pallas_call(
        functools.partial(_swa_sample_kernel, t_new=t_new),
        grid=(db // sb,),
        in_specs=[
            pl.BlockSpec(memory_space=pltpu.SMEM),
            pl.BlockSpec(memory_space=pltpu.SMEM),
            pl.BlockSpec((sb, t_new, ATT_WIDTH), lambda i: (i, 0, COL_QA // ATT_WIDTH)),
            pl.BlockSpec((sb, t_new, ATT_WIDTH), lambda i: (i, 0, COL_GATE_A // ATT_WIDTH)),
            pl.BlockSpec((sb, t_new, 2 * KV_WIDTH), lambda i: (i, 0, 0)),
            pl.BlockSpec((sb, w, KV_WIDTH), lambda i: (i, 0, 0)),
            pl.BlockSpec((sb, w, KV_WIDTH), lambda i: (i, 0, 0)),
        ],
        out_specs=[
            pl.BlockSpec((sb, t_new, ATT_WIDTH), lambda i: (i, 0, 0)),
            pl.BlockSpec((sb, w, KV_WIDTH), lambda i: (i, 0, 0)),
            pl.BlockSpec((sb, w, KV_WIDTH), lambda i: (i, 0, 0)),
        ],
        out_shape=[
            jax.ShapeDtypeStruct((db, t_new, ATT_WIDTH), BF16),
            jax.ShapeDtypeStruct((db, w, KV_WIDTH), F32),
            jax.ShapeDtypeStruct((db, w, KV_WIDTH), F32),
        ],
        scratch_shapes=[
            pltpu.VMEM((rows, w), F32),
            pltpu.VMEM((rows, t_new), F32),
            pltpu.VMEM((rows, 1), F32),
        ],
        compiler_params=pltpu.CompilerParams(
            dimension_semantics=("arbitrary",), vmem_limit_bytes=VMEM_LIMIT),
        name="swa_sample",
    )(rel_bias, sinks, p3, p3, kv3, cache_k, cache_v)


GLA_SAMPLE_SEQS = 4


def _cumsum_rows8(g):
    row = lax.broadcasted_iota(jnp.int32, g.shape, 0)
    out = g
    for sh in (1, 2, 4):
        out = out + jnp.where(row >= sh, pltpu.roll(out, sh, axis=0), 0.0)
    return out


def _gla_sample_kernel(qg_ref, kg_ref, vg_ref, gate_ref, lf_ref, gn_ref, st_ref, o_ref, so_ref):
    T = qg_ref.shape[1]
    gn = gn_ref[...]
    t_col = lax.broadcasted_iota(jnp.int32, (T, 1), 0)
    zpad = jnp.zeros((8, GLA_DK), F32)
    for s in range(qg_ref.shape[0]):
        q = qg_ref[s].astype(F32) * (GLA_DK ** -0.5)
        k = kg_ref[s].astype(F32)
        bc = _cumsum_rows8(lf_ref[s])
        b_last = bc[T - 1:T, :]
        qt = q * jnp.exp(bc)
        kt = k * jnp.exp(b_last - bc)
        v_all = vg_ref[s].astype(F32)
        cols = [[None] * T for _ in range(GLA_HEADS)]
        for j in range(T):
            bs = jnp.broadcast_to(bc[j:j + 1, :], bc.shape)
            ks_ = jnp.broadcast_to(k[j:j + 1, :], k.shape)
            p = q * jnp.exp(jnp.minimum(bc - bs, 0.0)) * ks_
            for h in range(GLA_HEADS):
                col = jnp.sum(p[:, h * GLA_DK:(h + 1) * GLA_DK], axis=-1, keepdims=True)
                cols[h][j] = jnp.where(t_col >= j, col, 0.0)
        for h in range(GLA_HEADS):
            ks = slice(h * GLA_DK, (h + 1) * GLA_DK)
            vs = slice(h * GLA_DV, (h + 1) * GLA_DV)
            v = v_all[:, vs]
            st = st_ref[s, h]
            qt16 = jnp.concatenate([qt[:, ks], zpad], axis=0).astype(BF16)
            o = jnp.dot(qt16, st.astype(BF16), preferred_element_type=F32)[:T, :]
            for j in range(T):
                o = o + cols[h][j] * v[j:j + 1, :]
            kt16 = jnp.concatenate([kt[:, ks], zpad], axis=0).astype(BF16)
            v16 = jnp.concatenate([v, jnp.zeros((8, GLA_DV), F32)], axis=0).astype(BF16)
            so_ref[s, h] = _decay_column(jnp.exp(b_last[:, ks])) * st + _tn_dot(kt16, v16)
            o_ref[s, :, vs] = _gla_finish(o, gn, gate_ref[s, :, vs].astype(F32))


def _gla_sample(p3, lf3, state, gla_norm_row):
    db, t_new, _ = p3.shape
    assert t_new == 8
    sb = GLA_SAMPLE_SEQS
    return pl.pallas_call(
        _gla_sample_kernel,
        grid=(db // sb,),
        in_specs=[
            pl.BlockSpec((sb, t_new, GLA_KEY_WIDTH), lambda i: (i, 0, COL_QG // GLA_KEY_WIDTH)),
            pl.BlockSpec((sb, t_new, GLA_KEY_WIDTH), lambda i: (i, 0, COL_KG // GLA_KEY_WIDTH)),
            pl.BlockSpec((sb, t_new, GLA_WIDTH), lambda i: (i, 0, COL_VG // GLA_WIDTH)),
            pl.BlockSpec((sb, t_new, GLA_WIDTH), lambda i: (i, 0, COL_GATE_G // GLA_WIDTH)),
            pl.BlockSpec((sb, t_new, GLA_KEY_WIDTH), lambda i: (i, 0, 0)),
            pl.BlockSpec((1, GLA_DV), lambda i: (0, 0)),
            pl.BlockSpec((sb, GLA_HEADS, GLA_DK, GLA_DV), lambda i: (i, 0, 0, 0)),
        ],
        out_specs=[
            pl.BlockSpec((sb, t_new, GLA_WIDTH), lambda i: (i, 0, 0)),
            pl.BlockSpec((sb, GLA_HEADS, GLA_DK, GLA_DV), lambda i: (i, 0, 0, 0)),
        ],
        out_shape=[
            jax.ShapeDtypeStruct((db, t_new, GLA_WIDTH), BF16),
            jax.ShapeDtypeStruct((db, GLA_HEADS, GLA_DK, GLA_DV), F32),
        ],
        compiler_params=pltpu.CompilerParams(
            dimension_semantics=("arbitrary",), vmem_limit_bytes=VMEM_LIMIT),
        name="gla_sample",
    )(p3, p3, p3, p3, lf3, gla_norm_row, state)


def kernel(x_prompt, x_sample, cache_k_win, cache_v_win, state_gla, meta_tokens, rel_bias,
           norm_pre, norm_post, w_in, w_a2, b_a, attn_sinks, gla_norm, w_out):
    n_batch, seq, _ = x_prompt.shape
    db, t_new, _ = x_sample.shape
    depth = w_in.shape[0]
    assert depth == 1
    l = 0
    lp = seq + WINDOW
    n_blocks = lp // WINDOW
    n_chunks = lp // GLA_CHUNK
    w_cache = cache_k_win.shape[2]

    w_in_t = w_in[l].T
    assert sum(IN_SIZES[:-1]) == N_MAIN
    w_a2p = jnp.pad(w_a2[l], ((0, LANES - GLA_RANK), (0, 0))).astype(BF16)
    npre = norm_pre[l][None, :]
    npost = norm_post[l][None, :]
    ba = b_a[l][None, :]
    gn = gla_norm[l][None, :]
    sinks = attn_sinks[l]

    tm = IN_PROJ_TM
    x_meta = jnp.concatenate([jnp.zeros((tm - N_META, D_MODEL), x_prompt.dtype),
                              meta_tokens.astype(x_prompt.dtype)], axis=0)
    p_all, kv_all, lf_all, ps, kvs, lfs = _in_proj(
        x_meta, x_prompt.reshape(n_batch * seq, D_MODEL), x_sample.reshape(db * t_new, D_MODEL),
        w_in_t, npre, w_a2p, ba)
    meta_blk = tm // WINDOW - 1

    mixed_a = _swa_prompt(p_all, rel_bias, sinks, n_batch, n_blocks, meta_blk)
    mixed_g, s_fin = _gla_prompt(p_all, lf_all, gn, n_batch, n_chunks, meta_blk)
    kv_p = jnp.stack([kv_all[tm + (b + 1) * seq - WINDOW:tm + (b + 1) * seq] for b in range(n_batch)])
    k_win_p = kv_p[..., :KV_WIDTH].reshape(1, n_batch, WINDOW, N_KV_HEADS, HEAD_DIM)
    v_win_p = kv_p[..., KV_WIDTH:].reshape(1, n_batch, WINDOW, N_KV_HEADS, HEAD_DIM)
    gla_p = s_fin[None]

    ps3 = ps.reshape(db, t_new, N_MAIN)
    mixed_as, k_win_s, v_win_s = _swa_sample(
        ps3, kvs.reshape(db, t_new, 2 * KV_WIDTH),
        cache_k_win[l].reshape(db, w_cache, KV_WIDTH), cache_v_win[l].reshape(db, w_cache, KV_WIDTH),
        rel_bias, sinks)
    mixed_gs, s_new = _gla_sample(ps3, lfs.reshape(db, t_new, GLA_KEY_WIDTH), state_gla[l], gn)

    y_prompt, y_sample = _out_proj(
        mixed_a, mixed_g.reshape(n_batch * seq, GLA_WIDTH), x_prompt.reshape(n_batch * seq, D_MODEL),
        mixed_as.reshape(db * t_new, ATT_WIDTH), mixed_gs.reshape(db * t_new, GLA_WIDTH),
        x_sample.reshape(db * t_new, D_MODEL), w_out[l], npost)
    y_prompt = y_prompt.reshape(n_batch, seq, D_MODEL)
    y_sample = y_sample.reshape(db, t_new, D_MODEL)
    k_win_s = k_win_s.reshape(1, db, w_cache, N_KV_HEADS, HEAD_DIM)
    v_win_s = v_win_s.reshape(1, db, w_cache, N_KV_HEADS, HEAD_DIM)

    return (y_prompt, y_sample, k_win_p, v_win_p, gla_p, k_win_s, v_win_s, s_new[None])
```

```python
import functools
import math

import jax
import jax.numpy as jnp
from jax import lax
from jax.experimental import pallas as pl
from jax.experimental.pallas import tpu as pltpu

F32 = jnp.float32
BF16 = jnp.bfloat16

D_MODEL = 2048
N_META = 16
ATT_WIDTH = 1024
HEAD_DIM = 64
N_HEADS = 16
N_KV_HEADS = 4
GQA_GROUP = 4
KV_WIDTH = 256
WINDOW = 128
N_BUCKETS = 32
MAX_DISTANCE = 128
GLA_WIDTH = 1024
GLA_HEADS = 4
GLA_DV = 256
GLA_DK = 128
GLA_KEY_WIDTH = 512
GLA_RANK = 16
GLA_NORMALIZER = 16.0
GLA_CHUNK = 64
FRONT_PAD = WINDOW - N_META
EPS = 1e-6
PAST_LEN = 8192

IN_SIZES = (ATT_WIDTH, KV_WIDTH, KV_WIDTH, ATT_WIDTH,
            GLA_KEY_WIDTH, GLA_KEY_WIDTH, GLA_WIDTH, GLA_WIDTH, GLA_RANK)

N_MAIN = 5632
TN = 512
COL_QA = 0
COL_GATE_A = 1024
COL_QG = 2048
COL_KG = 2560
COL_VG = 3072
COL_GATE_G = 4096
COL_KA = 5120
COL_VA = 5376
LANES = 128
NEG = -1e30

VMEM_LIMIT = 48 * 1024 * 1024


def _sigmoid(x):
    return 1.0 / (1.0 + jnp.exp(-x))


def _log_sigmoid(x):
    return jnp.minimum(x, 0.0) - jnp.log1p(jnp.exp(-jnp.abs(x)))


def _nt_dot(a, b):
    return lax.dot_general(a, b, (((1,), (1,)), ((), ())), preferred_element_type=F32)


def _tn_dot(a, b):
    return lax.dot_general(a, b, (((0,), (0,)), ((), ())), preferred_element_type=F32)


def _group_major_pieces(src0, dst0, n_rows):
    pieces = []
    for r in range(dst0, dst0 + n_rows, HEAD_DIM):
        g, h = divmod(r // HEAD_DIM, N_KV_HEADS)
        pieces.append((src0 + (h * GQA_GROUP + g) * HEAD_DIM, r - dst0, HEAD_DIM))
    return pieces


def _load_rows_as_bf16(src_hbm, chunk_pieces, stage_ref, sem, dst_ref, chunk_rows):
    def copies(c):
        return [pltpu.make_async_copy(src_hbm.at[pl.ds(src, n), :], stage_ref.at[c % 2, pl.ds(dst, n), :],
                                      sem.at[c % 2]) for src, dst, n in chunk_pieces[c]]

    for cp in copies(0):
        cp.start()
    for c in range(len(chunk_pieces)):
        if c + 1 < len(chunk_pieces):
            for cp in copies(c + 1):
                cp.start()
        for cp in copies(c):
            cp.wait()
        dst_ref[c * chunk_rows:(c + 1) * chunk_rows, :] = stage_ref[c % 2].astype(BF16)


W_CHUNK = 256
IN_PROJ_TM = 256
GATE_HEAD_AFTER_TILES = 3


def _in_proj_weight_pieces():
    src_off = dict(zip(("q_a", "k_a", "v_a", "gate_a", "q_g", "k_g", "v_g", "gate_g"),
                       (sum(IN_SIZES[:i]) for i in range(8))))
    chunks = []
    for name in ("q_a", "gate_a"):
        for d in range(0, ATT_WIDTH, W_CHUNK):
            chunks.append(_group_major_pieces(src_off[name], d, W_CHUNK))
    for name, width in (("q_g", GLA_KEY_WIDTH), ("k_g", GLA_KEY_WIDTH), ("v_g", GLA_WIDTH),
                        ("gate_g", GLA_WIDTH), ("k_a", KV_WIDTH), ("v_a", KV_WIDTH)):
        for d in range(0, width, W_CHUNK):
            chunks.append([(src_off[name] + d, 0, W_CHUNK)])
    return chunks


def _in_proj_kernel(xm_hbm, xp_hbm, xs_hbm, wt_hbm, nw_ref, wa2_ref, ba_ref,
                    p_ref, kv_ref, lf_ref,
                    wbf_ref, wz_ref, stage_ref, zstage_ref, xbuf_ref, hn0_ref, hn1_ref, lf0_ref, lf1_ref,
                    wsem, zsem, xsem, *, n_p, n_s):
    i = pl.program_id(0)
    tm = xbuf_ref.shape[1]
    n_steps = 1 + n_p + n_s

    def x_copy(step, slot, fn):
        @pl.when(step == 0)
        def _():
            fn(pltpu.make_async_copy(xm_hbm, xbuf_ref.at[slot], xsem.at[slot]))

        @pl.when((step >= 1) & (step <= n_p))
        def _():
            fn(pltpu.make_async_copy(xp_hbm.at[pl.ds((step - 1) * tm, tm), :], xbuf_ref.at[slot], xsem.at[slot]))

        @pl.when(step > n_p)
        def _():
            fn(pltpu.make_async_copy(xs_hbm.at[pl.ds((step - 1 - n_p) * tm, tm), :], xbuf_ref.at[slot],
                                     xsem.at[slot]))

    slot = i % 2
    nxt = 1 - slot

    def rmsnorm_to(x_slot, hn_out):
        x = xbuf_ref[x_slot]
        var = jnp.mean(x * x, axis=-1, keepdims=True)
        hn_out[...] = (x * lax.rsqrt(var + EPS) * nw_ref[...]).astype(BF16)

    def gate_head_to(hn_in, lf_out):
        z = _nt_dot(hn_in[...], wz_ref[...])
        logits = jnp.dot(z.astype(BF16), wa2_ref[...], preferred_element_type=F32) + ba_ref[...]
        lf_out[...] = _log_sigmoid(logits) * (1.0 / GLA_NORMALIZER)

    @pl.when(i == 0)
    def _first_step():
        x_copy(0, 0, lambda cp: cp.start())
        x_copy(1, 1, lambda cp: cp.start())

        z_copy = pltpu.make_async_copy(wt_hbm.at[pl.ds(N_MAIN, GLA_RANK), :], zstage_ref, zsem.at[0])
        z_copy.start()
        wz_ref[...] = jnp.zeros(wz_ref.shape, BF16)
        z_copy.wait()
        wz_ref[:GLA_RANK, :] = zstage_ref[...].astype(BF16)
        _load_rows_as_bf16(wt_hbm, _in_proj_weight_pieces(), stage_ref, wsem, wbf_ref, W_CHUNK)
        x_copy(0, 0, lambda cp: cp.wait())
        rmsnorm_to(0, hn0_ref)
        gate_head_to(hn0_ref, lf0_ref)

    @pl.when(i + 2 < n_steps)
    def _prefetch():
        x_copy(i + 2, slot, lambda cp: cp.start())

    @pl.when(i + 1 < n_steps)
    def _():
        x_copy(i + 1, nxt, lambda cp: cp.wait())

    def project(hn_cur, lf_cur, hn_nxt, lf_nxt):
        lf_ref[...] = lf_cur[...]
        hn = hn_cur[...]
        rmsnorm_to(nxt, hn_nxt)
        for j in range(N_MAIN // TN):
            if j == GATE_HEAD_AFTER_TILES:
                gate_head_to(hn_nxt, lf_nxt)
            acc = _nt_dot(hn, wbf_ref[j * TN:(j + 1) * TN, :])
            p_ref[:, j * TN:(j + 1) * TN] = acc.astype(BF16)
            if j * TN == COL_KA:
                kv_ref[...] = acc

    @pl.when(slot == 0)
    def _():
        project(hn0_ref, lf0_ref, hn1_ref, lf1_ref)

    @pl.when(slot == 1)
    def _():
        project(hn1_ref, lf1_ref, hn0_ref, lf0_ref)


def _in_proj(x_meta, x_prompt, x_sample, w_in_t, norm_w, w_a2p, b_a):
    tm = IN_PROJ_TM
    n_p = x_prompt.shape[0] // tm
    n_s = x_sample.shape[0] // tm
    assert x_meta.shape[0] == tm and x_prompt.shape[0] % tm == 0 and x_sample.shape[0] % tm == 0
    assert COL_KA % TN == 0 and len(_in_proj_weight_pieces()) * W_CHUNK == N_MAIN
    m_all = (1 + n_p + n_s) * tm

    def const(i):
        return (0, 0)

    any_spec = pl.BlockSpec(memory_space=pl.ANY)
    buffers = (D_MODEL * N_MAIN * 2 + 2 * D_MODEL * W_CHUNK * 4 + 2 * tm * D_MODEL * 4
               + 2 * tm * N_MAIN * 2 + 2 * 2 * tm * GLA_KEY_WIDTH * 4
               + D_MODEL * LANES * 2 + D_MODEL * GLA_RANK * 4 + 2 * LANES * GLA_KEY_WIDTH * 2)
    buffers += 2 * tm * D_MODEL * 2 + 2 * tm * GLA_KEY_WIDTH * 4
    temps = tm * D_MODEL * (4 + 4) + 2 * tm * TN * 4 + 2 * tm * GLA_KEY_WIDTH * 4
    vmem_bytes = buffers + temps
    return pl.pallas_call(
        functools.partial(_in_proj_kernel, n_p=n_p, n_s=n_s),
        grid=(1 + n_p + n_s,),
        in_specs=[
            any_spec, any_spec, any_spec, any_spec,
            pl.BlockSpec((1, D_MODEL), const),
            pl.BlockSpec((LANES, GLA_KEY_WIDTH), const),
            pl.BlockSpec((1, GLA_KEY_WIDTH), const),
        ],
        out_specs=[
            pl.BlockSpec((tm, N_MAIN), lambda i: (i, 0)),
            pl.BlockSpec((tm, 2 * KV_WIDTH), lambda i: (i, 0)),
            pl.BlockSpec((tm, GLA_KEY_WIDTH), lambda i: (i, 0)),
        ],
        out_shape=[
            jax.ShapeDtypeStruct((m_all, N_MAIN), BF16),
            jax.ShapeDtypeStruct((m_all, 2 * KV_WIDTH), F32),
            jax.ShapeDtypeStruct((m_all, GLA_KEY_WIDTH), F32),
        ],
        scratch_shapes=[
            pltpu.VMEM((N_MAIN, D_MODEL), BF16),
            pltpu.VMEM((LANES, D_MODEL), BF16),
            pltpu.VMEM((2, W_CHUNK, D_MODEL), F32),
            pltpu.VMEM((GLA_RANK, D_MODEL), F32),
            pltpu.VMEM((2, tm, D_MODEL), F32),
            pltpu.VMEM((tm, D_MODEL), BF16),
            pltpu.VMEM((tm, D_MODEL), BF16),
            pltpu.VMEM((tm, GLA_KEY_WIDTH), F32),
            pltpu.VMEM((tm, GLA_KEY_WIDTH), F32),
            pltpu.SemaphoreType.DMA((2,)),
            pltpu.SemaphoreType.DMA((1,)),
            pltpu.SemaphoreType.DMA((2,)),
        ],
        compiler_params=pltpu.CompilerParams(
            dimension_semantics=("arbitrary",), vmem_limit_bytes=vmem_bytes),
        name="in_proj",
    )(x_meta, x_prompt, x_sample, w_in_t, norm_w, w_a2p, b_a)


def _t5_bucket(dist):
    max_exact = N_BUCKETS // 2
    d = jnp.maximum(dist, 0)
    ratio = jnp.maximum(d, max_exact).astype(F32) / max_exact
    val = jnp.log(ratio) / math.log(MAX_DISTANCE / max_exact) * (N_BUCKETS - max_exact)
    large = jnp.full(dist.shape, max_exact, jnp.int32)
    for step in range(1, N_BUCKETS - max_exact):
        large = large + (val >= step).astype(jnp.int32)
    return jnp.where(d < max_exact, d, large)


def _swa_prompt_kernel(rb_ref, sink_ref, q_ref, gate_ref, kp_ref, kc_ref, vp_ref, vc_ref,
                       o_ref, tab_ref):
    b = pl.program_id(0)
    n = pl.program_id(1)
    W2 = 2 * WINDOW
    NL = N_KV_HEADS * W2

    @pl.when((b == 0) & (n == 0))
    def _build_bias_tables():
        qi = lax.broadcasted_iota(jnp.int32, (WINDOW, W2), 0)
        kj = lax.broadcasted_iota(jnp.int32, (WINDOW, W2), 1)
        dist = qi - kj + WINDOW
        valid = (dist >= 0) & (dist < WINDOW)
        bucket = _t5_bucket(dist)
        for g in range(GQA_GROUP):
            tab_ref[2, g] = jnp.full((WINDOW, NL), NEG, F32)

        def body(bk, carry):
            hit = (bucket == bk) & valid
            for g in range(GQA_GROUP):
                for h in range(N_KV_HEADS):
                    cur = tab_ref[2, g, :, h * W2:(h + 1) * W2]
                    tab_ref[2, g, :, h * W2:(h + 1) * W2] = jnp.where(hit, rb_ref[bk, h * GQA_GROUP + g], cur)
            return carry

        lax.fori_loop(0, N_BUCKETS, body, 0)
        for v in range(2):
            kvalid = (v - 1) * WINDOW + kj - FRONT_PAD >= 0
            for g in range(GQA_GROUP):
                for h in range(N_KV_HEADS):
                    tab_ref[v, g, :, h * W2:(h + 1) * W2] = jnp.where(
                        kvalid, tab_ref[2, g, :, h * W2:(h + 1) * W2], NEG)

    kk = jnp.concatenate([kp_ref[...], kc_ref[...]], axis=0)
    vv = jnp.concatenate([vp_ref[...], vc_ref[...]], axis=0)
    lane_head = lax.shift_right_logical(lax.broadcasted_iota(jnp.int32, (1, KV_WIDTH), 1), 6)
    zero = jnp.zeros((), BF16)
    kkm = jnp.concatenate([jnp.where(lane_head == h, kk, zero) for h in range(N_KV_HEADS)], axis=0)
    vvm = jnp.concatenate([jnp.where(lane_head == h, vv, zero) for h in range(N_KV_HEADS)], axis=0)
    tv = jnp.minimum(n, 2)
    scale = HEAD_DIM ** -0.5
    scores = []
    for g in range(GQA_GROUP):
        qg = (q_ref[:, g * KV_WIDTH:(g + 1) * KV_WIDTH].astype(F32) * scale).astype(BF16)
        scores.append(_nt_dot(qg, kkm) + tab_ref[tv, g])
    for g in range(GQA_GROUP):
        probs = []
        for h in range(N_KV_HEADS):
            s = scores[g][:, h * W2:(h + 1) * W2]
            sink = sink_ref[h * GQA_GROUP + g]
            m = jnp.maximum(jnp.max(s, axis=-1, keepdims=True), sink)
            e = jnp.exp(s - m)
            den = jnp.sum(e, axis=-1, keepdims=True) + jnp.exp(sink - m)
            probs.append((e * (1.0 / den)).astype(BF16))
        p = jnp.concatenate(probs, axis=1)
        o = jnp.dot(p, vvm, preferred_element_type=F32)
        gate = gate_ref[:, g * KV_WIDTH:(g + 1) * KV_WIDTH].astype(F32)
        o_ref[:, g * KV_WIDTH:(g + 1) * KV_WIDTH] = (o * (gate * _sigmoid(gate))).astype(BF16)


def _swa_prompt(p_all, rel_bias, sinks, n_batch, n_blocks, meta_blk):
    def real(b, m):
        return meta_blk + 1 + b * (n_blocks - 1) + m

    def row(b, n):
        return jnp.where(n == 0, meta_blk, real(b, n - 1))

    def prev(b, n):
        return jnp.where(n <= 1, meta_blk, real(b, n - 2))

    kblk = COL_KA // KV_WIDTH
    vblk = COL_VA // KV_WIDTH
    out_rows = n_batch * (n_blocks - 1) * WINDOW
    return pl.pallas_call(
        _swa_prompt_kernel,
        grid=(n_batch, n_blocks),
        in_specs=[
            pl.BlockSpec(memory_space=pltpu.SMEM),
            pl.BlockSpec(memory_space=pltpu.SMEM),
            pl.BlockSpec((WINDOW, ATT_WIDTH), lambda b, n: (row(b, n), COL_QA // ATT_WIDTH)),
            pl.BlockSpec((WINDOW, ATT_WIDTH), lambda b, n: (row(b, n), COL_GATE_A // ATT_WIDTH)),
            pl.BlockSpec((WINDOW, KV_WIDTH), lambda b, n: (prev(b, n), kblk)),
            pl.BlockSpec((WINDOW, KV_WIDTH), lambda b, n: (row(b, n), kblk)),
            pl.BlockSpec((WINDOW, KV_WIDTH), lambda b, n: (prev(b, n), vblk)),
            pl.BlockSpec((WINDOW, KV_WIDTH), lambda b, n: (row(b, n), vblk)),
        ],
        out_specs=pl.BlockSpec(
            (WINDOW, ATT_WIDTH), lambda b, n: (b * (n_blocks - 1) + jnp.maximum(n - 1, 0), 0)),
        out_shape=jax.ShapeDtypeStruct((out_rows, ATT_WIDTH), BF16),
        scratch_shapes=[pltpu.VMEM((3, GQA_GROUP, WINDOW, N_KV_HEADS * 2 * WINDOW), F32)],
        compiler_params=pltpu.CompilerParams(
            dimension_semantics=("arbitrary", "arbitrary"), vmem_limit_bytes=VMEM_LIMIT),
        name="swa_prompt",
    )(rel_bias, sinks, p_all, p_all, p_all, p_all, p_all, p_all)


def _diag_scores(q3, k3, b3, width):
    nb = q3.shape[0]
    n_heads = q3.shape[2] // GLA_DK
    r_io = lax.broadcasted_iota(jnp.int32, (nb, 8, width), 0)
    t_io = lax.broadcasted_iota(jnp.int32, (nb, 8, width), 1)
    l_io = lax.broadcasted_iota(jnp.int32, (nb, 8, width), 2)
    s_of_lane = l_io - 8 * r_io
    outs = [jnp.zeros((nb, 8, width), F32) for _ in range(n_heads)]
    for s in range(8):
        bs = jnp.broadcast_to(b3[:, s:s + 1, :], b3.shape)
        ks = jnp.broadcast_to(k3[:, s:s + 1, :], k3.shape)
        e = jnp.exp(jnp.minimum(b3 - bs, 0.0))
        p = q3 * e * ks
        for h in range(n_heads):
            col = jnp.sum(p[:, :, h * GLA_DK:(h + 1) * GLA_DK], axis=-1, keepdims=True)
            outs[h] = jnp.where(s_of_lane == s, col, outs[h])
    causal = (s_of_lane >= 0) & (s_of_lane <= t_io) & (s_of_lane < 8)
    return [jnp.where(causal, o, 0.0) for o in outs]


def _gla_finish(o, gn, gate):
    var = jnp.mean(o * o, axis=-1, keepdims=True)
    on = o * lax.rsqrt(var + EPS) * gn
    return (on * (gate * _sigmoid(gate))).astype(BF16)


def _decay_column(row):
    eye = (lax.broadcasted_iota(jnp.int32, (GLA_DK, GLA_DK), 0)
           == lax.broadcasted_iota(jnp.int32, (GLA_DK, GLA_DK), 1))
    return jnp.sum(jnp.where(eye, jnp.broadcast_to(row, (GLA_DK, GLA_DK)), 0.0), axis=1, keepdims=True)


def _gla_prompt_kernel(*refs):
    o_ref, s_ref = refs[-2:]
    gn_ref = refs[-3]
    n_batch = (len(refs) - 3) // 5
    qg_refs, kg_refs, vg_refs, gate_refs, lf_refs = (refs[j:5 * n_batch:5] for j in range(5))
    c = pl.program_id(0)
    C = GLA_CHUNK

    @pl.when(c == 0)
    def _():
        s_ref[...] = jnp.zeros(s_ref.shape, F32)

    row = lax.broadcasted_iota(jnp.int32, (C, 1), 0)
    valid = (c * C + row) >= FRONT_PAD
    tril = (lax.broadcasted_iota(jnp.int32, (C, C), 0)
            >= lax.broadcasted_iota(jnp.int32, (C, C), 1)).astype(F32).astype(BF16)
    t_io = lax.broadcasted_iota(jnp.int32, (C, C), 0)
    s_io = lax.broadcasted_iota(jnp.int32, (C, C), 1)
    n_levels = int(math.log2(C))
    level_masks = [(t_io > s_io) & (lax.shift_right_logical(t_io ^ s_io, sh) == 1) for sh in range(n_levels)]
    sub_io = lax.broadcasted_iota(jnp.int32, (C // 8, 8, 1), 1)
    gn = gn_ref[...]

    heads = [(b, h) for b in range(n_batch) for h in range(GLA_HEADS)]

    def ksl(h):
        return slice(h * GLA_DK, (h + 1) * GLA_DK)

    def vsl(h):
        return slice(h * GLA_DV, (h + 1) * GLA_DV)

    qs, kks, bcs = [], [], []
    for b in range(n_batch):
        qs.append(qg_refs[b][...].astype(F32) * (GLA_DK ** -0.5))
        kks.append(jnp.where(valid, kg_refs[b][...].astype(F32), 0.0))
        g = jnp.where(valid, lf_refs[b][...], 0.0)
        g1 = g.astype(BF16)
        r1 = g - g1.astype(F32)
        g2 = r1.astype(BF16)
        g3 = (r1 - g2.astype(F32)).astype(BF16)
        bcs.append(jnp.dot(tril, g1, preferred_element_type=F32)
                   + jnp.dot(tril, g2, preferred_element_type=F32)
                   + jnp.dot(tril, g3, preferred_element_type=F32))

    qts, kts, yss, qks, b_lasts = [], [], [], [], []
    for b in range(n_batch):
        q, k, bc = qs[b], kks[b], bcs[b]
        b_last = bc[C - 1:C, :]
        b_lasts.append(b_last)
        qts.append((q * jnp.exp(bc)).astype(BF16))
        kts.append((k * jnp.exp(b_last - bc)).astype(BF16))
        b3 = bc.reshape(C // 8, 8, -1)
        ys = []
        for sh in range(n_levels):
            half = 1 << sh
            if half >= 8:
                ref = jnp.concatenate(
                    [jnp.broadcast_to(bc[i + half - 1:i + half, :], (2 * half, bc.shape[1]))
                     for i in range(0, C, 2 * half)], axis=0)
            elif half == 4:
                ref = jnp.broadcast_to(b3[:, 3:4, :], b3.shape).reshape(bc.shape)
            elif half == 2:
                ref = jnp.where(sub_io >= 4, jnp.broadcast_to(b3[:, 5:6, :], b3.shape),
                                jnp.broadcast_to(b3[:, 1:2, :], b3.shape)).reshape(bc.shape)
            else:
                ref = jnp.where((row & 1) == 1, pltpu.roll(bc, 1, axis=0), bc)
            upper = (lax.shift_right_logical(row, sh) & 1) == 1
            ys.append((jnp.where(upper, q, k) * jnp.exp(-jnp.abs(bc - ref))).astype(BF16))
        yss.append(ys)
        qks.append(q * k)

    level_prods, inters = {}, {}
    for b, h in heads:
        level_prods[b, h] = [_nt_dot(yss[b][sh][:, ksl(h)], yss[b][sh][:, ksl(h)]) for sh in range(n_levels)]
    for b, h in heads:
        st = s_ref[b, h]
        inters[b, h] = jnp.dot(qts[b][:, ksl(h)], st.astype(BF16), preferred_element_type=F32)
        s_ref[b, h] = (_decay_column(jnp.exp(b_lasts[b][:, ksl(h)])) * st
                       + _tn_dot(kts[b][:, ksl(h)], vg_refs[b][:, vsl(h)]))

    for b, h in heads:
        a = jnp.where(t_io == s_io, jnp.sum(qks[b][:, ksl(h)], axis=-1, keepdims=True), 0.0)
        for sh in range(n_levels):
            a = jnp.where(level_masks[sh], level_prods[b, h][sh], a)
        o = jnp.dot(a.astype(BF16), vg_refs[b][:, vsl(h)], preferred_element_type=F32) + inters[b, h]
        o_ref[b, :, vsl(h)] = _gla_finish(o, gn, gate_refs[b][:, vsl(h)].astype(F32))


def _gla_prompt(p_all, lf_all, gla_norm_row, n_batch, n_chunks, meta_blk):
    skip = (FRONT_PAD + N_META) // GLA_CHUNK
    out_len = (n_chunks - skip) * GLA_CHUNK
    C = GLA_CHUNK
    per_blk = WINDOW // C

    def chunk(b):
        def f(c):
            return jnp.where(c < skip, meta_blk * per_blk + c, (meta_blk + 1) * per_blk + b * (n_chunks - skip) + c - skip)
        return f

    in_specs = []
    args = []
    for b in range(n_batch):
        f = chunk(b)
        in_specs += [
            pl.BlockSpec((C, GLA_KEY_WIDTH), lambda c, f=f: (f(c), COL_QG // GLA_KEY_WIDTH)),
            pl.BlockSpec((C, GLA_KEY_WIDTH), lambda c, f=f: (f(c), COL_KG // GLA_KEY_WIDTH)),
            pl.BlockSpec((C, GLA_WIDTH), lambda c, f=f: (f(c), COL_VG // GLA_WIDTH)),
            pl.BlockSpec((C, GLA_WIDTH), lambda c, f=f: (f(c), COL_GATE_G // GLA_WIDTH)),
            pl.BlockSpec((C, GLA_KEY_WIDTH), lambda c, f=f: (f(c), 0)),
        ]
        args += [p_all, p_all, p_all, p_all, lf_all]
    return pl.pallas_call(
        _gla_prompt_kernel,
        grid=(n_chunks,),
        in_specs=in_specs + [pl.BlockSpec((1, GLA_DV), lambda c: (0, 0))],
        out_specs=[
            pl.BlockSpec((n_batch, C, GLA_WIDTH), lambda c: (0, jnp.maximum(c - skip, 0), 0)),
            pl.BlockSpec((n_batch, GLA_HEADS, GLA_DK, GLA_DV), lambda c: (0, 0, 0, 0)),
        ],
        out_shape=[
            jax.ShapeDtypeStruct((n_batch, out_len, GLA_WIDTH), BF16),
            jax.ShapeDtypeStruct((n_batch, GLA_HEADS, GLA_DK, GLA_DV), F32),
        ],
        compiler_params=pltpu.CompilerParams(
            dimension_semantics=("arbitrary",), vmem_limit_bytes=VMEM_LIMIT),
        name="gla_prompt",
    )(*args, gla_norm_row)


OUT_PROJ_TM = 256
WO_CHUNK = 512


def _out_proj_kernel(map_ref, mgp_ref, xp_ref, mas_ref, mgs_ref, xs_ref, nw_ref, w_hbm,
                     yp_ref, ys_ref, wbf_ref, stage_ref, wsem, *, n_p):
    i = pl.program_id(0)

    @pl.when(i == 0)
    def _load_weight():
        chunks = [_group_major_pieces(0, d, WO_CHUNK) for d in range(0, ATT_WIDTH, WO_CHUNK)]
        chunks += [[(d, 0, WO_CHUNK)] for d in range(ATT_WIDTH, D_MODEL, WO_CHUNK)]
        _load_rows_as_bf16(w_hbm, chunks, stage_ref, wsem, wbf_ref, WO_CHUNK)

    def finish(ma_ref, mg_ref, x_ref, y_ref):
        o = (jnp.dot(ma_ref[...], wbf_ref[:ATT_WIDTH, :], preferred_element_type=F32)
             + jnp.dot(mg_ref[...], wbf_ref[ATT_WIDTH:, :], preferred_element_type=F32))
        var = jnp.mean(o * o, axis=-1, keepdims=True)
        y_ref[...] = x_ref[...] + o * lax.rsqrt(var + EPS) * nw_ref[...]

    @pl.when(i < n_p)
    def _():
        finish(map_ref, mgp_ref, xp_ref, yp_ref)

    @pl.when(i >= n_p)
    def _():
        finish(mas_ref, mgs_ref, xs_ref, ys_ref)


def _out_proj(ma_p, mg_p, x_p, ma_s, mg_s, x_s, w_out_l, norm_w):
    tm = OUT_PROJ_TM
    n_p = x_p.shape[0] // tm
    n_s = x_s.shape[0] // tm
    assert x_p.shape[0] % tm == 0 and x_s.shape[0] % tm == 0 and ATT_WIDTH + GLA_WIDTH == D_MODEL

    def p_blk(i):
        return (jnp.minimum(i, n_p - 1), 0)

    def s_blk(i):
        return (jnp.maximum(i - n_p, 0), 0)

    row_bytes = ATT_WIDTH * 2 + GLA_WIDTH * 2 + 2 * D_MODEL * 4
    vmem_bytes = (D_MODEL * D_MODEL * 2 + 2 * WO_CHUNK * D_MODEL * 4 + 2 * 2 * tm * row_bytes
                  + 3 * tm * D_MODEL * 4)
    return pl.pallas_call(
        functools.partial(_out_proj_kernel, n_p=n_p),
        grid=(n_p + n_s,),
        in_specs=[
            pl.BlockSpec((tm, ATT_WIDTH), p_blk),
            pl.BlockSpec((tm, GLA_WIDTH), p_blk),
            pl.BlockSpec((tm, D_MODEL), p_blk),
            pl.BlockSpec((tm, ATT_WIDTH), s_blk),
            pl.BlockSpec((tm, GLA_WIDTH), s_blk),
            pl.BlockSpec((tm, D_MODEL), s_blk),
            pl.BlockSpec((1, D_MODEL), lambda i: (0, 0)),
            pl.BlockSpec(memory_space=pl.ANY),
        ],
        out_specs=[
            pl.BlockSpec((tm, D_MODEL), p_blk),
            pl.BlockSpec((tm, D_MODEL), s_blk),
        ],
        out_shape=[
            jax.ShapeDtypeStruct(x_p.shape, F32),
            jax.ShapeDtypeStruct(x_s.shape, F32),
        ],
        scratch_shapes=[
            pltpu.VMEM((D_MODEL, D_MODEL), BF16),
            pltpu.VMEM((2, WO_CHUNK, D_MODEL), F32),
            pltpu.SemaphoreType.DMA((2,)),
        ],
        compiler_params=pltpu.CompilerParams(
            dimension_semantics=("arbitrary",), vmem_limit_bytes=vmem_bytes),
        name="out_proj",
    )(ma_p, mg_p, x_p, ma_s, mg_s, x_s, norm_w, w_out_l)


SWA_SAMPLE_SEQS = 8


def _swa_sample_kernel(rb_ref, sink_ref, q_ref, gate_ref, kvn_ref, ck_ref, cv_ref,
                       o_ref, ko_ref, vo_ref, tabc_ref, tabn_ref, sinkcol_ref, *, t_new):
    step = pl.program_id(0)
    W = ck_ref.shape[1]
    rows = N_HEADS * t_new

    @pl.when(step == 0)
    def _build_tables():
        r = lax.broadcasted_iota(jnp.int32, (rows, W), 0)
        qi = r & (t_new - 1)
        kj = lax.broadcasted_iota(jnp.int32, (rows, W), 1)
        dist_c = qi + W - kj
        valid_c = (dist_c >= 0) & (dist_c < WINDOW)
        bucket_c = _t5_bucket(dist_c)
        rn = lax.broadcasted_iota(jnp.int32, (rows, t_new), 0)
        qn = rn & (t_new - 1)
        jn = lax.broadcasted_iota(jnp.int32, (rows, t_new), 1)
        dist_n = qn - jn
        valid_n = (dist_n >= 0) & (dist_n < WINDOW)
        bucket_n = _t5_bucket(dist_n)
        head_c = lax.shift_right_logical(r, int(math.log2(t_new)))
        head_n = lax.shift_right_logical(rn, int(math.log2(t_new)))
        head_s = lax.shift_right_logical(lax.broadcasted_iota(jnp.int32, (rows, 1), 0), int(math.log2(t_new)))
        tabc_ref[...] = jnp.full((rows, W), NEG, F32)
        tabn_ref[...] = jnp.full((rows, t_new), NEG, F32)
        sinkcol_ref[...] = jnp.zeros((rows, 1), F32)
        for g in range(GQA_GROUP):
            for h in range(N_KV_HEADS):
                hd = h * GQA_GROUP + g
                blk = g * N_KV_HEADS + h
                sinkcol_ref[...] = jnp.where(head_s == blk, sink_ref[hd], sinkcol_ref[...])

        def body(bk, carry):
            for g in range(GQA_GROUP):
                for h in range(N_KV_HEADS):
                    hd = h * GQA_GROUP + g
                    blk = g * N_KV_HEADS + h
                    val = rb_ref[bk, hd]
                    tabc_ref[...] = jnp.where((bucket_c == bk) & valid_c & (head_c == blk), val, tabc_ref[...])
                    tabn_ref[...] = jnp.where((bucket_n == bk) & valid_n & (head_n == blk), val, tabn_ref[...])
            return carry

        lax.fori_loop(0, N_BUCKETS, body, 0)

    lane_head = lax.shift_right_logical(lax.broadcasted_iota(jnp.int32, (1, KV_WIDTH), 1), 6)
    scale = HEAD_DIM ** -0.5
    sink = sinkcol_ref[...]
    seqs = range(q_ref.shape[0])
    scores = []
    for s in seqs:
        q = q_ref[s].astype(F32) * scale
        pieces = []
        for g in range(GQA_GROUP):
            qg = q[:, g * KV_WIDTH:(g + 1) * KV_WIDTH]
            for h in range(N_KV_HEADS):
                pieces.append(jnp.where(lane_head == h, qg, 0.0))
        lhs = jnp.concatenate(pieces, axis=0).astype(BF16)
        sc = _nt_dot(lhs, ck_ref[s].astype(BF16)) + tabc_ref[...]
        sn = _nt_dot(lhs, kvn_ref[s, :, :KV_WIDTH].astype(BF16)) + tabn_ref[...]
        scores.append((sc, sn))
    probs = []
    for s in seqs:
        sc, sn = scores[s]
        m = jnp.maximum(jnp.maximum(jnp.max(sc, axis=-1, keepdims=True),
                                    jnp.max(sn, axis=-1, keepdims=True)), sink)
        ec = jnp.exp(sc - m)
        en = jnp.exp(sn - m)
        den = (jnp.sum(ec, axis=-1, keepdims=True) + jnp.sum(en, axis=-1, keepdims=True)
               + jnp.exp(sink - m))
        inv = 1.0 / den
        probs.append(((ec * inv).astype(BF16), (en * inv).astype(BF16)))
    for s in seqs:
        pc, pn = probs[s]
        o = (jnp.dot(pc, cv_ref[s].astype(BF16), preferred_element_type=F32)
             + jnp.dot(pn, kvn_ref[s, :, KV_WIDTH:].astype(BF16), preferred_element_type=F32))
        outs = []
        for g in range(GQA_GROUP):
            acc = jnp.zeros((t_new, KV_WIDTH), F32)
            for h in range(N_KV_HEADS):
                blk = g * N_KV_HEADS + h
                acc = jnp.where(lane_head == h, o[blk * t_new:(blk + 1) * t_new, :], acc)
            outs.append(acc)
        att = jnp.concatenate(outs, axis=1)
        gate = gate_ref[s].astype(F32)
        o_ref[s] = (att * (gate * _sigmoid(gate))).astype(BF16)
        ko_ref[s, :W - t_new, :] = ck_ref[s, t_new:, :]
        ko_ref[s, W - t_new:, :] = kvn_ref[s, :, :KV_WIDTH]
        vo_ref[s, :W - t_new, :] = cv_ref[s, t_new:, :]
        vo_ref[s, W - t_new:, :] = kvn_ref[s, :, KV_WIDTH:]


def _swa_sample(p3, kv3, cache_k, cache_v, rel_bias, sinks, seq0):
    t_new = p3.shape[1]
    db, w, _ = cache_k.shape
    sb = SWA_SAMPLE_SEQS
    assert seq0 % sb == 0
    blk0 = seq0 // sb
    assert t_new & (t_new - 1) == 0 and w >= t_new and w <= WINDOW
    sb = SWA_SAMPLE_SEQS
    rows = N_HEADS * t_new
    return pl.pallas_call(
        functools.partial(_swa_sample_kernel, t_new=t_new),
        grid=(db // sb,),
        in_specs=[
            pl.BlockSpec(memory_space=pltpu.SMEM),
            pl.BlockSpec(memory_space=pltpu.SMEM),
            pl.BlockSpec((sb, t_new, ATT_WIDTH), lambda i: (blk0 + i, 0, COL_QA // ATT_WIDTH)),
            pl.BlockSpec((sb, t_new, ATT_WIDTH), lambda i: (blk0 + i, 0, COL_GATE_A // ATT_WIDTH)),
            pl.BlockSpec((sb, t_new, 2 * KV_WIDTH), lambda i: (blk0 + i, 0, 0)),
            pl.BlockSpec((sb, w, KV_WIDTH), lambda i: (i, 0, 0)),
            pl.BlockSpec((sb, w, KV_WIDTH), lambda i: (i, 0, 0)),
        ],
        out_specs=[
            pl.BlockSpec((sb, t_new, ATT_WIDTH), lambda i: (i, 0, 0)),
            pl.BlockSpec((sb, w, KV_WIDTH), lambda i: (i, 0, 0)),
            pl.BlockSpec((sb, w, KV_WIDTH), lambda i: (i, 0, 0)),
        ],
        out_shape=[
            jax.ShapeDtypeStruct((db, t_new, ATT_WIDTH), BF16),
            jax.ShapeDtypeStruct((db, w, KV_WIDTH), F32),
            jax.ShapeDtypeStruct((db, w, KV_WIDTH), F32),
        ],
        scratch_shapes=[
            pltpu.VMEM((rows, w), F32),
            pltpu.VMEM((rows, t_new), F32),
            pltpu.VMEM((rows, 1), F32),
        ],
        compiler_params=pltpu.CompilerParams(
            dimension_semantics=("arbitrary",), vmem_limit_bytes=VMEM_LIMIT),
        name="swa_sample",
    )(rel_bias, sinks, p3, p3, kv3, cache_k, cache_v)


GLA_SAMPLE_SEQS = 4


def _cumsum_rows8(g):
    row = lax.broadcasted_iota(jnp.int32, g.shape, 0)
    out = g
    for sh in (1, 2, 4):
        out = out + jnp.where(row >= sh, pltpu.roll(out, sh, axis=0), 0.0)
    return out


def _gla_sample_kernel(qg_ref, kg_ref, vg_ref, gate_ref, lf_ref, gn_ref, st_ref, o_ref, so_ref):
    T = qg_ref.shape[1]
    gn = gn_ref[...]
    t_col = lax.broadcasted_iota(jnp.int32, (T, 1), 0)
    zpad = jnp.zeros((8, GLA_DK), F32)
    for s in range(qg_ref.shape[0]):
        q = qg_ref[s].astype(F32) * (GLA_DK ** -0.5)
        k = kg_ref[s].astype(F32)
        bc = _cumsum_rows8(lf_ref[s])
        b_last = bc[T - 1:T, :]
        qt = q * jnp.exp(bc)
        kt = k * jnp.exp(b_last - bc)
        v_all = vg_ref[s].astype(F32)
        cols = [[None] * T for _ in range(GLA_HEADS)]
        for j in range(T):
            bs = jnp.broadcast_to(bc[j:j + 1, :], bc.shape)
            ks_ = jnp.broadcast_to(k[j:j + 1, :], k.shape)
            p = q * jnp.exp(jnp.minimum(bc - bs, 0.0)) * ks_
            for h in range(GLA_HEADS):
                col = jnp.sum(p[:, h * GLA_DK:(h + 1) * GLA_DK], axis=-1, keepdims=True)
                cols[h][j] = jnp.where(t_col >= j, col, 0.0)
        for h in range(GLA_HEADS):
            ks = slice(h * GLA_DK, (h + 1) * GLA_DK)
            vs = slice(h * GLA_DV, (h + 1) * GLA_DV)
            v = v_all[:, vs]
            st = st_ref[s, h]
            qt16 = jnp.concatenate([qt[:, ks], zpad], axis=0).astype(BF16)
            o = jnp.dot(qt16, st.astype(BF16), preferred_element_type=F32)[:T, :]
            for j in range(T):
                o = o + cols[h][j] * v[j:j + 1, :]
            kt16 = jnp.concatenate([kt[:, ks], zpad], axis=0).astype(BF16)
            v16 = jnp.concatenate([v, jnp.zeros((8, GLA_DV), F32)], axis=0).astype(BF16)
            so_ref[s, h] = _decay_column(jnp.exp(b_last[:, ks])) * st + _tn_dot(kt16, v16)
            o_ref[s, :, vs] = _gla_finish(o, gn, gate_ref[s, :, vs].astype(F32))


def _gla_sample(p3, lf3, state, gla_norm_row, seq0):
    t_new = p3.shape[1]
    db = state.shape[0]
    sb = GLA_SAMPLE_SEQS
    assert t_new == 8 and seq0 % sb == 0
    blk0 = seq0 // sb
    return pl.pallas_call(
        _gla_sample_kernel,
        grid=(db // sb,),
        in_specs=[
            pl.BlockSpec((sb, t_new, GLA_KEY_WIDTH), lambda i: (blk0 + i, 0, COL_QG // GLA_KEY_WIDTH)),
            pl.BlockSpec((sb, t_new, GLA_KEY_WIDTH), lambda i: (blk0 + i, 0, COL_KG // GLA_KEY_WIDTH)),
            pl.BlockSpec((sb, t_new, GLA_WIDTH), lambda i: (blk0 + i, 0, COL_VG // GLA_WIDTH)),
            pl.BlockSpec((sb, t_new, GLA_WIDTH), lambda i: (blk0 + i, 0, COL_GATE_G // GLA_WIDTH)),
            pl.BlockSpec((sb, t_new, GLA_KEY_WIDTH), lambda i: (blk0 + i, 0, 0)),
            pl.BlockSpec((1, GLA_DV), lambda i: (0, 0)),
            pl.BlockSpec((sb, GLA_HEADS, GLA_DK, GLA_DV), lambda i: (i, 0, 0, 0)),
        ],
        out_specs=[
            pl.BlockSpec((sb, t_new, GLA_WIDTH), lambda i: (i, 0, 0)),
            pl.BlockSpec((sb, GLA_HEADS, GLA_DK, GLA_DV), lambda i: (i, 0, 0, 0)),
        ],
        out_shape=[
            jax.ShapeDtypeStruct((db, t_new, GLA_WIDTH), BF16),
            jax.ShapeDtypeStruct((db, GLA_HEADS, GLA_DK, GLA_DV), F32),
        ],
        compiler_params=pltpu.CompilerParams(
            dimension_semantics=("arbitrary",), vmem_limit_bytes=VMEM_LIMIT),
        name="gla_sample",
    )(p3, p3, p3, p3, lf3, gla_norm_row, state)


def kernel(x_prompt, x_sample, cache_k_win, cache_v_win, state_gla, meta_tokens, rel_bias,
           norm_pre, norm_post, w_in, w_a2, b_a, attn_sinks, gla_norm, w_out):
    n_batch, seq, _ = x_prompt.shape
    db, t_new, _ = x_sample.shape
    depth = w_in.shape[0]
    assert depth == 1
    l = 0
    lp = seq + WINDOW
    n_blocks = lp // WINDOW
    n_chunks = lp // GLA_CHUNK
    w_cache = cache_k_win.shape[2]

    w_in_t = w_in[l].T
    assert sum(IN_SIZES[:-1]) == N_MAIN
    w_a2p = jnp.pad(w_a2[l], ((0, LANES - GLA_RANK), (0, 0))).astype(BF16)
    npre = norm_pre[l][None, :]
    npost = norm_post[l][None, :]
    ba = b_a[l][None, :]
    gn = gla_norm[l][None, :]
    sinks = attn_sinks[l]

    tm = IN_PROJ_TM
    x_meta = jnp.concatenate([jnp.zeros((tm - N_META, D_MODEL), x_prompt.dtype),
                              meta_tokens.astype(x_prompt.dtype)], axis=0)
    p_all, kv_all, lf_all = _in_proj(
        x_meta, x_prompt.reshape(n_batch * seq, D_MODEL), x_sample.reshape(db * t_new, D_MODEL),
        w_in_t, npre, w_a2p, ba)
    meta_blk = tm // WINDOW - 1
    sample_seq0 = (tm + n_batch * seq) // t_new

    mixed_a = _swa_prompt(p_all, rel_bias, sinks, n_batch, n_blocks, meta_blk)
    mixed_g, s_fin = _gla_prompt(p_all, lf_all, gn, n_batch, n_chunks, meta_blk)
    kv_p = jnp.stack([kv_all[tm + (b + 1) * seq - WINDOW:tm + (b + 1) * seq] for b in range(n_batch)])
    k_win_p = kv_p[..., :KV_WIDTH].reshape(1, n_batch, WINDOW, N_KV_HEADS, HEAD_DIM)
    v_win_p = kv_p[..., KV_WIDTH:].reshape(1, n_batch, WINDOW, N_KV_HEADS, HEAD_DIM)
    gla_p = s_fin[None]

    ps3 = p_all.reshape(-1, t_new, N_MAIN)
    mixed_as, k_win_s, v_win_s = _swa_sample(
        ps3, kv_all.reshape(-1, t_new, 2 * KV_WIDTH),
        cache_k_win[l].reshape(db, w_cache, KV_WIDTH), cache_v_win[l].reshape(db, w_cache, KV_WIDTH),
        rel_bias, sinks, sample_seq0)
    mixed_gs, s_new = _gla_sample(ps3, lf_all.reshape(-1, t_new, GLA_KEY_WIDTH), state_gla[l], gn, sample_seq0)

    y_prompt, y_sample = _out_proj(
        mixed_a, mixed_g.reshape(n_batch * seq, GLA_WIDTH), x_prompt.reshape(n_batch * seq, D_MODEL),
        mixed_as.reshape(db * t_new, ATT_WIDTH), mixed_gs.reshape(db * t_new, GLA_WIDTH),
        x_sample.reshape(db * t_new, D_MODEL), w_out[l], npost)
    y_prompt = y_prompt.reshape(n_batch, seq, D_MODEL)
    y_sample = y_sample.reshape(db, t_new, D_MODEL)
    k_win_s = k_win_s.reshape(1, db, w_cache, N_KV_HEADS, HEAD_DIM)
    v_win_s = v_win_s.reshape(1, db, w_cache, N_KV_HEADS, HEAD_DIM)

    return (y_prompt, y_sample, k_win_p, v_win_p, gla_p, k_win_s, v_win_s, s_new[None])
```

```python
import functools
import math

import jax
import jax.numpy as jnp
from jax import lax
from jax.experimental import pallas as pl
from jax.experimental.pallas import tpu as pltpu

F32 = jnp.float32
BF16 = jnp.bfloat16

D_MODEL = 2048
N_META = 16
ATT_WIDTH = 1024
HEAD_DIM = 64
N_HEADS = 16
N_KV_HEADS = 4
GQA_GROUP = 4
KV_WIDTH = 256
WINDOW = 128
N_BUCKETS = 32
MAX_DISTANCE = 128
GLA_WIDTH = 1024
GLA_HEADS = 4
GLA_DV = 256
GLA_DK = 128
GLA_KEY_WIDTH = 512
GLA_RANK = 16
GLA_NORMALIZER = 16.0
GLA_CHUNK = 64
FRONT_PAD = WINDOW - N_META
EPS = 1e-6
PAST_LEN = 8192

IN_SIZES = (ATT_WIDTH, KV_WIDTH, KV_WIDTH, ATT_WIDTH,
            GLA_KEY_WIDTH, GLA_KEY_WIDTH, GLA_WIDTH, GLA_WIDTH, GLA_RANK)

N_MAIN = 5632
TN = 512
COL_QA = 0
COL_GATE_A = 1024
COL_QG = 2048
COL_KG = 2560
COL_VG = 3072
COL_GATE_G = 4096
COL_KA = 5120
COL_VA = 5376
LANES = 128
NEG = -1e30

VMEM_LIMIT = 48 * 1024 * 1024


def _sigmoid(x):
    return 1.0 / (1.0 + jnp.exp(-x))


def _log_sigmoid(x):
    return jnp.minimum(x, 0.0) - jnp.log1p(jnp.exp(-jnp.abs(x)))


def _nt_dot(a, b):
    return lax.dot_general(a, b, (((1,), (1,)), ((), ())), preferred_element_type=F32)


def _tn_dot(a, b):
    return lax.dot_general(a, b, (((0,), (0,)), ((), ())), preferred_element_type=F32)


def _group_major_pieces(src0, dst0, n_rows):
    pieces = []
    for r in range(dst0, dst0 + n_rows, HEAD_DIM):
        g, h = divmod(r // HEAD_DIM, N_KV_HEADS)
        pieces.append((src0 + (h * GQA_GROUP + g) * HEAD_DIM, r - dst0, HEAD_DIM))
    return pieces


def _load_rows_as_bf16(src_hbm, chunk_pieces, stage_ref, sem, dst_ref, chunk_rows):
    def copies(c):
        return [pltpu.make_async_copy(src_hbm.at[pl.ds(src, n), :], stage_ref.at[c % 2, pl.ds(dst, n), :],
                                      sem.at[c % 2]) for src, dst, n in chunk_pieces[c]]

    for cp in copies(0):
        cp.start()
    for c in range(len(chunk_pieces)):
        if c + 1 < len(chunk_pieces):
            for cp in copies(c + 1):
                cp.start()
        for cp in copies(c):
            cp.wait()
        dst_ref[c * chunk_rows:(c + 1) * chunk_rows, :] = stage_ref[c % 2].astype(BF16)


W_CHUNK = 256
IN_PROJ_TM = 256
GATE_HEAD_AFTER_TILES = 3


def _in_proj_weight_pieces():
    src_off = dict(zip(("q_a", "k_a", "v_a", "gate_a", "q_g", "k_g", "v_g", "gate_g"),
                       (sum(IN_SIZES[:i]) for i in range(8))))
    chunks = []
    for name in ("q_a", "gate_a"):
        for d in range(0, ATT_WIDTH, W_CHUNK):
            chunks.append(_group_major_pieces(src_off[name], d, W_CHUNK))
    for name, width in (("q_g", GLA_KEY_WIDTH), ("k_g", GLA_KEY_WIDTH), ("v_g", GLA_WIDTH),
                        ("gate_g", GLA_WIDTH), ("k_a", KV_WIDTH), ("v_a", KV_WIDTH)):
        for d in range(0, width, W_CHUNK):
            chunks.append([(src_off[name] + d, 0, W_CHUNK)])
    return chunks


def _in_proj_kernel(xm_hbm, xp_hbm, xs_hbm, wt_hbm, nw_ref, wa2_ref, ba_ref,
                    p_ref, kv_ref, lf_ref,
                    wbf_ref, wz_ref, stage_ref, zstage_ref, xbuf_ref, hn0_ref, hn1_ref, lf0_ref, lf1_ref,
                    wsem, zsem, xsem, *, n_p, n_s):
    i = pl.program_id(0)
    tm = xbuf_ref.shape[1]
    n_steps = 1 + n_p + n_s

    def x_copy(step, slot, fn):
        @pl.when(step == 0)
        def _():
            fn(pltpu.make_async_copy(xm_hbm, xbuf_ref.at[slot], xsem.at[slot]))

        @pl.when((step >= 1) & (step <= n_p))
        def _():
            fn(pltpu.make_async_copy(xp_hbm.at[pl.ds((step - 1) * tm, tm), :], xbuf_ref.at[slot], xsem.at[slot]))

        @pl.when(step > n_p)
        def _():
            fn(pltpu.make_async_copy(xs_hbm.at[pl.ds((step - 1 - n_p) * tm, tm), :], xbuf_ref.at[slot],
                                     xsem.at[slot]))

    slot = i % 2
    nxt = 1 - slot

    def rmsnorm_to(x_slot, hn_out):
        x = xbuf_ref[x_slot]
        var = jnp.mean(x * x, axis=-1, keepdims=True)
        hn_out[...] = (x * lax.rsqrt(var + EPS) * nw_ref[...]).astype(BF16)

    def gate_head_to(hn_in, lf_out):
        z = _nt_dot(hn_in[...], wz_ref[...])
        logits = jnp.dot(z.astype(BF16), wa2_ref[...], preferred_element_type=F32) + ba_ref[...]
        lf_out[...] = _log_sigmoid(logits) * (1.0 / GLA_NORMALIZER)

    @pl.when(i == 0)
    def _first_step():
        x_copy(0, 0, lambda cp: cp.start())
        x_copy(1, 1, lambda cp: cp.start())

        z_copy = pltpu.make_async_copy(wt_hbm.at[pl.ds(N_MAIN, GLA_RANK), :], zstage_ref, zsem.at[0])
        z_copy.start()
        wz_ref[...] = jnp.zeros(wz_ref.shape, BF16)
        z_copy.wait()
        wz_ref[:GLA_RANK, :] = zstage_ref[...].astype(BF16)
        _load_rows_as_bf16(wt_hbm, _in_proj_weight_pieces(), stage_ref, wsem, wbf_ref, W_CHUNK)
        x_copy(0, 0, lambda cp: cp.wait())
        rmsnorm_to(0, hn0_ref)
        gate_head_to(hn0_ref, lf0_ref)

    @pl.when(i + 2 < n_steps)
    def _prefetch():
        x_copy(i + 2, slot, lambda cp: cp.start())

    @pl.when(i + 1 < n_steps)
    def _():
        x_copy(i + 1, nxt, lambda cp: cp.wait())

    def project(hn_cur, lf_cur, hn_nxt, lf_nxt):
        lf_ref[...] = lf_cur[...]
        hn = hn_cur[...]
        rmsnorm_to(nxt, hn_nxt)
        for j in range(N_MAIN // TN):
            if j == GATE_HEAD_AFTER_TILES:
                gate_head_to(hn_nxt, lf_nxt)
            acc = _nt_dot(hn, wbf_ref[j * TN:(j + 1) * TN, :])
            p_ref[:, j * TN:(j + 1) * TN] = acc.astype(BF16)
            if j * TN == COL_KA:
                kv_ref[...] = acc

    @pl.when(slot == 0)
    def _():
        project(hn0_ref, lf0_ref, hn1_ref, lf1_ref)

    @pl.when(slot == 1)
    def _():
        project(hn1_ref, lf1_ref, hn0_ref, lf0_ref)


def _in_proj(x_meta, x_prompt, x_sample, w_in_t, norm_w, w_a2p, b_a):
    tm = IN_PROJ_TM
    n_p = x_prompt.shape[0] // tm
    n_s = x_sample.shape[0] // tm
    assert x_meta.shape[0] == tm and x_prompt.shape[0] % tm == 0 and x_sample.shape[0] % tm == 0
    assert COL_KA % TN == 0 and len(_in_proj_weight_pieces()) * W_CHUNK == N_MAIN
    m_all = (1 + n_p + n_s) * tm

    def const(i):
        return (0, 0)

    any_spec = pl.BlockSpec(memory_space=pl.ANY)
    buffers = (D_MODEL * N_MAIN * 2 + 2 * D_MODEL * W_CHUNK * 4 + 2 * tm * D_MODEL * 4
               + 2 * tm * N_MAIN * 2 + 2 * 2 * tm * GLA_KEY_WIDTH * 4
               + D_MODEL * LANES * 2 + D_MODEL * GLA_RANK * 4 + 2 * LANES * GLA_KEY_WIDTH * 2)
    buffers += 2 * tm * D_MODEL * 2 + 2 * tm * GLA_KEY_WIDTH * 4
    temps = tm * D_MODEL * (4 + 4) + 2 * tm * TN * 4 + 2 * tm * GLA_KEY_WIDTH * 4
    vmem_bytes = buffers + temps
    return pl.pallas_call(
        functools.partial(_in_proj_kernel, n_p=n_p, n_s=n_s),
        grid=(1 + n_p + n_s,),
        in_specs=[
            any_spec, any_spec, any_spec, any_spec,
            pl.BlockSpec((1, D_MODEL), const),
            pl.BlockSpec((LANES, GLA_KEY_WIDTH), const),
            pl.BlockSpec((1, GLA_KEY_WIDTH), const),
        ],
        out_specs=[
            pl.BlockSpec((tm, N_MAIN), lambda i: (i, 0)),
            pl.BlockSpec((tm, 2 * KV_WIDTH), lambda i: (i, 0)),
            pl.BlockSpec((tm, GLA_KEY_WIDTH), lambda i: (i, 0)),
        ],
        out_shape=[
            jax.ShapeDtypeStruct((m_all, N_MAIN), BF16),
            jax.ShapeDtypeStruct((m_all, 2 * KV_WIDTH), F32),
            jax.ShapeDtypeStruct((m_all, GLA_KEY_WIDTH), F32),
        ],
        scratch_shapes=[
            pltpu.VMEM((N_MAIN, D_MODEL), BF16),
            pltpu.VMEM((LANES, D_MODEL), BF16),
            pltpu.VMEM((2, W_CHUNK, D_MODEL), F32),
            pltpu.VMEM((GLA_RANK, D_MODEL), F32),
            pltpu.VMEM((2, tm, D_MODEL), F32),
            pltpu.VMEM((tm, D_MODEL), BF16),
            pltpu.VMEM((tm, D_MODEL), BF16),
            pltpu.VMEM((tm, GLA_KEY_WIDTH), F32),
            pltpu.VMEM((tm, GLA_KEY_WIDTH), F32),
            pltpu.SemaphoreType.DMA((2,)),
            pltpu.SemaphoreType.DMA((1,)),
            pltpu.SemaphoreType.DMA((2,)),
        ],
        compiler_params=pltpu.CompilerParams(
            dimension_semantics=("arbitrary",), vmem_limit_bytes=vmem_bytes),
        name="in_proj",
    )(x_meta, x_prompt, x_sample, w_in_t, norm_w, w_a2p, b_a)


def _t5_bucket(dist):
    max_exact = N_BUCKETS // 2
    d = jnp.maximum(dist, 0)
    ratio = jnp.maximum(d, max_exact).astype(F32) / max_exact
    val = jnp.log(ratio) / math.log(MAX_DISTANCE / max_exact) * (N_BUCKETS - max_exact)
    large = jnp.full(dist.shape, max_exact, jnp.int32)
    for step in range(1, N_BUCKETS - max_exact):
        large = large + (val >= step).astype(jnp.int32)
    return jnp.where(d < max_exact, d, large)


SWA_BLOCKS_PER_STEP = 2
HEAD_SHIFT = HEAD_DIM.bit_length() - 1


def _swa_prompt_kernel(rb_ref, sink_ref, qg_ref, kvp_ref, kvc_ref, o_ref, tab_ref):
    b = pl.program_id(0)
    m = pl.program_id(1)
    W2 = 2 * WINDOW
    NL = N_KV_HEADS * W2
    n_sub = qg_ref.shape[0] // WINDOW

    @pl.when((b == 0) & (m == 0))
    def _build_bias_tables():
        qi = lax.broadcasted_iota(jnp.int32, (WINDOW, W2), 0)
        kj = lax.broadcasted_iota(jnp.int32, (WINDOW, W2), 1)
        dist = qi - kj + WINDOW
        valid = (dist >= 0) & (dist < WINDOW)
        bucket = _t5_bucket(dist)
        for g in range(GQA_GROUP):
            tab_ref[1, g] = jnp.full((WINDOW, NL), NEG, F32)

        def body(bk, carry):
            hit = (bucket == bk) & valid
            for g in range(GQA_GROUP):
                for h in range(N_KV_HEADS):
                    cur = tab_ref[1, g, :, h * W2:(h + 1) * W2]
                    tab_ref[1, g, :, h * W2:(h + 1) * W2] = jnp.where(hit, rb_ref[bk, h * GQA_GROUP + g], cur)
            return carry

        lax.fori_loop(0, N_BUCKETS, body, 0)
        kvalid = kj - FRONT_PAD >= 0
        for g in range(GQA_GROUP):
            for h in range(N_KV_HEADS):
                tab_ref[0, g, :, h * W2:(h + 1) * W2] = jnp.where(
                    kvalid, tab_ref[1, g, :, h * W2:(h + 1) * W2], NEG)

    lane_head = lax.shift_right_logical(lax.broadcasted_iota(jnp.int32, (1, KV_WIDTH), 1), HEAD_SHIFT)
    zero = jnp.zeros((), BF16)

    def per_head(x):
        return [jnp.where(lane_head == h, x, zero) for h in range(N_KV_HEADS)]

    def rows(r):
        return slice(r * WINDOW, (r + 1) * WINDOW)

    km = [per_head(kvp_ref[:, :KV_WIDTH])] + [per_head(kvc_ref[rows(r), :KV_WIDTH]) for r in range(n_sub)]
    vm = [per_head(kvp_ref[:, KV_WIDTH:])] + [per_head(kvc_ref[rows(r), KV_WIDTH:]) for r in range(n_sub)]

    def stacked(parts, r):
        return jnp.concatenate([blk for h in range(N_KV_HEADS) for blk in (parts[r][h], parts[r + 1][h])], axis=0)

    scale = HEAD_DIM ** -0.5
    scores = {}
    for r in range(n_sub):
        kkm = stacked(km, r)
        tv = jnp.where(m == 0, 0, 1) if r == 0 else 1
        for g in range(GQA_GROUP):
            qg = (qg_ref[rows(r), g * KV_WIDTH:(g + 1) * KV_WIDTH].astype(F32) * scale).astype(BF16)
            scores[r, g] = _nt_dot(qg, kkm) + tab_ref[tv, g]
    for r in range(n_sub):
        vvm = stacked(vm, r)
        for g in range(GQA_GROUP):
            probs = []
            for h in range(N_KV_HEADS):
                s = scores[r, g][:, h * W2:(h + 1) * W2]
                sink = sink_ref[h * GQA_GROUP + g]
                mx = jnp.maximum(jnp.max(s, axis=-1, keepdims=True), sink)
                e = jnp.exp(s - mx)
                den = jnp.sum(e, axis=-1, keepdims=True) + jnp.exp(sink - mx)
                probs.append((e * (1.0 / den)).astype(BF16))
            p = jnp.concatenate(probs, axis=1)
            o = jnp.dot(p, vvm, preferred_element_type=F32)
            gate = qg_ref[rows(r), ATT_WIDTH + g * KV_WIDTH:ATT_WIDTH + (g + 1) * KV_WIDTH].astype(F32)
            o_ref[rows(r), g * KV_WIDTH:(g + 1) * KV_WIDTH] = (o * (gate * _sigmoid(gate))).astype(BF16)


def _swa_prompt(p_all, rel_bias, sinks, n_batch, n_blocks, meta_blk):
    nb = SWA_BLOCKS_PER_STEP
    n_real = n_blocks - 1
    rows = nb * WINDOW
    assert n_real % nb == 0 and (meta_blk + 1) % nb == 0
    assert COL_GATE_A == COL_QA + ATT_WIDTH and COL_VA == COL_KA + KV_WIDTH and COL_QA == 0
    steps = n_real // nb
    first = (meta_blk + 1) // nb

    def cur(b, m):
        return first + b * steps + m

    def prev(b, m):
        return jnp.where(m == 0, meta_blk, meta_blk + b * n_real + nb * m)

    return pl.pallas_call(
        _swa_prompt_kernel,
        grid=(n_batch, steps),
        in_specs=[
            pl.BlockSpec(memory_space=pltpu.SMEM),
            pl.BlockSpec(memory_space=pltpu.SMEM),
            pl.BlockSpec((rows, 2 * ATT_WIDTH), lambda b, m: (cur(b, m), 0)),
            pl.BlockSpec((WINDOW, 2 * KV_WIDTH), lambda b, m: (prev(b, m), COL_KA // (2 * KV_WIDTH))),
            pl.BlockSpec((rows, 2 * KV_WIDTH), lambda b, m: (cur(b, m), COL_KA // (2 * KV_WIDTH))),
        ],
        out_specs=pl.BlockSpec((rows, ATT_WIDTH), lambda b, m: (b * steps + m, 0)),
        out_shape=jax.ShapeDtypeStruct((n_batch * n_real * WINDOW, ATT_WIDTH), BF16),
        scratch_shapes=[pltpu.VMEM((2, GQA_GROUP, WINDOW, N_KV_HEADS * 2 * WINDOW), F32)],
        compiler_params=pltpu.CompilerParams(
            dimension_semantics=("arbitrary", "arbitrary"), vmem_limit_bytes=VMEM_LIMIT),
        name="swa_prompt",
    )(rel_bias, sinks, p_all, p_all, p_all)


def _diag_scores(q3, k3, b3, width):
    nb = q3.shape[0]
    n_heads = q3.shape[2] // GLA_DK
    r_io = lax.broadcasted_iota(jnp.int32, (nb, 8, width), 0)
    t_io = lax.broadcasted_iota(jnp.int32, (nb, 8, width), 1)
    l_io = lax.broadcasted_iota(jnp.int32, (nb, 8, width), 2)
    s_of_lane = l_io - 8 * r_io
    outs = [jnp.zeros((nb, 8, width), F32) for _ in range(n_heads)]
    for s in range(8):
        bs = jnp.broadcast_to(b3[:, s:s + 1, :], b3.shape)
        ks = jnp.broadcast_to(k3[:, s:s + 1, :], k3.shape)
        e = jnp.exp(jnp.minimum(b3 - bs, 0.0))
        p = q3 * e * ks
        for h in range(n_heads):
            col = jnp.sum(p[:, :, h * GLA_DK:(h + 1) * GLA_DK], axis=-1, keepdims=True)
            outs[h] = jnp.where(s_of_lane == s, col, outs[h])
    causal = (s_of_lane >= 0) & (s_of_lane <= t_io) & (s_of_lane < 8)
    return [jnp.where(causal, o, 0.0) for o in outs]


def _gla_finish(o, gn, gate):
    var = jnp.mean(o * o, axis=-1, keepdims=True)
    on = o * lax.rsqrt(var + EPS) * gn
    return (on * (gate * _sigmoid(gate))).astype(BF16)


def _decay_column(row):
    eye = (lax.broadcasted_iota(jnp.int32, (GLA_DK, GLA_DK), 0)
           == lax.broadcasted_iota(jnp.int32, (GLA_DK, GLA_DK), 1))
    return jnp.sum(jnp.where(eye, jnp.broadcast_to(row, (GLA_DK, GLA_DK)), 0.0), axis=1, keepdims=True)


def _gla_prompt_kernel(*refs):
    o_ref, s_ref = refs[-2:]
    gn_ref = refs[-3]
    n_batch = (len(refs) - 3) // 5
    qg_refs, kg_refs, vg_refs, gate_refs, lf_refs = (refs[j:5 * n_batch:5] for j in range(5))
    c = pl.program_id(0)
    C = GLA_CHUNK

    @pl.when(c == 0)
    def _():
        s_ref[...] = jnp.zeros(s_ref.shape, F32)

    row = lax.broadcasted_iota(jnp.int32, (C, 1), 0)
    valid = (c * C + row) >= FRONT_PAD
    tril = (lax.broadcasted_iota(jnp.int32, (C, C), 0)
            >= lax.broadcasted_iota(jnp.int32, (C, C), 1)).astype(F32).astype(BF16)
    t_io = lax.broadcasted_iota(jnp.int32, (C, C), 0)
    s_io = lax.broadcasted_iota(jnp.int32, (C, C), 1)
    n_levels = int(math.log2(C))
    level_masks = [(t_io > s_io) & (lax.shift_right_logical(t_io ^ s_io, sh) == 1) for sh in range(n_levels)]
    sub_io = lax.broadcasted_iota(jnp.int32, (C // 8, 8, 1), 1)
    gn = gn_ref[...]

    heads = [(b, h) for b in range(n_batch) for h in range(GLA_HEADS)]

    def ksl(h):
        return slice(h * GLA_DK, (h + 1) * GLA_DK)

    def vsl(h):
        return slice(h * GLA_DV, (h + 1) * GLA_DV)

    qs, kks, bcs = [], [], []
    for b in range(n_batch):
        qs.append(qg_refs[b][...].astype(F32) * (GLA_DK ** -0.5))
        kks.append(jnp.where(valid, kg_refs[b][...].astype(F32), 0.0))
        g = jnp.where(valid, lf_refs[b][...], 0.0)
        g1 = g.astype(BF16)
        r1 = g - g1.astype(F32)
        g2 = r1.astype(BF16)
        g3 = (r1 - g2.astype(F32)).astype(BF16)
        bcs.append(jnp.dot(tril, g1, preferred_element_type=F32)
                   + jnp.dot(tril, g2, preferred_element_type=F32)
                   + jnp.dot(tril, g3, preferred_element_type=F32))

    qts, kts, yss, qks, b_lasts = [], [], [], [], []
    for b in range(n_batch):
        q, k, bc = qs[b], kks[b], bcs[b]
        b_last = bc[C - 1:C, :]
        b_lasts.append(b_last)
        qts.append((q * jnp.exp(bc)).astype(BF16))
        kts.append((k * jnp.exp(b_last - bc)).astype(BF16))
        b3 = bc.reshape(C // 8, 8, -1)
        ys = []
        for sh in range(n_levels):
            half = 1 << sh
            if half >= 8:
                ref = jnp.concatenate(
                    [jnp.broadcast_to(bc[i + half - 1:i + half, :], (2 * half, bc.shape[1]))
                     for i in range(0, C, 2 * half)], axis=0)
            elif half == 4:
                ref = jnp.broadcast_to(b3[:, 3:4, :], b3.shape).reshape(bc.shape)
            elif half == 2:
                ref = jnp.where(sub_io >= 4, jnp.broadcast_to(b3[:, 5:6, :], b3.shape),
                                jnp.broadcast_to(b3[:, 1:2, :], b3.shape)).reshape(bc.shape)
            else:
                ref = jnp.where((row & 1) == 1, pltpu.roll(bc, 1, axis=0), bc)
            upper = (lax.shift_right_logical(row, sh) & 1) == 1
            ys.append((jnp.where(upper, q, k) * jnp.exp(-jnp.abs(bc - ref))).astype(BF16))
        yss.append(ys)
        qks.append(q * k)

    level_prods, inters = {}, {}
    for b, h in heads:
        level_prods[b, h] = [_nt_dot(yss[b][sh][:, ksl(h)], yss[b][sh][:, ksl(h)]) for sh in range(n_levels)]
    for b, h in heads:
        st = s_ref[b, h]
        inters[b, h] = jnp.dot(qts[b][:, ksl(h)], st.astype(BF16), preferred_element_type=F32)
        s_ref[b, h] = (_decay_column(jnp.exp(b_lasts[b][:, ksl(h)])) * st
                       + _tn_dot(kts[b][:, ksl(h)], vg_refs[b][:, vsl(h)]))

    for b, h in heads:
        a = jnp.where(t_io == s_io, jnp.sum(qks[b][:, ksl(h)], axis=-1, keepdims=True), 0.0)
        for sh in range(n_levels):
            a = jnp.where(level_masks[sh], level_prods[b, h][sh], a)
        o = jnp.dot(a.astype(BF16), vg_refs[b][:, vsl(h)], preferred_element_type=F32) + inters[b, h]
        o_ref[b, :, vsl(h)] = _gla_finish(o, gn, gate_refs[b][:, vsl(h)].astype(F32))


def _gla_prompt(p_all, lf_all, gla_norm_row, n_batch, n_chunks, meta_blk):
    skip = (FRONT_PAD + N_META) // GLA_CHUNK
    out_len = (n_chunks - skip) * GLA_CHUNK
    C = GLA_CHUNK
    per_blk = WINDOW // C

    def chunk(b):
        def f(c):
            return jnp.where(c < skip, meta_blk * per_blk + c, (meta_blk + 1) * per_blk + b * (n_chunks - skip) + c - skip)
        return f

    in_specs = []
    args = []
    for b in range(n_batch):
        f = chunk(b)
        in_specs += [
            pl.BlockSpec((C, GLA_KEY_WIDTH), lambda c, f=f: (f(c), COL_QG // GLA_KEY_WIDTH)),
            pl.BlockSpec((C, GLA_KEY_WIDTH), lambda c, f=f: (f(c), COL_KG // GLA_KEY_WIDTH)),
            pl.BlockSpec((C, GLA_WIDTH), lambda c, f=f: (f(c), COL_VG // GLA_WIDTH)),
            pl.BlockSpec((C, GLA_WIDTH), lambda c, f=f: (f(c), COL_GATE_G // GLA_WIDTH)),
            pl.BlockSpec((C, GLA_KEY_WIDTH), lambda c, f=f: (f(c), 0)),
        ]
        args += [p_all, p_all, p_all, p_all, lf_all]
    return pl.pallas_call(
        _gla_prompt_kernel,
        grid=(n_chunks,),
        in_specs=in_specs + [pl.BlockSpec((1, GLA_DV), lambda c: (0, 0))],
        out_specs=[
            pl.BlockSpec((n_batch, C, GLA_WIDTH), lambda c: (0, jnp.maximum(c - skip, 0), 0)),
            pl.BlockSpec((n_batch, GLA_HEADS, GLA_DK, GLA_DV), lambda c: (0, 0, 0, 0)),
        ],
        out_shape=[
            jax.ShapeDtypeStruct((n_batch, out_len, GLA_WIDTH), BF16),
            jax.ShapeDtypeStruct((n_batch, GLA_HEADS, GLA_DK, GLA_DV), F32),
        ],
        compiler_params=pltpu.CompilerParams(
            dimension_semantics=("arbitrary",), vmem_limit_bytes=VMEM_LIMIT),
        name="gla_prompt",
    )(*args, gla_norm_row)


OUT_PROJ_TM = 256
WO_CHUNK = 512


def _out_proj_kernel(map_ref, mgp_ref, xp_ref, mas_ref, mgs_ref, xs_ref, nw_ref, w_hbm,
                     yp_ref, ys_ref, wbf_ref, stage_ref, wsem, *, n_p):
    i = pl.program_id(0)

    @pl.when(i == 0)
    def _load_weight():
        chunks = [_group_major_pieces(0, d, WO_CHUNK) for d in range(0, ATT_WIDTH, WO_CHUNK)]
        chunks += [[(d, 0, WO_CHUNK)] for d in range(ATT_WIDTH, D_MODEL, WO_CHUNK)]
        _load_rows_as_bf16(w_hbm, chunks, stage_ref, wsem, wbf_ref, WO_CHUNK)

    def finish(ma_ref, mg_ref, x_ref, y_ref):
        o = (jnp.dot(ma_ref[...], wbf_ref[:ATT_WIDTH, :], preferred_element_type=F32)
             + jnp.dot(mg_ref[...], wbf_ref[ATT_WIDTH:, :], preferred_element_type=F32))
        var = jnp.mean(o * o, axis=-1, keepdims=True)
        y_ref[...] = x_ref[...] + o * lax.rsqrt(var + EPS) * nw_ref[...]

    @pl.when(i < n_p)
    def _():
        finish(map_ref, mgp_ref, xp_ref, yp_ref)

    @pl.when(i >= n_p)
    def _():
        finish(mas_ref, mgs_ref, xs_ref, ys_ref)


def _out_proj(ma_p, mg_p, x_p, ma_s, mg_s, x_s, w_out_l, norm_w):
    tm = OUT_PROJ_TM
    n_p = x_p.shape[0] // tm
    n_s = x_s.shape[0] // tm
    assert x_p.shape[0] % tm == 0 and x_s.shape[0] % tm == 0 and ATT_WIDTH + GLA_WIDTH == D_MODEL

    def p_blk(i):
        return (jnp.minimum(i, n_p - 1), 0)

    def s_blk(i):
        return (jnp.maximum(i - n_p, 0), 0)

    row_bytes = ATT_WIDTH * 2 + GLA_WIDTH * 2 + 2 * D_MODEL * 4
    vmem_bytes = (D_MODEL * D_MODEL * 2 + 2 * WO_CHUNK * D_MODEL * 4 + 2 * 2 * tm * row_bytes
                  + 3 * tm * D_MODEL * 4)
    return pl.pallas_call(
        functools.partial(_out_proj_kernel, n_p=n_p),
        grid=(n_p + n_s,),
        in_specs=[
            pl.BlockSpec((tm, ATT_WIDTH), p_blk),
            pl.BlockSpec((tm, GLA_WIDTH), p_blk),
            pl.BlockSpec((tm, D_MODEL), p_blk),
            pl.BlockSpec((tm, ATT_WIDTH), s_blk),
            pl.BlockSpec((tm, GLA_WIDTH), s_blk),
            pl.BlockSpec((tm, D_MODEL), s_blk),
            pl.BlockSpec((1, D_MODEL), lambda i: (0, 0)),
            pl.BlockSpec(memory_space=pl.ANY),
        ],
        out_specs=[
            pl.BlockSpec((tm, D_MODEL), p_blk),
            pl.BlockSpec((tm, D_MODEL), s_blk),
        ],
        out_shape=[
            jax.ShapeDtypeStruct(x_p.shape, F32),
            jax.ShapeDtypeStruct(x_s.shape, F32),
        ],
        scratch_shapes=[
            pltpu.VMEM((D_MODEL, D_MODEL), BF16),
            pltpu.VMEM((2, WO_CHUNK, D_MODEL), F32),
            pltpu.SemaphoreType.DMA((2,)),
        ],
        compiler_params=pltpu.CompilerParams(
            dimension_semantics=("arbitrary",), vmem_limit_bytes=vmem_bytes),
        name="out_proj",
    )(ma_p, mg_p, x_p, ma_s, mg_s, x_s, norm_w, w_out_l)


SWA_SAMPLE_SEQS = 8


def _swa_sample_kernel(rb_ref, sink_ref, q_ref, gate_ref, kvn_ref, ck_ref, cv_ref,
                       o_ref, ko_ref, vo_ref, tabc_ref, tabn_ref, sinkcol_ref, *, t_new):
    step = pl.program_id(0)
    W = ck_ref.shape[1]
    rows = N_HEADS * t_new

    @pl.when(step == 0)
    def _build_tables():
        r = lax.broadcasted_iota(jnp.int32, (rows, W), 0)
        qi = r & (t_new - 1)
        kj = lax.broadcasted_iota(jnp.int32, (rows, W), 1)
        dist_c = qi + W - kj
        valid_c = (dist_c >= 0) & (dist_c < WINDOW)
        bucket_c = _t5_bucket(dist_c)
        rn = lax.broadcasted_iota(jnp.int32, (rows, t_new), 0)
        qn = rn & (t_new - 1)
        jn = lax.broadcasted_iota(jnp.int32, (rows, t_new), 1)
        dist_n = qn - jn
        valid_n = (dist_n >= 0) & (dist_n < WINDOW)
        bucket_n = _t5_bucket(dist_n)
        head_c = lax.shift_right_logical(r, int(math.log2(t_new)))
        head_n = lax.shift_right_logical(rn, int(math.log2(t_new)))
        head_s = lax.shift_right_logical(lax.broadcasted_iota(jnp.int32, (rows, 1), 0), int(math.log2(t_new)))
        tabc_ref[...] = jnp.full((rows, W), NEG, F32)
        tabn_ref[...] = jnp.full((rows, t_new), NEG, F32)
        sinkcol_ref[...] = jnp.zeros((rows, 1), F32)
        for g in range(GQA_GROUP):
            for h in range(N_KV_HEADS):
                hd = h * GQA_GROUP + g
                blk = g * N_KV_HEADS + h
                sinkcol_ref[...] = jnp.where(head_s == blk, sink_ref[hd], sinkcol_ref[...])

        def body(bk, carry):
            for g in range(GQA_GROUP):
                for h in range(N_KV_HEADS):
                    hd = h * GQA_GROUP + g
                    blk = g * N_KV_HEADS + h
                    val = rb_ref[bk, hd]
                    tabc_ref[...] = jnp.where((bucket_c == bk) & valid_c & (head_c == blk), val, tabc_ref[...])
                    tabn_ref[...] = jnp.where((bucket_n == bk) & valid_n & (head_n == blk), val, tabn_ref[...])
            return carry

        lax.fori_loop(0, N_BUCKETS, body, 0)

    lane_head = lax.shift_right_logical(lax.broadcasted_iota(jnp.int32, (1, KV_WIDTH), 1), HEAD_SHIFT)
    scale = HEAD_DIM ** -0.5
    sink = sinkcol_ref[...]
    seqs = range(q_ref.shape[0])
    scores = []
    for s in seqs:
        q = q_ref[s].astype(F32) * scale
        pieces = []
        for g in range(GQA_GROUP):
            qg = q[:, g * KV_WIDTH:(g + 1) * KV_WIDTH]
            for h in range(N_KV_HEADS):
                pieces.append(jnp.where(lane_head == h, qg, 0.0))
        lhs = jnp.concatenate(pieces, axis=0).astype(BF16)
        sc = _nt_dot(lhs, ck_ref[s].astype(BF16)) + tabc_ref[...]
        sn = _nt_dot(lhs, kvn_ref[s, :, :KV_WIDTH].astype(BF16)) + tabn_ref[...]
        scores.append((sc, sn))
    probs = []
    for s in seqs:
        sc, sn = scores[s]
        m = jnp.maximum(jnp.maximum(jnp.max(sc, axis=-1, keepdims=True),
                                    jnp.max(sn, axis=-1, keepdims=True)), sink)
        ec = jnp.exp(sc - m)
        en = jnp.exp(sn - m)
        den = (jnp.sum(ec, axis=-1, keepdims=True) + jnp.sum(en, axis=-1, keepdims=True)
               + jnp.exp(sink - m))
        inv = 1.0 / den
        probs.append(((ec * inv).astype(BF16), (en * inv).astype(BF16)))
    for s in seqs:
        pc, pn = probs[s]
        o = (jnp.dot(pc, cv_ref[s].astype(BF16), preferred_element_type=F32)
             + jnp.dot(pn, kvn_ref[s, :, KV_WIDTH:].astype(BF16), preferred_element_type=F32))
        outs = []
        for g in range(GQA_GROUP):
            acc = jnp.zeros((t_new, KV_WIDTH), F32)
            for h in range(N_KV_HEADS):
                blk = g * N_KV_HEADS + h
                acc = jnp.where(lane_head == h, o[blk * t_new:(blk + 1) * t_new, :], acc)
            outs.append(acc)
        att = jnp.concatenate(outs, axis=1)
        gate = gate_ref[s].astype(F32)
        o_ref[s] = (att * (gate * _sigmoid(gate))).astype(BF16)
        ko_ref[s, :W - t_new, :] = ck_ref[s, t_new:, :]
        ko_ref[s, W - t_new:, :] = kvn_ref[s, :, :KV_WIDTH]
        vo_ref[s, :W - t_new, :] = cv_ref[s, t_new:, :]
        vo_ref[s, W - t_new:, :] = kvn_ref[s, :, KV_WIDTH:]


def _swa_sample(p3, kv3, cache_k, cache_v, rel_bias, sinks, seq0):
    t_new = p3.shape[1]
    db, w, _ = cache_k.shape
    sb = SWA_SAMPLE_SEQS
    assert seq0 % sb == 0
    blk0 = seq0 // sb
    assert t_new & (t_new - 1) == 0 and w >= t_new and w <= WINDOW
    sb = SWA_SAMPLE_SEQS
    rows = N_HEADS * t_new
    return pl.pallas_call(
        functools.partial(_swa_sample_kernel, t_new=t_new),
        grid=(db // sb,),
        in_specs=[
            pl.BlockSpec(memory_space=pltpu.SMEM),
            pl.BlockSpec(memory_space=pltpu.SMEM),
            pl.BlockSpec((sb, t_new, ATT_WIDTH), lambda i: (blk0 + i, 0, COL_QA // ATT_WIDTH)),
            pl.BlockSpec((sb, t_new, ATT_WIDTH), lambda i: (blk0 + i, 0, COL_GATE_A // ATT_WIDTH)),
            pl.BlockSpec((sb, t_new, 2 * KV_WIDTH), lambda i: (blk0 + i, 0, 0)),
            pl.BlockSpec((sb, w, KV_WIDTH), lambda i: (i, 0, 0)),
            pl.BlockSpec((sb, w, KV_WIDTH), lambda i: (i, 0, 0)),
        ],
        out_specs=[
            pl.BlockSpec((sb, t_new, ATT_WIDTH), lambda i: (i, 0, 0)),
            pl.BlockSpec((sb, w, KV_WIDTH), lambda i: (i, 0, 0)),
            pl.BlockSpec((sb, w, KV_WIDTH), lambda i: (i, 0, 0)),
        ],
        out_shape=[
            jax.ShapeDtypeStruct((db, t_new, ATT_WIDTH), BF16),
            jax.ShapeDtypeStruct((db, w, KV_WIDTH), F32),
            jax.ShapeDtypeStruct((db, w, KV_WIDTH), F32),
        ],
        scratch_shapes=[
            pltpu.VMEM((rows, w), F32),
            pltpu.VMEM((rows, t_new), F32),
            pltpu.VMEM((rows, 1), F32),
        ],
        compiler_params=pltpu.CompilerParams(
            dimension_semantics=("arbitrary",), vmem_limit_bytes=VMEM_LIMIT),
        name="swa_sample",
    )(rel_bias, sinks, p3, p3, kv3, cache_k, cache_v)


GLA_SAMPLE_SEQS = 8


def _cumsum_rows8(g):
    row = lax.broadcasted_iota(jnp.int32, g.shape, 0)
    out = g
    for sh in (1, 2, 4):
        out = out + jnp.where(row >= sh, pltpu.roll(out, sh, axis=0), 0.0)
    return out


def _gla_sample_kernel(qg_ref, kg_ref, vg_ref, gate_ref, lf_ref, gn_ref, st_ref, o_ref, so_ref):
    T = qg_ref.shape[1]
    gn = gn_ref[...]
    t_col = lax.broadcasted_iota(jnp.int32, (T, 1), 0)
    zpad = jnp.zeros((8, GLA_DK), F32)
    seqs = range(qg_ref.shape[0])
    pairs = [(s, h) for s in seqs for h in range(GLA_HEADS)]
    zpad_v = jnp.zeros((8, GLA_DV), F32)

    def ksl(h):
        return slice(h * GLA_DK, (h + 1) * GLA_DK)

    def vsl(h):
        return slice(h * GLA_DV, (h + 1) * GLA_DV)

    qts, kts, b_lasts, vs_, cols = [], [], [], [], {}
    for s in seqs:
        q = qg_ref[s].astype(F32) * (GLA_DK ** -0.5)
        k = kg_ref[s].astype(F32)
        bc = _cumsum_rows8(lf_ref[s])
        b_last = bc[T - 1:T, :]
        b_lasts.append(b_last)
        qts.append(q * jnp.exp(bc))
        kts.append(k * jnp.exp(b_last - bc))
        vs_.append(vg_ref[s].astype(F32))
        for j in range(T):
            bs = jnp.broadcast_to(bc[j:j + 1, :], bc.shape)
            ks_ = jnp.broadcast_to(k[j:j + 1, :], k.shape)
            p = q * jnp.exp(jnp.minimum(bc - bs, 0.0)) * ks_
            for h in range(GLA_HEADS):
                col = jnp.sum(p[:, ksl(h)], axis=-1, keepdims=True)
                cols[s, h, j] = jnp.where(t_col >= j, col, 0.0)

    inters = {}
    for s, h in pairs:
        qt16 = jnp.concatenate([qts[s][:, ksl(h)], zpad], axis=0).astype(BF16)
        inters[s, h] = jnp.dot(qt16, st_ref[s, h].astype(BF16), preferred_element_type=F32)[:T, :]
    for s, h in pairs:
        kt16 = jnp.concatenate([kts[s][:, ksl(h)], zpad], axis=0).astype(BF16)
        v16 = jnp.concatenate([vs_[s][:, vsl(h)], zpad_v], axis=0).astype(BF16)
        so_ref[s, h] = (_decay_column(jnp.exp(b_lasts[s][:, ksl(h)])) * st_ref[s, h]
                        + _tn_dot(kt16, v16))

    for s, h in pairs:
        o = inters[s, h]
        v = vs_[s][:, vsl(h)]
        for j in range(T):
            o = o + cols[s, h, j] * v[j:j + 1, :]
        o_ref[s, :, vsl(h)] = _gla_finish(o, gn, gate_ref[s, :, vsl(h)].astype(F32))


def _gla_sample(p3, lf3, state, gla_norm_row, seq0):
    t_new = p3.shape[1]
    db = state.shape[0]
    sb = GLA_SAMPLE_SEQS
    assert t_new == 8 and seq0 % sb == 0
    blk0 = seq0 // sb
    return pl.pallas_call(
        _gla_sample_kernel,
        grid=(db // sb,),
        in_specs=[
            pl.BlockSpec((sb, t_new, GLA_KEY_WIDTH), lambda i: (blk0 + i, 0, COL_QG // GLA_KEY_WIDTH)),
            pl.BlockSpec((sb, t_new, GLA_KEY_WIDTH), lambda i: (blk0 + i, 0, COL_KG // GLA_KEY_WIDTH)),
            pl.BlockSpec((sb, t_new, GLA_WIDTH), lambda i: (blk0 + i, 0, COL_VG // GLA_WIDTH)),
            pl.BlockSpec((sb, t_new, GLA_WIDTH), lambda i: (blk0 + i, 0, COL_GATE_G // GLA_WIDTH)),
            pl.BlockSpec((sb, t_new, GLA_KEY_WIDTH), lambda i: (blk0 + i, 0, 0)),
            pl.BlockSpec((1, GLA_DV), lambda i: (0, 0)),
            pl.BlockSpec((sb, GLA_HEADS, GLA_DK, GLA_DV), lambda i: (i, 0, 0, 0)),
        ],
        out_specs=[
            pl.BlockSpec((sb, t_new, GLA_WIDTH), lambda i: (i, 0, 0)),
            pl.BlockSpec((sb, GLA_HEADS, GLA_DK, GLA_DV), lambda i: (i, 0, 0, 0)),
        ],
        out_shape=[
            jax.ShapeDtypeStruct((db, t_new, GLA_WIDTH), BF16),
            jax.ShapeDtypeStruct((db, GLA_HEADS, GLA_DK, GLA_DV), F32),
        ],
        compiler_params=pltpu.CompilerParams(
            dimension_semantics=("arbitrary",), vmem_limit_bytes=VMEM_LIMIT),
        name="gla_sample",
    )(p3, p3, p3, p3, lf3, gla_norm_row, state)


def kernel(x_prompt, x_sample, cache_k_win, cache_v_win, state_gla, meta_tokens, rel_bias,
           norm_pre, norm_post, w_in, w_a2, b_a, attn_sinks, gla_norm, w_out):
    n_batch, seq, _ = x_prompt.shape
    db, t_new, _ = x_sample.shape
    depth = w_in.shape[0]
    assert depth == 1
    l = 0
    lp = seq + WINDOW
    n_blocks = lp // WINDOW
    n_chunks = lp // GLA_CHUNK
    w_cache = cache_k_win.shape[2]

    w_in_t = w_in[l].T
    assert sum(IN_SIZES[:-1]) == N_MAIN
    w_a2p = jnp.pad(w_a2[l], ((0, LANES - GLA_RANK), (0, 0))).astype(BF16)
    npre = norm_pre[l][None, :]
    npost = norm_post[l][None, :]
    ba = b_a[l][None, :]
    gn = gla_norm[l][None, :]
    sinks = attn_sinks[l]

    tm = IN_PROJ_TM
    x_meta = jnp.concatenate([jnp.zeros((tm - N_META, D_MODEL), x_prompt.dtype),
                              meta_tokens.astype(x_prompt.dtype)], axis=0)
    p_all, kv_all, lf_all = _in_proj(
        x_meta, x_prompt.reshape(n_batch * seq, D_MODEL), x_sample.reshape(db * t_new, D_MODEL),
        w_in_t, npre, w_a2p, ba)
    meta_blk = tm // WINDOW - 1
    sample_seq0 = (tm + n_batch * seq) // t_new

    mixed_a = _swa_prompt(p_all, rel_bias, sinks, n_batch, n_blocks, meta_blk)
    mixed_g, s_fin = _gla_prompt(p_all, lf_all, gn, n_batch, n_chunks, meta_blk)
    kv_p = jnp.stack([kv_all[tm + (b + 1) * seq - WINDOW:tm + (b + 1) * seq] for b in range(n_batch)])
    k_win_p = kv_p[..., :KV_WIDTH].reshape(1, n_batch, WINDOW, N_KV_HEADS, HEAD_DIM)
    v_win_p = kv_p[..., KV_WIDTH:].reshape(1, n_batch, WINDOW, N_KV_HEADS, HEAD_DIM)
    gla_p = s_fin[None]

    ps3 = p_all.reshape(-1, t_new, N_MAIN)
    mixed_as, k_win_s, v_win_s = _swa_sample(
        ps3, kv_all.reshape(-1, t_new, 2 * KV_WIDTH),
        cache_k_win[l].reshape(db, w_cache, KV_WIDTH), cache_v_win[l].reshape(db, w_cache, KV_WIDTH),
        rel_bias, sinks, sample_seq0)
    mixed_gs, s_new = _gla_sample(ps3, lf_all.reshape(-1, t_new, GLA_KEY_WIDTH), state_gla[l], gn, sample_seq0)

    y_prompt, y_sample = _out_proj(
        mixed_a, mixed_g.reshape(n_batch * seq, GLA_WIDTH), x_prompt.reshape(n_batch * seq, D_MODEL),
        mixed_as.reshape(db * t_new, ATT_WIDTH), mixed_gs.reshape(db * t_new, GLA_WIDTH),
        x_sample.reshape(db * t_new, D_MODEL), w_out[l], npost)
    y_prompt = y_prompt.reshape(n_batch, seq, D_MODEL)
    y_sample = y_sample.reshape(db, t_new, D_MODEL)
    k_win_s = k_win_s.reshape(1, db, w_cache, N_KV_HEADS, HEAD_DIM)
    v_win_s = v_win_s.reshape(1, db, w_cache, N_KV_HEADS, HEAD_DIM)

    return (y_prompt, y_sample, k_win_p, v_win_p, gla_p, k_win_s, v_win_s, s_new[None])
```

```python
import functools
import math

import jax
import jax.numpy as jnp
from jax import lax
from jax.experimental import pallas as pl
from jax.experimental.pallas import tpu as pltpu

F32 = jnp.float32
BF16 = jnp.bfloat16

D_MODEL = 2048
N_META = 16
ATT_WIDTH = 1024
HEAD_DIM = 64
N_HEADS = 16
N_KV_HEADS = 4
GQA_GROUP = 4
KV_WIDTH = 256
WINDOW = 128
N_BUCKETS = 32
MAX_DISTANCE = 128
GLA_WIDTH = 1024
GLA_HEADS = 4
GLA_DV = 256
GLA_DK = 128
GLA_KEY_WIDTH = 512
GLA_RANK = 16
GLA_NORMALIZER = 16.0
GLA_CHUNK = 64
FRONT_PAD = WINDOW - N_META
EPS = 1e-6
PAST_LEN = 8192

IN_SIZES = (ATT_WIDTH, KV_WIDTH, KV_WIDTH, ATT_WIDTH,
            GLA_KEY_WIDTH, GLA_KEY_WIDTH, GLA_WIDTH, GLA_WIDTH, GLA_RANK)

N_MAIN = 5632
TN = 512
COL_QA = 0
COL_GATE_A = 1024
COL_QG = 2048
COL_KG = 2560
COL_VG = 3072
COL_GATE_G = 4096
COL_KA = 5120
COL_VA = 5376
LANES = 128
NEG = -1e30

VMEM_LIMIT = 48 * 1024 * 1024


def _sigmoid(x):
    return 1.0 / (1.0 + jnp.exp(-x))


def _log_sigmoid(x):
    return jnp.minimum(x, 0.0) - jnp.log1p(jnp.exp(-jnp.abs(x)))


def _nt_dot(a, b):
    return lax.dot_general(a, b, (((1,), (1,)), ((), ())), preferred_element_type=F32)


def _tn_dot(a, b):
    return lax.dot_general(a, b, (((0,), (0,)), ((), ())), preferred_element_type=F32)


def _group_major_pieces(src0, dst0, n_rows):
    pieces = []
    for r in range(dst0, dst0 + n_rows, HEAD_DIM):
        g, h = divmod(r // HEAD_DIM, N_KV_HEADS)
        pieces.append((src0 + (h * GQA_GROUP + g) * HEAD_DIM, r - dst0, HEAD_DIM))
    return pieces


def _load_rows_as_bf16(src_hbm, chunk_pieces, stage_ref, sem, dst_ref, chunk_rows):
    def copies(c):
        return [pltpu.make_async_copy(src_hbm.at[pl.ds(src, n), :], stage_ref.at[c % 2, pl.ds(dst, n), :],
                                      sem.at[c % 2]) for src, dst, n in chunk_pieces[c]]

    for cp in copies(0):
        cp.start()
    for c in range(len(chunk_pieces)):
        if c + 1 < len(chunk_pieces):
            for cp in copies(c + 1):
                cp.start()
        for cp in copies(c):
            cp.wait()
        dst_ref[c * chunk_rows:(c + 1) * chunk_rows, :] = stage_ref[c % 2].astype(BF16)


W_CHUNK = 256
IN_PROJ_TM = 256
GATE_HEAD_AFTER_TILES = 3


def _in_proj_weight_pieces():
    src_off = dict(zip(("q_a", "k_a", "v_a", "gate_a", "q_g", "k_g", "v_g", "gate_g"),
                       (sum(IN_SIZES[:i]) for i in range(8))))
    chunks = []
    for name in ("q_a", "gate_a"):
        for d in range(0, ATT_WIDTH, W_CHUNK):
            chunks.append(_group_major_pieces(src_off[name], d, W_CHUNK))
    for name, width in (("q_g", GLA_KEY_WIDTH), ("k_g", GLA_KEY_WIDTH), ("v_g", GLA_WIDTH),
                        ("gate_g", GLA_WIDTH), ("k_a", KV_WIDTH), ("v_a", KV_WIDTH)):
        for d in range(0, width, W_CHUNK):
            chunks.append([(src_off[name] + d, 0, W_CHUNK)])
    return chunks


def _in_proj_kernel(xm_hbm, xp_hbm, xs_hbm, wt_hbm, nw_ref, wa2_ref, ba_ref,
                    p_ref, kv_ref, lf_ref,
                    wbf_ref, wz_ref, stage_ref, zstage_ref, xbuf_ref, hn0_ref, hn1_ref, lf0_ref, lf1_ref,
                    wsem, zsem, xsem, *, n_p, n_s):
    i = pl.program_id(0)
    tm = xbuf_ref.shape[1]
    n_steps = 1 + n_p + n_s

    def x_copy(step, slot, fn):
        @pl.when(step == 0)
        def _():
            fn(pltpu.make_async_copy(xm_hbm, xbuf_ref.at[slot], xsem.at[slot]))

        @pl.when((step >= 1) & (step <= n_p))
        def _():
            fn(pltpu.make_async_copy(xp_hbm.at[pl.ds((step - 1) * tm, tm), :], xbuf_ref.at[slot], xsem.at[slot]))

        @pl.when(step > n_p)
        def _():
            fn(pltpu.make_async_copy(xs_hbm.at[pl.ds((step - 1 - n_p) * tm, tm), :], xbuf_ref.at[slot],
                                     xsem.at[slot]))

    slot = i % 2
    nxt = 1 - slot

    def rmsnorm_to(x_slot, hn_out):
        x = xbuf_ref[x_slot]
        var = jnp.mean(x * x, axis=-1, keepdims=True)
        hn_out[...] = (x * lax.rsqrt(var + EPS) * nw_ref[...]).astype(BF16)

    def gate_head_to(hn_in, lf_out):
        z = _nt_dot(hn_in[...], wz_ref[...])
        logits = jnp.dot(z.astype(BF16), wa2_ref[...], preferred_element_type=F32) + ba_ref[...]
        lf_out[...] = _log_sigmoid(logits) * (1.0 / GLA_NORMALIZER)

    @pl.when(i == 0)
    def _first_step():
        x_copy(0, 0, lambda cp: cp.start())
        x_copy(1, 1, lambda cp: cp.start())

        z_copy = pltpu.make_async_copy(wt_hbm.at[pl.ds(N_MAIN, GLA_RANK), :], zstage_ref, zsem.at[0])
        z_copy.start()
        wz_ref[...] = jnp.zeros(wz_ref.shape, BF16)
        z_copy.wait()
        wz_ref[:GLA_RANK, :] = zstage_ref[...].astype(BF16)
        _load_rows_as_bf16(wt_hbm, _in_proj_weight_pieces(), stage_ref, wsem, wbf_ref, W_CHUNK)
        x_copy(0, 0, lambda cp: cp.wait())
        rmsnorm_to(0, hn0_ref)
        gate_head_to(hn0_ref, lf0_ref)

    @pl.when(i + 2 < n_steps)
    def _prefetch():
        x_copy(i + 2, slot, lambda cp: cp.start())

    @pl.when(i + 1 < n_steps)
    def _():
        x_copy(i + 1, nxt, lambda cp: cp.wait())

    def project(hn_cur, lf_cur, hn_nxt, lf_nxt):
        lf_ref[...] = lf_cur[...]
        hn = hn_cur[...]
        rmsnorm_to(nxt, hn_nxt)
        for j in range(N_MAIN // TN):
            if j == GATE_HEAD_AFTER_TILES:
                gate_head_to(hn_nxt, lf_nxt)
            acc = _nt_dot(hn, wbf_ref[j * TN:(j + 1) * TN, :])
            p_ref[:, j * TN:(j + 1) * TN] = acc.astype(BF16)
            if j * TN == COL_KA:
                kv_ref[...] = acc

    @pl.when(slot == 0)
    def _():
        project(hn0_ref, lf0_ref, hn1_ref, lf1_ref)

    @pl.when(slot == 1)
    def _():
        project(hn1_ref, lf1_ref, hn0_ref, lf0_ref)


def _in_proj(x_meta, x_prompt, x_sample, w_in_t, norm_w, w_a2p, b_a):
    tm = IN_PROJ_TM
    n_p = x_prompt.shape[0] // tm
    n_s = x_sample.shape[0] // tm
    assert x_meta.shape[0] == tm and x_prompt.shape[0] % tm == 0 and x_sample.shape[0] % tm == 0
    assert COL_KA % TN == 0 and len(_in_proj_weight_pieces()) * W_CHUNK == N_MAIN
    m_all = (1 + n_p + n_s) * tm

    def const(i):
        return (0, 0)

    any_spec = pl.BlockSpec(memory_space=pl.ANY)
    buffers = (D_MODEL * N_MAIN * 2 + 2 * D_MODEL * W_CHUNK * 4 + 2 * tm * D_MODEL * 4
               + 2 * tm * N_MAIN * 2 + 2 * 2 * tm * GLA_KEY_WIDTH * 4
               + D_MODEL * LANES * 2 + D_MODEL * GLA_RANK * 4 + 2 * LANES * GLA_KEY_WIDTH * 2)
    buffers += 2 * tm * D_MODEL * 2 + 2 * tm * GLA_KEY_WIDTH * 4
    temps = tm * D_MODEL * (4 + 4) + 2 * tm * TN * 4 + 2 * tm * GLA_KEY_WIDTH * 4
    vmem_bytes = buffers + temps
    return pl.pallas_call(
        functools.partial(_in_proj_kernel, n_p=n_p, n_s=n_s),
        grid=(1 + n_p + n_s,),
        in_specs=[
            any_spec, any_spec, any_spec, any_spec,
            pl.BlockSpec((1, D_MODEL), const),
            pl.BlockSpec((LANES, GLA_KEY_WIDTH), const),
            pl.BlockSpec((1, GLA_KEY_WIDTH), const),
        ],
        out_specs=[
            pl.BlockSpec((tm, N_MAIN), lambda i: (i, 0)),
            pl.BlockSpec((tm, 2 * KV_WIDTH), lambda i: (i, 0)),
            pl.BlockSpec((tm, GLA_KEY_WIDTH), lambda i: (i, 0)),
        ],
        out_shape=[
            jax.ShapeDtypeStruct((m_all, N_MAIN), BF16),
            jax.ShapeDtypeStruct((m_all, 2 * KV_WIDTH), F32),
            jax.ShapeDtypeStruct((m_all, GLA_KEY_WIDTH), F32),
        ],
        scratch_shapes=[
            pltpu.VMEM((N_MAIN, D_MODEL), BF16),
            pltpu.VMEM((LANES, D_MODEL), BF16),
            pltpu.VMEM((2, W_CHUNK, D_MODEL), F32),
            pltpu.VMEM((GLA_RANK, D_MODEL), F32),
            pltpu.VMEM((2, tm, D_MODEL), F32),
            pltpu.VMEM((tm, D_MODEL), BF16),
            pltpu.VMEM((tm, D_MODEL), BF16),
            pltpu.VMEM((tm, GLA_KEY_WIDTH), F32),
            pltpu.VMEM((tm, GLA_KEY_WIDTH), F32),
            pltpu.SemaphoreType.DMA((2,)),
            pltpu.SemaphoreType.DMA((1,)),
            pltpu.SemaphoreType.DMA((2,)),
        ],
        compiler_params=pltpu.CompilerParams(
            dimension_semantics=("arbitrary",), vmem_limit_bytes=vmem_bytes),
        name="in_proj",
    )(x_meta, x_prompt, x_sample, w_in_t, norm_w, w_a2p, b_a)


def _t5_bucket(dist):
    max_exact = N_BUCKETS // 2
    d = jnp.maximum(dist, 0)
    ratio = jnp.maximum(d, max_exact).astype(F32) / max_exact
    val = jnp.log(ratio) / math.log(MAX_DISTANCE / max_exact) * (N_BUCKETS - max_exact)
    large = jnp.full(dist.shape, max_exact, jnp.int32)
    for step in range(1, N_BUCKETS - max_exact):
        large = large + (val >= step).astype(jnp.int32)
    return jnp.where(d < max_exact, d, large)


SWA_BLOCKS_PER_STEP = 2
HEAD_SHIFT = HEAD_DIM.bit_length() - 1


def _swa_prompt_kernel(rb_ref, sink_ref, qg_ref, kvp_ref, kvc_ref, o_ref, tab_ref):
    b = pl.program_id(0)
    m = pl.program_id(1)
    W2 = 2 * WINDOW
    NL = N_KV_HEADS * W2
    n_sub = qg_ref.shape[0] // WINDOW

    @pl.when((b == 0) & (m == 0))
    def _build_bias_tables():
        qi = lax.broadcasted_iota(jnp.int32, (WINDOW, W2), 0)
        kj = lax.broadcasted_iota(jnp.int32, (WINDOW, W2), 1)
        dist = qi - kj + WINDOW
        valid = (dist >= 0) & (dist < WINDOW)
        bucket = _t5_bucket(dist)
        for g in range(GQA_GROUP):
            tab_ref[1, g] = jnp.full((WINDOW, NL), NEG, F32)

        def body(bk, carry):
            hit = (bucket == bk) & valid
            for g in range(GQA_GROUP):
                for h in range(N_KV_HEADS):
                    cur = tab_ref[1, g, :, h * W2:(h + 1) * W2]
                    tab_ref[1, g, :, h * W2:(h + 1) * W2] = jnp.where(hit, rb_ref[bk, h * GQA_GROUP + g], cur)
            return carry

        lax.fori_loop(0, N_BUCKETS, body, 0)
        kvalid = kj - FRONT_PAD >= 0
        for g in range(GQA_GROUP):
            for h in range(N_KV_HEADS):
                tab_ref[0, g, :, h * W2:(h + 1) * W2] = jnp.where(
                    kvalid, tab_ref[1, g, :, h * W2:(h + 1) * W2], NEG)

    lane_head = lax.shift_right_logical(lax.broadcasted_iota(jnp.int32, (1, KV_WIDTH), 1), HEAD_SHIFT)
    zero = jnp.zeros((), BF16)

    def per_head(x):
        return [jnp.where(lane_head == h, x, zero) for h in range(N_KV_HEADS)]

    def rows(r):
        return slice(r * WINDOW, (r + 1) * WINDOW)

    km = [per_head(kvp_ref[:, :KV_WIDTH])] + [per_head(kvc_ref[rows(r), :KV_WIDTH]) for r in range(n_sub)]
    vm = [per_head(kvp_ref[:, KV_WIDTH:])] + [per_head(kvc_ref[rows(r), KV_WIDTH:]) for r in range(n_sub)]

    def stacked(parts, r):
        return jnp.concatenate([blk for h in range(N_KV_HEADS) for blk in (parts[r][h], parts[r + 1][h])], axis=0)

    scale = HEAD_DIM ** -0.5
    scores = {}
    for r in range(n_sub):
        kkm = stacked(km, r)
        tv = jnp.where(m == 0, 0, 1) if r == 0 else 1
        for g in range(GQA_GROUP):
            qg = (qg_ref[rows(r), g * KV_WIDTH:(g + 1) * KV_WIDTH].astype(F32) * scale).astype(BF16)
            scores[r, g] = _nt_dot(qg, kkm) + tab_ref[tv, g]
    for r in range(n_sub):
        vvm = stacked(vm, r)
        for g in range(GQA_GROUP):
            probs = []
            for h in range(N_KV_HEADS):
                s = scores[r, g][:, h * W2:(h + 1) * W2]
                sink = sink_ref[h * GQA_GROUP + g]
                mx = jnp.maximum(jnp.max(s, axis=-1, keepdims=True), sink)
                e = jnp.exp(s - mx)
                den = jnp.sum(e, axis=-1, keepdims=True) + jnp.exp(sink - mx)
                probs.append((e * (1.0 / den)).astype(BF16))
            p = jnp.concatenate(probs, axis=1)
            o = jnp.dot(p, vvm, preferred_element_type=F32)
            gate = qg_ref[rows(r), ATT_WIDTH + g * KV_WIDTH:ATT_WIDTH + (g + 1) * KV_WIDTH].astype(F32)
            o_ref[rows(r), g * KV_WIDTH:(g + 1) * KV_WIDTH] = (o * (gate * _sigmoid(gate))).astype(BF16)


def _swa_prompt(p_all, rel_bias, sinks, n_batch, n_blocks, meta_blk):
    nb = SWA_BLOCKS_PER_STEP
    n_real = n_blocks - 1
    rows = nb * WINDOW
    assert n_real % nb == 0 and (meta_blk + 1) % nb == 0
    assert COL_GATE_A == COL_QA + ATT_WIDTH and COL_VA == COL_KA + KV_WIDTH and COL_QA == 0
    steps = n_real // nb
    first = (meta_blk + 1) // nb

    def cur(b, m):
        return first + b * steps + m

    def prev(b, m):
        return jnp.where(m == 0, meta_blk, meta_blk + b * n_real + nb * m)

    return pl.pallas_call(
        _swa_prompt_kernel,
        grid=(n_batch, steps),
        in_specs=[
            pl.BlockSpec(memory_space=pltpu.SMEM),
            pl.BlockSpec(memory_space=pltpu.SMEM),
            pl.BlockSpec((rows, 2 * ATT_WIDTH), lambda b, m: (cur(b, m), 0)),
            pl.BlockSpec((WINDOW, 2 * KV_WIDTH), lambda b, m: (prev(b, m), COL_KA // (2 * KV_WIDTH))),
            pl.BlockSpec((rows, 2 * KV_WIDTH), lambda b, m: (cur(b, m), COL_KA // (2 * KV_WIDTH))),
        ],
        out_specs=pl.BlockSpec((rows, ATT_WIDTH), lambda b, m: (b * steps + m, 0)),
        out_shape=jax.ShapeDtypeStruct((n_batch * n_real * WINDOW, ATT_WIDTH), BF16),
        scratch_shapes=[pltpu.VMEM((2, GQA_GROUP, WINDOW, N_KV_HEADS * 2 * WINDOW), F32)],
        compiler_params=pltpu.CompilerParams(
            dimension_semantics=("arbitrary", "arbitrary"), vmem_limit_bytes=VMEM_LIMIT),
        name="swa_prompt",
    )(rel_bias, sinks, p_all, p_all, p_all)


def _diag_scores(q3, k3, b3, width):
    nb = q3.shape[0]
    n_heads = q3.shape[2] // GLA_DK
    r_io = lax.broadcasted_iota(jnp.int32, (nb, 8, width), 0)
    t_io = lax.broadcasted_iota(jnp.int32, (nb, 8, width), 1)
    l_io = lax.broadcasted_iota(jnp.int32, (nb, 8, width), 2)
    s_of_lane = l_io - 8 * r_io
    outs = [jnp.zeros((nb, 8, width), F32) for _ in range(n_heads)]
    for s in range(8):
        bs = jnp.broadcast_to(b3[:, s:s + 1, :], b3.shape)
        ks = jnp.broadcast_to(k3[:, s:s + 1, :], k3.shape)
        e = jnp.exp(jnp.minimum(b3 - bs, 0.0))
        p = q3 * e * ks
        for h in range(n_heads):
            col = jnp.sum(p[:, :, h * GLA_DK:(h + 1) * GLA_DK], axis=-1, keepdims=True)
            outs[h] = jnp.where(s_of_lane == s, col, outs[h])
    causal = (s_of_lane >= 0) & (s_of_lane <= t_io) & (s_of_lane < 8)
    return [jnp.where(causal, o, 0.0) for o in outs]


def _gla_finish(o, gn, gate):
    var = jnp.mean(o * o, axis=-1, keepdims=True)
    on = o * lax.rsqrt(var + EPS) * gn
    return (on * (gate * _sigmoid(gate))).astype(BF16)


def _decay_column(row):
    eye = (lax.broadcasted_iota(jnp.int32, (GLA_DK, GLA_DK), 0)
           == lax.broadcasted_iota(jnp.int32, (GLA_DK, GLA_DK), 1))
    return jnp.sum(jnp.where(eye, jnp.broadcast_to(row, (GLA_DK, GLA_DK)), 0.0), axis=1, keepdims=True)


def _gla_prompt_kernel(*refs):
    o_ref, s_ref = refs[-2:]
    gn_ref = refs[-3]
    n_batch = (len(refs) - 3) // 5
    qg_refs, kg_refs, vg_refs, gate_refs, lf_refs = (refs[j:5 * n_batch:5] for j in range(5))
    c = pl.program_id(0)
    C = GLA_CHUNK

    @pl.when(c == 0)
    def _():
        s_ref[...] = jnp.zeros(s_ref.shape, F32)

    row = lax.broadcasted_iota(jnp.int32, (C, 1), 0)
    valid = (c * C + row) >= FRONT_PAD
    tril = (lax.broadcasted_iota(jnp.int32, (C, C), 0)
            >= lax.broadcasted_iota(jnp.int32, (C, C), 1)).astype(F32).astype(BF16)
    t_io = lax.broadcasted_iota(jnp.int32, (C, C), 0)
    s_io = lax.broadcasted_iota(jnp.int32, (C, C), 1)
    n_levels = int(math.log2(C))
    level_masks = [(t_io > s_io) & (lax.shift_right_logical(t_io ^ s_io, sh) == 1) for sh in range(n_levels)]
    sub_io = lax.broadcasted_iota(jnp.int32, (C // 8, 8, 1), 1)
    gn = gn_ref[...]

    heads = [(b, h) for b in range(n_batch) for h in range(GLA_HEADS)]

    def ksl(h):
        return slice(h * GLA_DK, (h + 1) * GLA_DK)

    def vsl(h):
        return slice(h * GLA_DV, (h + 1) * GLA_DV)

    qs, kks, bcs = [], [], []
    for b in range(n_batch):
        qs.append(qg_refs[b][...].astype(F32) * (GLA_DK ** -0.5))
        kks.append(jnp.where(valid, kg_refs[b][...].astype(F32), 0.0))
        g = jnp.where(valid, lf_refs[b][...], 0.0)
        g1 = g.astype(BF16)
        r1 = g - g1.astype(F32)
        g2 = r1.astype(BF16)
        g3 = (r1 - g2.astype(F32)).astype(BF16)
        bcs.append(jnp.dot(tril, g1, preferred_element_type=F32)
                   + jnp.dot(tril, g2, preferred_element_type=F32)
                   + jnp.dot(tril, g3, preferred_element_type=F32))

    qts, kts, yss, qks, b_lasts = [], [], [], [], []
    for b in range(n_batch):
        q, k, bc = qs[b], kks[b], bcs[b]
        b_last = bc[C - 1:C, :]
        b_lasts.append(b_last)
        qts.append((q * jnp.exp(bc)).astype(BF16))
        kts.append((k * jnp.exp(b_last - bc)).astype(BF16))
        b3 = bc.reshape(C // 8, 8, -1)
        ys = []
        for sh in range(n_levels):
            half = 1 << sh
            if half >= 8:
                ref = jnp.concatenate(
                    [jnp.broadcast_to(bc[i + half - 1:i + half, :], (2 * half, bc.shape[1]))
                     for i in range(0, C, 2 * half)], axis=0)
            elif half == 4:
                ref = jnp.broadcast_to(b3[:, 3:4, :], b3.shape).reshape(bc.shape)
            elif half == 2:
                ref = jnp.where(sub_io >= 4, jnp.broadcast_to(b3[:, 5:6, :], b3.shape),
                                jnp.broadcast_to(b3[:, 1:2, :], b3.shape)).reshape(bc.shape)
            else:
                ref = jnp.where((row & 1) == 1, pltpu.roll(bc, 1, axis=0), bc)
            upper = (lax.shift_right_logical(row, sh) & 1) == 1
            ys.append((jnp.where(upper, q, k) * jnp.exp(-jnp.abs(bc - ref))).astype(BF16))
        yss.append(ys)
        qks.append(q * k)

    level_prods, inters = {}, {}
    for b, h in heads:
        level_prods[b, h] = [_nt_dot(yss[b][sh][:, ksl(h)], yss[b][sh][:, ksl(h)]) for sh in range(n_levels)]
    for b, h in heads:
        st = s_ref[b, h]
        inters[b, h] = jnp.dot(qts[b][:, ksl(h)], st.astype(BF16), preferred_element_type=F32)
        s_ref[b, h] = (_decay_column(jnp.exp(b_lasts[b][:, ksl(h)])) * st
                       + _tn_dot(kts[b][:, ksl(h)], vg_refs[b][:, vsl(h)]))

    for b, h in heads:
        a = jnp.where(t_io == s_io, jnp.sum(qks[b][:, ksl(h)], axis=-1, keepdims=True), 0.0)
        for sh in range(n_levels):
            a = jnp.where(level_masks[sh], level_prods[b, h][sh], a)
        o = jnp.dot(a.astype(BF16), vg_refs[b][:, vsl(h)], preferred_element_type=F32) + inters[b, h]
        o_ref[b, :, vsl(h)] = _gla_finish(o, gn, gate_refs[b][:, vsl(h)].astype(F32))


def _gla_prompt(p_all, lf_all, gla_norm_row, n_batch, n_chunks, meta_blk):
    skip = (FRONT_PAD + N_META) // GLA_CHUNK
    out_len = (n_chunks - skip) * GLA_CHUNK
    C = GLA_CHUNK
    per_blk = WINDOW // C

    def chunk(b):
        def f(c):
            return jnp.where(c < skip, meta_blk * per_blk + c, (meta_blk + 1) * per_blk + b * (n_chunks - skip) + c - skip)
        return f

    in_specs = []
    args = []
    for b in range(n_batch):
        f = chunk(b)
        in_specs += [
            pl.BlockSpec((C, GLA_KEY_WIDTH), lambda c, f=f: (f(c), COL_QG // GLA_KEY_WIDTH)),
            pl.BlockSpec((C, GLA_KEY_WIDTH), lambda c, f=f: (f(c), COL_KG // GLA_KEY_WIDTH)),
            pl.BlockSpec((C, GLA_WIDTH), lambda c, f=f: (f(c), COL_VG // GLA_WIDTH)),
            pl.BlockSpec((C, GLA_WIDTH), lambda c, f=f: (f(c), COL_GATE_G // GLA_WIDTH)),
            pl.BlockSpec((C, GLA_KEY_WIDTH), lambda c, f=f: (f(c), 0)),
        ]
        args += [p_all, p_all, p_all, p_all, lf_all]
    return pl.pallas_call(
        _gla_prompt_kernel,
        grid=(n_chunks,),
        in_specs=in_specs + [pl.BlockSpec((1, GLA_DV), lambda c: (0, 0))],
        out_specs=[
            pl.BlockSpec((n_batch, C, GLA_WIDTH), lambda c: (0, jnp.maximum(c - skip, 0), 0)),
            pl.BlockSpec((n_batch, GLA_HEADS, GLA_DK, GLA_DV), lambda c: (0, 0, 0, 0)),
        ],
        out_shape=[
            jax.ShapeDtypeStruct((n_batch, out_len, GLA_WIDTH), BF16),
            jax.ShapeDtypeStruct((n_batch, GLA_HEADS, GLA_DK, GLA_DV), F32),
        ],
        compiler_params=pltpu.CompilerParams(
            dimension_semantics=("arbitrary",), vmem_limit_bytes=VMEM_LIMIT),
        name="gla_prompt",
    )(*args, gla_norm_row)


OUT_PROJ_TM = 256
WO_CHUNK = 512


def _out_proj_kernel(map_ref, mgp_ref, xp_ref, mas_ref, mgs_ref, xs_ref, nw_ref, w_hbm,
                     yp_ref, ys_ref, wbf_ref, stage_ref, wsem, *, n_p):
    i = pl.program_id(0)

    @pl.when(i == 0)
    def _load_weight():
        chunks = [_group_major_pieces(0, d, WO_CHUNK) for d in range(0, ATT_WIDTH, WO_CHUNK)]
        chunks += [[(d, 0, WO_CHUNK)] for d in range(ATT_WIDTH, D_MODEL, WO_CHUNK)]
        _load_rows_as_bf16(w_hbm, chunks, stage_ref, wsem, wbf_ref, WO_CHUNK)

    def finish(ma_ref, mg_ref, x_ref, y_ref):
        o = (jnp.dot(ma_ref[...], wbf_ref[:ATT_WIDTH, :], preferred_element_type=F32)
             + jnp.dot(mg_ref[...], wbf_ref[ATT_WIDTH:, :], preferred_element_type=F32))
        var = jnp.mean(o * o, axis=-1, keepdims=True)
        y_ref[...] = x_ref[...] + o * lax.rsqrt(var + EPS) * nw_ref[...]

    @pl.when(i < n_p)
    def _():
        finish(map_ref, mgp_ref, xp_ref, yp_ref)

    @pl.when(i >= n_p)
    def _():
        finish(mas_ref, mgs_ref, xs_ref, ys_ref)


def _out_proj(ma_p, mg_p, x_p, ma_s, mg_s, x_s, w_out_l, norm_w):
    tm = OUT_PROJ_TM
    n_p = x_p.shape[0] // tm
    n_s = x_s.shape[0] // tm
    assert x_p.shape[0] % tm == 0 and x_s.shape[0] % tm == 0 and ATT_WIDTH + GLA_WIDTH == D_MODEL

    def p_blk(i):
        return (jnp.minimum(i, n_p - 1), 0)

    def s_blk(i):
        return (jnp.maximum(i - n_p, 0), 0)

    row_bytes = ATT_WIDTH * 2 + GLA_WIDTH * 2 + 2 * D_MODEL * 4
    vmem_bytes = (D_MODEL * D_MODEL * 2 + 2 * WO_CHUNK * D_MODEL * 4 + 2 * 2 * tm * row_bytes
                  + 3 * tm * D_MODEL * 4)
    return pl.pallas_call(
        functools.partial(_out_proj_kernel, n_p=n_p),
        grid=(n_p + n_s,),
        in_specs=[
            pl.BlockSpec((tm, ATT_WIDTH), p_blk),
            pl.BlockSpec((tm, GLA_WIDTH), p_blk),
            pl.BlockSpec((tm, D_MODEL), p_blk),
            pl.BlockSpec((tm, ATT_WIDTH), s_blk),
            pl.BlockSpec((tm, GLA_WIDTH), s_blk),
            pl.BlockSpec((tm, D_MODEL), s_blk),
            pl.BlockSpec((1, D_MODEL), lambda i: (0, 0)),
            pl.BlockSpec(memory_space=pl.ANY),
        ],
        out_specs=[
            pl.BlockSpec((tm, D_MODEL), p_blk),
            pl.BlockSpec((tm, D_MODEL), s_blk),
        ],
        out_shape=[
            jax.ShapeDtypeStruct(x_p.shape, F32),
            jax.ShapeDtypeStruct(x_s.shape, F32),
        ],
        scratch_shapes=[
            pltpu.VMEM((D_MODEL, D_MODEL), BF16),
            pltpu.VMEM((2, WO_CHUNK, D_MODEL), F32),
            pltpu.SemaphoreType.DMA((2,)),
        ],
        compiler_params=pltpu.CompilerParams(
            dimension_semantics=("arbitrary",), vmem_limit_bytes=vmem_bytes),
        name="out_proj",
    )(ma_p, mg_p, x_p, ma_s, mg_s, x_s, norm_w, w_out_l)


SWA_SAMPLE_SEQS = 8


def _swa_sample_kernel(rb_ref, sink_ref, q_ref, gate_ref, kvn_ref, ckt_ref, cvt_ref,
                       o_ref, kot_ref, vot_ref, tabc_ref, tabn_ref, sinkcol_ref, place_ref, *, t_new):
    step = pl.program_id(0)
    n_seq = q_ref.shape[0]
    W = ckt_ref.shape[2]
    rows = N_HEADS * t_new

    @pl.when(step == 0)
    def _build_tables():
        r = lax.broadcasted_iota(jnp.int32, (rows, W), 0)
        qi = r & (t_new - 1)
        kj = lax.broadcasted_iota(jnp.int32, (rows, W), 1)
        dist_c = qi + W - kj
        valid_c = (dist_c >= 0) & (dist_c < WINDOW)
        bucket_c = _t5_bucket(dist_c)
        rn = lax.broadcasted_iota(jnp.int32, (rows, t_new), 0)
        qn = rn & (t_new - 1)
        jn = lax.broadcasted_iota(jnp.int32, (rows, t_new), 1)
        dist_n = qn - jn
        valid_n = (dist_n >= 0) & (dist_n < WINDOW)
        bucket_n = _t5_bucket(dist_n)
        head_c = lax.shift_right_logical(r, int(math.log2(t_new)))
        head_n = lax.shift_right_logical(rn, int(math.log2(t_new)))
        head_s = lax.shift_right_logical(lax.broadcasted_iota(jnp.int32, (rows, 1), 0), int(math.log2(t_new)))
        tabc_ref[...] = jnp.full((rows, W), NEG, F32)
        tabn_ref[...] = jnp.full((rows, t_new), NEG, F32)
        sinkcol_ref[...] = jnp.zeros((rows, 1), F32)
        for g in range(GQA_GROUP):
            for h in range(N_KV_HEADS):
                hd = h * GQA_GROUP + g
                blk = g * N_KV_HEADS + h
                sinkcol_ref[...] = jnp.where(head_s == blk, sink_ref[hd], sinkcol_ref[...])

        def body(bk, carry):
            for g in range(GQA_GROUP):
                for h in range(N_KV_HEADS):
                    hd = h * GQA_GROUP + g
                    blk = g * N_KV_HEADS + h
                    val = rb_ref[bk, hd]
                    tabc_ref[...] = jnp.where((bucket_c == bk) & valid_c & (head_c == blk), val, tabc_ref[...])
                    tabn_ref[...] = jnp.where((bucket_n == bk) & valid_n & (head_n == blk), val, tabn_ref[...])
            return carry

        lax.fori_loop(0, N_BUCKETS, body, 0)
        src = lax.broadcasted_iota(jnp.int32, place_ref.shape, 0)
        dst = lax.broadcasted_iota(jnp.int32, place_ref.shape, 1)
        want = lax.shift_right_logical(src, int(math.log2(t_new))) * W + (W - t_new) + (src & (t_new - 1))
        place_ref[...] = (dst == want).astype(F32).astype(BF16)

    lane_head = lax.shift_right_logical(lax.broadcasted_iota(jnp.int32, (1, KV_WIDTH), 1), HEAD_SHIFT)
    scale = HEAD_DIM ** -0.5
    sink = sinkcol_ref[...]
    seqs = range(n_seq)
    scores = []
    for s in seqs:
        q = q_ref[s].astype(F32) * scale
        pieces = []
        for g in range(GQA_GROUP):
            qg = q[:, g * KV_WIDTH:(g + 1) * KV_WIDTH]
            for h in range(N_KV_HEADS):
                pieces.append(jnp.where(lane_head == h, qg, 0.0))
        lhs = jnp.concatenate(pieces, axis=0).astype(BF16)
        sc = jnp.dot(lhs, ckt_ref[s].astype(BF16), preferred_element_type=F32) + tabc_ref[...]
        sn = _nt_dot(lhs, kvn_ref[s, :, :KV_WIDTH].astype(BF16)) + tabn_ref[...]
        scores.append((sc, sn))
    probs = []
    for s in seqs:
        sc, sn = scores[s]
        m = jnp.maximum(jnp.maximum(jnp.max(sc, axis=-1, keepdims=True),
                                    jnp.max(sn, axis=-1, keepdims=True)), sink)
        ec = jnp.exp(sc - m)
        en = jnp.exp(sn - m)
        den = (jnp.sum(ec, axis=-1, keepdims=True) + jnp.sum(en, axis=-1, keepdims=True)
               + jnp.exp(sink - m))
        inv = 1.0 / den
        probs.append(((ec * inv).astype(BF16), (en * inv).astype(BF16)))
    for s in seqs:
        pc, pn = probs[s]
        o = (_nt_dot(pc, cvt_ref[s].astype(BF16))
             + jnp.dot(pn, kvn_ref[s, :, KV_WIDTH:].astype(BF16), preferred_element_type=F32))
        outs = []
        for g in range(GQA_GROUP):
            acc = jnp.zeros((t_new, KV_WIDTH), F32)
            for h in range(N_KV_HEADS):
                blk = g * N_KV_HEADS + h
                acc = jnp.where(lane_head == h, o[blk * t_new:(blk + 1) * t_new, :], acc)
            outs.append(acc)
        att = jnp.concatenate(outs, axis=1)
        gate = gate_ref[s].astype(F32)
        o_ref[s] = (att * (gate * _sigmoid(gate))).astype(BF16)

    new_rows = kvn_ref[...].reshape(n_seq * t_new, 2 * KV_WIDTH)
    hi = new_rows.astype(BF16)
    lo = (new_rows - hi.astype(F32)).astype(BF16)
    place = place_ref[...]
    is_new = lax.broadcasted_iota(jnp.int32, (1, W), 1) >= W - t_new
    for src_ref, dst_ref, cols in ((ckt_ref, kot_ref, slice(0, KV_WIDTH)), (cvt_ref, vot_ref, slice(KV_WIDTH, None))):
        placed = _tn_dot(hi[:, cols], place) + _tn_dot(lo[:, cols], place)
        for s in seqs:
            dst_ref[s] = jnp.where(is_new, placed[:, s * W:(s + 1) * W], pltpu.roll(src_ref[s], W - t_new, axis=1))


def _swa_sample(p3, kv3, cache_k, cache_v, rel_bias, sinks, seq0):
    t_new = p3.shape[1]
    db, _, w = cache_k.shape
    sb = SWA_SAMPLE_SEQS
    assert seq0 % sb == 0
    blk0 = seq0 // sb
    assert t_new & (t_new - 1) == 0 and w >= t_new and w <= WINDOW
    sb = SWA_SAMPLE_SEQS
    rows = N_HEADS * t_new
    return pl.pallas_call(
        functools.partial(_swa_sample_kernel, t_new=t_new),
        grid=(db // sb,),
        in_specs=[
            pl.BlockSpec(memory_space=pltpu.SMEM),
            pl.BlockSpec(memory_space=pltpu.SMEM),
            pl.BlockSpec((sb, t_new, ATT_WIDTH), lambda i: (blk0 + i, 0, COL_QA // ATT_WIDTH)),
            pl.BlockSpec((sb, t_new, ATT_WIDTH), lambda i: (blk0 + i, 0, COL_GATE_A // ATT_WIDTH)),
            pl.BlockSpec((sb, t_new, 2 * KV_WIDTH), lambda i: (blk0 + i, 0, 0)),
            pl.BlockSpec((sb, KV_WIDTH, w), lambda i: (i, 0, 0)),
            pl.BlockSpec((sb, KV_WIDTH, w), lambda i: (i, 0, 0)),
        ],
        out_specs=[
            pl.BlockSpec((sb, t_new, ATT_WIDTH), lambda i: (i, 0, 0)),
            pl.BlockSpec((sb, KV_WIDTH, w), lambda i: (i, 0, 0)),
            pl.BlockSpec((sb, KV_WIDTH, w), lambda i: (i, 0, 0)),
        ],
        out_shape=[
            jax.ShapeDtypeStruct((db, t_new, ATT_WIDTH), BF16),
            jax.ShapeDtypeStruct((db, KV_WIDTH, w), F32),
            jax.ShapeDtypeStruct((db, KV_WIDTH, w), F32),
        ],
        scratch_shapes=[
            pltpu.VMEM((rows, w), F32),
            pltpu.VMEM((rows, t_new), F32),
            pltpu.VMEM((rows, 1), F32),
            pltpu.VMEM((sb * t_new, sb * w), BF16),
        ],
        compiler_params=pltpu.CompilerParams(
            dimension_semantics=("arbitrary",), vmem_limit_bytes=VMEM_LIMIT),
        name="swa_sample",
    )(rel_bias, sinks, p3, p3, kv3, cache_k, cache_v)


GLA_SAMPLE_SEQS = 8


def _cumsum_rows8(g):
    row = lax.broadcasted_iota(jnp.int32, g.shape, 0)
    out = g
    for sh in (1, 2, 4):
        out = out + jnp.where(row >= sh, pltpu.roll(out, sh, axis=0), 0.0)
    return out


def _gla_sample_kernel(qg_ref, kg_ref, vg_ref, gate_ref, lf_ref, gn_ref, st_ref, o_ref, so_ref):
    T = qg_ref.shape[1]
    gn = gn_ref[...]
    t_col = lax.broadcasted_iota(jnp.int32, (T, 1), 0)
    zpad = jnp.zeros((8, GLA_DK), F32)
    seqs = range(qg_ref.shape[0])
    pairs = [(s, h) for s in seqs for h in range(GLA_HEADS)]
    zpad_v = jnp.zeros((8, GLA_DV), F32)

    def ksl(h):
        return slice(h * GLA_DK, (h + 1) * GLA_DK)

    def vsl(h):
        return slice(h * GLA_DV, (h + 1) * GLA_DV)

    qts, kts, b_lasts, vs_, cols = [], [], [], [], {}
    for s in seqs:
        q = qg_ref[s].astype(F32) * (GLA_DK ** -0.5)
        k = kg_ref[s].astype(F32)
        bc = _cumsum_rows8(lf_ref[s])
        b_last = bc[T - 1:T, :]
        b_lasts.append(b_last)
        qts.append(q * jnp.exp(bc))
        kts.append(k * jnp.exp(b_last - bc))
        vs_.append(vg_ref[s].astype(F32))
        for j in range(T):
            bs = jnp.broadcast_to(bc[j:j + 1, :], bc.shape)
            ks_ = jnp.broadcast_to(k[j:j + 1, :], k.shape)
            p = q * jnp.exp(jnp.minimum(bc - bs, 0.0)) * ks_
            for h in range(GLA_HEADS):
                col = jnp.sum(p[:, ksl(h)], axis=-1, keepdims=True)
                cols[s, h, j] = jnp.where(t_col >= j, col, 0.0)

    inters = {}
    for s, h in pairs:
        qt16 = jnp.concatenate([qts[s][:, ksl(h)], zpad], axis=0).astype(BF16)
        inters[s, h] = jnp.dot(qt16, st_ref[s, h].astype(BF16), preferred_element_type=F32)[:T, :]
    for s, h in pairs:
        kt16 = jnp.concatenate([kts[s][:, ksl(h)], zpad], axis=0).astype(BF16)
        v16 = jnp.concatenate([vs_[s][:, vsl(h)], zpad_v], axis=0).astype(BF16)
        so_ref[s, h] = (_decay_column(jnp.exp(b_lasts[s][:, ksl(h)])) * st_ref[s, h]
                        + _tn_dot(kt16, v16))

    for s, h in pairs:
        o = inters[s, h]
        v = vs_[s][:, vsl(h)]
        for j in range(T):
            o = o + cols[s, h, j] * v[j:j + 1, :]
        o_ref[s, :, vsl(h)] = _gla_finish(o, gn, gate_ref[s, :, vsl(h)].astype(F32))


def _gla_sample(p3, lf3, state, gla_norm_row, seq0):
    t_new = p3.shape[1]
    db = state.shape[0]
    sb = GLA_SAMPLE_SEQS
    assert t_new == 8 and seq0 % sb == 0
    blk0 = seq0 // sb
    return pl.pallas_call(
        _gla_sample_kernel,
        grid=(db // sb,),
        in_specs=[
            pl.BlockSpec((sb, t_new, GLA_KEY_WIDTH), lambda i: (blk0 + i, 0, COL_QG // GLA_KEY_WIDTH)),
            pl.BlockSpec((sb, t_new, GLA_KEY_WIDTH), lambda i: (blk0 + i, 0, COL_KG // GLA_KEY_WIDTH)),
            pl.BlockSpec((sb, t_new, GLA_WIDTH), lambda i: (blk0 + i, 0, COL_VG // GLA_WIDTH)),
            pl.BlockSpec((sb, t_new, GLA_WIDTH), lambda i: (blk0 + i, 0, COL_GATE_G // GLA_WIDTH)),
            pl.BlockSpec((sb, t_new, GLA_KEY_WIDTH), lambda i: (blk0 + i, 0, 0)),
            pl.BlockSpec((1, GLA_DV), lambda i: (0, 0)),
            pl.BlockSpec((sb, GLA_HEADS, GLA_DK, GLA_DV), lambda i: (i, 0, 0, 0)),
        ],
        out_specs=[
            pl.BlockSpec((sb, t_new, GLA_WIDTH), lambda i: (i, 0, 0)),
            pl.BlockSpec((sb, GLA_HEADS, GLA_DK, GLA_DV), lambda i: (i, 0, 0, 0)),
        ],
        out_shape=[
            jax.ShapeDtypeStruct((db, t_new, GLA_WIDTH), BF16),
            jax.ShapeDtypeStruct((db, GLA_HEADS, GLA_DK, GLA_DV), F32),
        ],
        compiler_params=pltpu.CompilerParams(
            dimension_semantics=("arbitrary",), vmem_limit_bytes=VMEM_LIMIT),
        name="gla_sample",
    )(p3, p3, p3, p3, lf3, gla_norm_row, state)


def _window_feature_major(win):
    db, w = win.shape[:2]
    return win.transpose(0, 2, 3, 1).reshape(db, KV_WIDTH, w)


def _window_from_feature_major(win_t):
    db, _, w = win_t.shape
    return win_t.reshape(db, N_KV_HEADS, HEAD_DIM, w).transpose(0, 3, 1, 2)


def kernel(x_prompt, x_sample, cache_k_win, cache_v_win, state_gla, meta_tokens, rel_bias,
           norm_pre, norm_post, w_in, w_a2, b_a, attn_sinks, gla_norm, w_out):
    n_batch, seq, _ = x_prompt.shape
    db, t_new, _ = x_sample.shape
    depth = w_in.shape[0]
    assert depth == 1
    l = 0
    lp = seq + WINDOW
    n_blocks = lp // WINDOW
    n_chunks = lp // GLA_CHUNK
    w_cache = cache_k_win.shape[2]

    w_in_t = w_in[l].T
    assert sum(IN_SIZES[:-1]) == N_MAIN
    w_a2p = jnp.pad(w_a2[l], ((0, LANES - GLA_RANK), (0, 0))).astype(BF16)
    npre = norm_pre[l][None, :]
    npost = norm_post[l][None, :]
    ba = b_a[l][None, :]
    gn = gla_norm[l][None, :]
    sinks = attn_sinks[l]

    tm = IN_PROJ_TM
    x_meta = jnp.concatenate([jnp.zeros((tm - N_META, D_MODEL), x_prompt.dtype),
                              meta_tokens.astype(x_prompt.dtype)], axis=0)
    p_all, kv_all, lf_all = _in_proj(
        x_meta, x_prompt.reshape(n_batch * seq, D_MODEL), x_sample.reshape(db * t_new, D_MODEL),
        w_in_t, npre, w_a2p, ba)
    meta_blk = tm // WINDOW - 1
    sample_seq0 = (tm + n_batch * seq) // t_new

    mixed_a = _swa_prompt(p_all, rel_bias, sinks, n_batch, n_blocks, meta_blk)
    mixed_g, s_fin = _gla_prompt(p_all, lf_all, gn, n_batch, n_chunks, meta_blk)
    kv_p = jnp.stack([kv_all[tm + (b + 1) * seq - WINDOW:tm + (b + 1) * seq] for b in range(n_batch)])
    k_win_p = kv_p[..., :KV_WIDTH].reshape(1, n_batch, WINDOW, N_KV_HEADS, HEAD_DIM)
    v_win_p = kv_p[..., KV_WIDTH:].reshape(1, n_batch, WINDOW, N_KV_HEADS, HEAD_DIM)
    gla_p = s_fin[None]

    ps3 = p_all.reshape(-1, t_new, N_MAIN)
    mixed_as, k_win_s, v_win_s = _swa_sample(
        ps3, kv_all.reshape(-1, t_new, 2 * KV_WIDTH),
        _window_feature_major(cache_k_win[l]), _window_feature_major(cache_v_win[l]),
        rel_bias, sinks, sample_seq0)
    mixed_gs, s_new = _gla_sample(ps3, lf_all.reshape(-1, t_new, GLA_KEY_WIDTH), state_gla[l], gn, sample_seq0)

    y_prompt, y_sample = _out_proj(
        mixed_a, mixed_g.reshape(n_batch * seq, GLA_WIDTH), x_prompt.reshape(n_batch * seq, D_MODEL),
        mixed_as.reshape(db * t_new, ATT_WIDTH), mixed_gs.reshape(db * t_new, GLA_WIDTH),
        x_sample.reshape(db * t_new, D_MODEL), w_out[l], npost)
    y_prompt = y_prompt.reshape(n_batch, seq, D_MODEL)
    y_sample = y_sample.reshape(db, t_new, D_MODEL)
    k_win_s = _window_from_feature_major(k_win_s)[None]
    v_win_s = _window_from_feature_major(v_win_s)[None]

    return (y_prompt, y_sample, k_win_p, v_win_p, gla_p, k_win_s, v_win_s, s_new[None])
```

```python
import functools
import math

import jax
import jax.numpy as jnp
from jax import lax
from jax.experimental import pallas as pl
from jax.experimental.pallas import tpu as pltpu

F32 = jnp.float32
BF16 = jnp.bfloat16

D_MODEL = 2048
N_META = 16
ATT_WIDTH = 1024
HEAD_DIM = 64
N_HEADS = 16
N_KV_HEADS = 4
GQA_GROUP = 4
KV_WIDTH = 256
WINDOW = 128
N_BUCKETS = 32
MAX_DISTANCE = 128
GLA_WIDTH = 1024
GLA_HEADS = 4
GLA_DV = 256
GLA_DK = 128
GLA_KEY_WIDTH = 512
GLA_RANK = 16
GLA_NORMALIZER = 16.0
GLA_CHUNK = 64
FRONT_PAD = WINDOW - N_META
EPS = 1e-6
PAST_LEN = 8192

IN_SIZES = (ATT_WIDTH, KV_WIDTH, KV_WIDTH, ATT_WIDTH,
            GLA_KEY_WIDTH, GLA_KEY_WIDTH, GLA_WIDTH, GLA_WIDTH, GLA_RANK)

N_MAIN = 5632
TN = 512
COL_QA = 0
COL_GATE_A = 1024
COL_QG = 2048
COL_KG = 2560
COL_VG = 3072
COL_GATE_G = 4096
COL_KA = 5120
COL_VA = 5376
LANES = 128
NEG = -1e30
LOG2E = math.log2(math.e)

VMEM_LIMIT = 48 * 1024 * 1024


def _sigmoid(x):
    return 1.0 / (1.0 + jnp.exp(-x))


def _log_sigmoid(x):
    return jnp.minimum(x, 0.0) - jnp.log1p(jnp.exp(-jnp.abs(x)))


def _nt_dot(a, b):
    return lax.dot_general(a, b, (((1,), (1,)), ((), ())), preferred_element_type=F32)


def _tn_dot(a, b):
    return lax.dot_general(a, b, (((0,), (0,)), ((), ())), preferred_element_type=F32)


def _group_major_pieces(src0, dst0, n_rows):
    pieces = []
    for r in range(dst0, dst0 + n_rows, HEAD_DIM):
        g, h = divmod(r // HEAD_DIM, N_KV_HEADS)
        pieces.append((src0 + (h * GQA_GROUP + g) * HEAD_DIM, r - dst0, HEAD_DIM))
    return pieces


def _load_rows_as_bf16(src_hbm, chunk_pieces, stage_ref, sem, dst_ref, chunk_rows, after_chunk=None):
    def copies(c):
        return [pltpu.make_async_copy(src_hbm.at[pl.ds(src, n), :], stage_ref.at[c % 2, pl.ds(dst, n), :],
                                      sem.at[c % 2]) for src, dst, n in chunk_pieces[c]]

    for cp in copies(0):
        cp.start()
    for c in range(len(chunk_pieces)):
        if c + 1 < len(chunk_pieces):
            for cp in copies(c + 1):
                cp.start()
        for cp in copies(c):
            cp.wait()
        dst_ref[c * chunk_rows:(c + 1) * chunk_rows, :] = stage_ref[c % 2].astype(BF16)
        if after_chunk is not None:
            after_chunk(c)


W_CHUNK = 256
IN_PROJ_TM = 256
GATE_HEAD_AFTER_TILES = 3


def _in_proj_weight_pieces():
    src_off = dict(zip(("q_a", "k_a", "v_a", "gate_a", "q_g", "k_g", "v_g", "gate_g"),
                       (sum(IN_SIZES[:i]) for i in range(8))))
    chunks = []
    for name in ("q_a", "gate_a"):
        for d in range(0, ATT_WIDTH, W_CHUNK):
            chunks.append(_group_major_pieces(src_off[name], d, W_CHUNK))
    for name, width in (("q_g", GLA_KEY_WIDTH), ("k_g", GLA_KEY_WIDTH), ("v_g", GLA_WIDTH),
                        ("gate_g", GLA_WIDTH), ("k_a", KV_WIDTH), ("v_a", KV_WIDTH)):
        for d in range(0, width, W_CHUNK):
            chunks.append([(src_off[name] + d, 0, W_CHUNK)])
    return chunks


def _in_proj_kernel(xm_hbm, xp_hbm, xs_hbm, wt_hbm, nw_ref, wa2_ref, ba_ref,
                    p_ref, kv_ref, lf_ref,
                    wbf_ref, wz_ref, stage_ref, zstage_ref, xbuf_ref, hn0_ref, hn1_ref, lf0_ref, lf1_ref,
                    wsem, zsem, xsem, *, n_p, n_s):
    i = pl.program_id(0)
    tm = xbuf_ref.shape[1]
    n_steps = 1 + n_p + n_s

    def x_copy(step, slot, fn):
        @pl.when(step == 0)
        def _():
            fn(pltpu.make_async_copy(xm_hbm, xbuf_ref.at[slot], xsem.at[slot]))

        @pl.when((step >= 1) & (step <= n_p))
        def _():
            fn(pltpu.make_async_copy(xp_hbm.at[pl.ds((step - 1) * tm, tm), :], xbuf_ref.at[slot], xsem.at[slot]))

        @pl.when(step > n_p)
        def _():
            fn(pltpu.make_async_copy(xs_hbm.at[pl.ds((step - 1 - n_p) * tm, tm), :], xbuf_ref.at[slot],
                                     xsem.at[slot]))

    slot = i % 2
    nxt = 1 - slot

    def rmsnorm_to(x_slot, hn_out):
        x = xbuf_ref[x_slot]
        var = jnp.mean(x * x, axis=-1, keepdims=True)
        hn_out[...] = (x * lax.rsqrt(var + EPS) * nw_ref[...]).astype(BF16)

    def gate_head_to(hn_in, lf_out):
        z = _nt_dot(hn_in[...], wz_ref[...])
        logits = jnp.dot(z.astype(BF16), wa2_ref[...], preferred_element_type=F32) + ba_ref[...]
        lf_out[...] = _log_sigmoid(logits) * (1.0 / GLA_NORMALIZER)

    def project_tile(hn, j):
        acc = _nt_dot(hn, wbf_ref[j * TN:(j + 1) * TN, :])
        p_ref[:, j * TN:(j + 1) * TN] = acc.astype(BF16)
        if j * TN == COL_KA:
            kv_ref[...] = acc

    @pl.when(i == 0)
    def _first_step():
        x_copy(0, 0, lambda cp: cp.start())
        x_copy(1, 1, lambda cp: cp.start())

        z_copy = pltpu.make_async_copy(wt_hbm.at[pl.ds(N_MAIN, GLA_RANK), :], zstage_ref, zsem.at[0])
        z_copy.start()
        wz_ref[...] = jnp.zeros(wz_ref.shape, BF16)
        z_copy.wait()
        wz_ref[:GLA_RANK, :] = zstage_ref[...].astype(BF16)
        x_copy(0, 0, lambda cp: cp.wait())
        rmsnorm_to(0, hn0_ref)
        gate_head_to(hn0_ref, lf0_ref)
        lf_ref[...] = lf0_ref[...]
        hn = hn0_ref[...]
        chunks_per_tile = TN // W_CHUNK

        def project_ready_tile(c):
            if (c + 1) % chunks_per_tile == 0:
                project_tile(hn, c // chunks_per_tile)

        _load_rows_as_bf16(wt_hbm, _in_proj_weight_pieces(), stage_ref, wsem, wbf_ref, W_CHUNK, project_ready_tile)

    @pl.when(i + 2 < n_steps)
    def _prefetch():
        x_copy(i + 2, slot, lambda cp: cp.start())

    @pl.when(i + 1 < n_steps)
    def _():
        x_copy(i + 1, nxt, lambda cp: cp.wait())

    def project(hn_cur, lf_cur, hn_nxt, lf_nxt):
        lf_ref[...] = lf_cur[...]
        hn = hn_cur[...]
        rmsnorm_to(nxt, hn_nxt)
        for j in range(N_MAIN // TN):
            if j == GATE_HEAD_AFTER_TILES:
                gate_head_to(hn_nxt, lf_nxt)
            project_tile(hn, j)

    @pl.when(i == 0)
    def _():
        rmsnorm_to(1, hn1_ref)
        gate_head_to(hn1_ref, lf1_ref)

    @pl.when((slot == 0) & (i > 0))
    def _():
        project(hn0_ref, lf0_ref, hn1_ref, lf1_ref)

    @pl.when(slot == 1)
    def _():
        project(hn1_ref, lf1_ref, hn0_ref, lf0_ref)


def _in_proj(x_meta, x_prompt, x_sample, w_in_t, norm_w, w_a2p, b_a):
    tm = IN_PROJ_TM
    n_p = x_prompt.shape[0] // tm
    n_s = x_sample.shape[0] // tm
    assert x_meta.shape[0] == tm and x_prompt.shape[0] % tm == 0 and x_sample.shape[0] % tm == 0
    assert COL_KA % TN == 0 and len(_in_proj_weight_pieces()) * W_CHUNK == N_MAIN
    m_all = (1 + n_p + n_s) * tm

    def const(i):
        return (0, 0)

    any_spec = pl.BlockSpec(memory_space=pl.ANY)
    buffers = (D_MODEL * N_MAIN * 2 + 2 * D_MODEL * W_CHUNK * 4 + 2 * tm * D_MODEL * 4
               + 2 * tm * N_MAIN * 2 + 2 * 2 * tm * GLA_KEY_WIDTH * 4
               + D_MODEL * LANES * 2 + D_MODEL * GLA_RANK * 4 + 2 * LANES * GLA_KEY_WIDTH * 2)
    buffers += 2 * tm * D_MODEL * 2 + 2 * tm * GLA_KEY_WIDTH * 4
    temps = tm * D_MODEL * (4 + 4) + 2 * tm * TN * 4 + 2 * tm * GLA_KEY_WIDTH * 4
    vmem_bytes = buffers + temps
    return pl.pallas_call(
        functools.partial(_in_proj_kernel, n_p=n_p, n_s=n_s),
        grid=(1 + n_p + n_s,),
        in_specs=[
            any_spec, any_spec, any_spec, any_spec,
            pl.BlockSpec((1, D_MODEL), const),
            pl.BlockSpec((LANES, GLA_KEY_WIDTH), const),
            pl.BlockSpec((1, GLA_KEY_WIDTH), const),
        ],
        out_specs=[
            pl.BlockSpec((tm, N_MAIN), lambda i: (i, 0)),
            pl.BlockSpec((tm, 2 * KV_WIDTH), lambda i: (i, 0)),
            pl.BlockSpec((tm, GLA_KEY_WIDTH), lambda i: (i, 0)),
        ],
        out_shape=[
            jax.ShapeDtypeStruct((m_all, N_MAIN), BF16),
            jax.ShapeDtypeStruct((m_all, 2 * KV_WIDTH), F32),
            jax.ShapeDtypeStruct((m_all, GLA_KEY_WIDTH), F32),
        ],
        scratch_shapes=[
            pltpu.VMEM((N_MAIN, D_MODEL), BF16),
            pltpu.VMEM((LANES, D_MODEL), BF16),
            pltpu.VMEM((2, W_CHUNK, D_MODEL), F32),
            pltpu.VMEM((GLA_RANK, D_MODEL), F32),
            pltpu.VMEM((2, tm, D_MODEL), F32),
            pltpu.VMEM((tm, D_MODEL), BF16),
            pltpu.VMEM((tm, D_MODEL), BF16),
            pltpu.VMEM((tm, GLA_KEY_WIDTH), F32),
            pltpu.VMEM((tm, GLA_KEY_WIDTH), F32),
            pltpu.SemaphoreType.DMA((2,)),
            pltpu.SemaphoreType.DMA((1,)),
            pltpu.SemaphoreType.DMA((2,)),
        ],
        compiler_params=pltpu.CompilerParams(
            dimension_semantics=("arbitrary",), vmem_limit_bytes=vmem_bytes),
        name="in_proj",
    )(x_meta, x_prompt, x_sample, w_in_t, norm_w, w_a2p, b_a)


def _t5_bucket(dist):
    max_exact = N_BUCKETS // 2
    d = jnp.maximum(dist, 0)
    ratio = jnp.maximum(d, max_exact).astype(F32) / max_exact
    val = jnp.log(ratio) / math.log(MAX_DISTANCE / max_exact) * (N_BUCKETS - max_exact)
    large = jnp.full(dist.shape, max_exact, jnp.int32)
    for step in range(1, N_BUCKETS - max_exact):
        large = large + (val >= step).astype(jnp.int32)
    return jnp.where(d < max_exact, d, large)


SWA_BLOCKS_PER_STEP = 2
HEAD_SHIFT = HEAD_DIM.bit_length() - 1


def _swa_prompt_kernel(rb_ref, sink_ref, qg_ref, kvp_ref, kvc_ref, o_ref, tab_ref):
    b = pl.program_id(0)
    m = pl.program_id(1)
    W2 = 2 * WINDOW
    NL = N_KV_HEADS * W2
    n_sub = qg_ref.shape[0] // WINDOW

    @pl.when((b == 0) & (m == 0))
    def _build_bias_tables():
        qi = lax.broadcasted_iota(jnp.int32, (WINDOW, W2), 0)
        kj = lax.broadcasted_iota(jnp.int32, (WINDOW, W2), 1)
        dist = qi - kj + WINDOW
        valid = (dist >= 0) & (dist < WINDOW)
        bucket = _t5_bucket(dist)
        for g in range(GQA_GROUP):
            tab_ref[1, g] = jnp.full((WINDOW, NL), NEG, F32)

        def body(bk, carry):
            hit = (bucket == bk) & valid
            for g in range(GQA_GROUP):
                for h in range(N_KV_HEADS):
                    cur = tab_ref[1, g, :, h * W2:(h + 1) * W2]
                    tab_ref[1, g, :, h * W2:(h + 1) * W2] = jnp.where(
                        hit, rb_ref[bk, h * GQA_GROUP + g] * LOG2E, cur)
            return carry

        lax.fori_loop(0, N_BUCKETS, body, 0)
        kvalid = kj - FRONT_PAD >= 0
        for g in range(GQA_GROUP):
            for h in range(N_KV_HEADS):
                tab_ref[0, g, :, h * W2:(h + 1) * W2] = jnp.where(
                    kvalid, tab_ref[1, g, :, h * W2:(h + 1) * W2], NEG)

    lane_head = lax.shift_right_logical(lax.broadcasted_iota(jnp.int32, (1, KV_WIDTH), 1), HEAD_SHIFT)
    zero = jnp.zeros((), BF16)

    def per_head(x):
        return [jnp.where(lane_head == h, x, zero) for h in range(N_KV_HEADS)]

    def rows(r):
        return slice(r * WINDOW, (r + 1) * WINDOW)

    km = [per_head(kvp_ref[:, :KV_WIDTH])] + [per_head(kvc_ref[rows(r), :KV_WIDTH]) for r in range(n_sub)]
    vm = [per_head(kvp_ref[:, KV_WIDTH:])] + [per_head(kvc_ref[rows(r), KV_WIDTH:]) for r in range(n_sub)]

    def stacked(parts, r):
        return jnp.concatenate([blk for h in range(N_KV_HEADS) for blk in (parts[r][h], parts[r + 1][h])], axis=0)

    scale = HEAD_DIM ** -0.5 * LOG2E
    scores = {}
    for r in range(n_sub):
        kkm = stacked(km, r)
        tv = jnp.where(m == 0, 0, 1) if r == 0 else 1
        for g in range(GQA_GROUP):
            qg = (qg_ref[rows(r), g * KV_WIDTH:(g + 1) * KV_WIDTH].astype(F32) * scale).astype(BF16)
            scores[r, g] = _nt_dot(qg, kkm) + tab_ref[tv, g]
    heads = [(g, h) for g in range(GQA_GROUP) for h in range(N_KV_HEADS)]
    sinks = {(g, h): sink_ref[h * GQA_GROUP + g] * LOG2E for g, h in heads}
    for r in range(n_sub):
        vvm = stacked(vm, r)
        sl = {(g, h): scores[r, g][:, h * W2:(h + 1) * W2] for g, h in heads}
        mx = {gh: jnp.maximum(jnp.max(sl[gh], axis=-1, keepdims=True), sinks[gh]) for gh in heads}
        ex = {gh: jnp.exp2(sl[gh] - mx[gh]) for gh in heads}
        den = {gh: jnp.sum(ex[gh], axis=-1, keepdims=True) + jnp.exp2(sinks[gh] - mx[gh]) for gh in heads}
        for g in range(GQA_GROUP):
            p = jnp.concatenate([ex[g, h].astype(BF16) for h in range(N_KV_HEADS)], axis=1)
            o = jnp.dot(p, vvm, preferred_element_type=F32)
            inv = jnp.zeros((WINDOW, KV_WIDTH), F32)
            for h in range(N_KV_HEADS):
                inv = jnp.where(lane_head == h, 1.0 / den[g, h], inv)
            gate = qg_ref[rows(r), ATT_WIDTH + g * KV_WIDTH:ATT_WIDTH + (g + 1) * KV_WIDTH].astype(F32)
            o_ref[rows(r), g * KV_WIDTH:(g + 1) * KV_WIDTH] = (o * inv * (gate * _sigmoid(gate))).astype(BF16)


def _swa_prompt(p_all, rel_bias, sinks, n_batch, n_blocks, meta_blk):
    nb = SWA_BLOCKS_PER_STEP
    n_real = n_blocks - 1
    rows = nb * WINDOW
    assert n_real % nb == 0 and (meta_blk + 1) % nb == 0
    assert COL_GATE_A == COL_QA + ATT_WIDTH and COL_VA == COL_KA + KV_WIDTH and COL_QA == 0
    steps = n_real // nb
    first = (meta_blk + 1) // nb

    def cur(b, m):
        return first + b * steps + m

    def prev(b, m):
        return jnp.where(m == 0, meta_blk, meta_blk + b * n_real + nb * m)

    return pl.pallas_call(
        _swa_prompt_kernel,
        grid=(n_batch, steps),
        in_specs=[
            pl.BlockSpec(memory_space=pltpu.SMEM),
            pl.BlockSpec(memory_space=pltpu.SMEM),
            pl.BlockSpec((rows, 2 * ATT_WIDTH), lambda b, m: (cur(b, m), 0)),
            pl.BlockSpec((WINDOW, 2 * KV_WIDTH), lambda b, m: (prev(b, m), COL_KA // (2 * KV_WIDTH))),
            pl.BlockSpec((rows, 2 * KV_WIDTH), lambda b, m: (cur(b, m), COL_KA // (2 * KV_WIDTH))),
        ],
        out_specs=pl.BlockSpec((rows, ATT_WIDTH), lambda b, m: (b * steps + m, 0)),
        out_shape=jax.ShapeDtypeStruct((n_batch * n_real * WINDOW, ATT_WIDTH), BF16),
        scratch_shapes=[pltpu.VMEM((2, GQA_GROUP, WINDOW, N_KV_HEADS * 2 * WINDOW), F32)],
        compiler_params=pltpu.CompilerParams(
            dimension_semantics=("arbitrary", "arbitrary"), vmem_limit_bytes=VMEM_LIMIT),
        name="swa_prompt",
    )(rel_bias, sinks, p_all, p_all, p_all)


def _diag_scores(q3, k3, b3, width):
    nb = q3.shape[0]
    n_heads = q3.shape[2] // GLA_DK
    r_io = lax.broadcasted_iota(jnp.int32, (nb, 8, width), 0)
    t_io = lax.broadcasted_iota(jnp.int32, (nb, 8, width), 1)
    l_io = lax.broadcasted_iota(jnp.int32, (nb, 8, width), 2)
    s_of_lane = l_io - 8 * r_io
    outs = [jnp.zeros((nb, 8, width), F32) for _ in range(n_heads)]
    for s in range(8):
        bs = jnp.broadcast_to(b3[:, s:s + 1, :], b3.shape)
        ks = jnp.broadcast_to(k3[:, s:s + 1, :], k3.shape)
        e = jnp.exp(jnp.minimum(b3 - bs, 0.0))
        p = q3 * e * ks
        for h in range(n_heads):
            col = jnp.sum(p[:, :, h * GLA_DK:(h + 1) * GLA_DK], axis=-1, keepdims=True)
            outs[h] = jnp.where(s_of_lane == s, col, outs[h])
    causal = (s_of_lane >= 0) & (s_of_lane <= t_io) & (s_of_lane < 8)
    return [jnp.where(causal, o, 0.0) for o in outs]


def _gla_finish(o, gn, gate):
    var = jnp.mean(o * o, axis=-1, keepdims=True)
    on = o * lax.rsqrt(var + EPS) * gn
    return (on * (gate * _sigmoid(gate))).astype(BF16)


def _decay_column(row):
    eye = (lax.broadcasted_iota(jnp.int32, (GLA_DK, GLA_DK), 0)
           == lax.broadcasted_iota(jnp.int32, (GLA_DK, GLA_DK), 1))
    return jnp.sum(jnp.where(eye, jnp.broadcast_to(row, (GLA_DK, GLA_DK)), 0.0), axis=1, keepdims=True)


def _gla_prompt_kernel(*refs):
    o_ref, s_ref = refs[-2:]
    gn_ref = refs[-3]
    n_batch = (len(refs) - 3) // 5
    qg_refs, kg_refs, vg_refs, gate_refs, lf_refs = (refs[j:5 * n_batch:5] for j in range(5))
    c = pl.program_id(0)
    C = GLA_CHUNK

    @pl.when(c == 0)
    def _():
        s_ref[...] = jnp.zeros(s_ref.shape, F32)

    row = lax.broadcasted_iota(jnp.int32, (C, 1), 0)
    valid = (c * C + row) >= FRONT_PAD
    tril = (lax.broadcasted_iota(jnp.int32, (C, C), 0)
            >= lax.broadcasted_iota(jnp.int32, (C, C), 1)).astype(F32).astype(BF16)
    t_io = lax.broadcasted_iota(jnp.int32, (C, C), 0)
    s_io = lax.broadcasted_iota(jnp.int32, (C, C), 1)
    n_levels = int(math.log2(C))
    level_masks = [(t_io > s_io) & (lax.shift_right_logical(t_io ^ s_io, sh) == 1) for sh in range(n_levels)]
    sub_io = lax.broadcasted_iota(jnp.int32, (C // 8, 8, 1), 1)
    gn = gn_ref[...]

    heads = [(b, h) for b in range(n_batch) for h in range(GLA_HEADS)]

    def ksl(h):
        return slice(h * GLA_DK, (h + 1) * GLA_DK)

    def vsl(h):
        return slice(h * GLA_DV, (h + 1) * GLA_DV)

    qs, kks, bcs = [], [], []
    for b in range(n_batch):
        qs.append(qg_refs[b][...].astype(F32) * (GLA_DK ** -0.5))
        kks.append(jnp.where(valid, kg_refs[b][...].astype(F32), 0.0))
        g = jnp.where(valid, lf_refs[b][...], 0.0)
        g1 = g.astype(BF16)
        r1 = g - g1.astype(F32)
        g2 = r1.astype(BF16)
        g3 = (r1 - g2.astype(F32)).astype(BF16)
        bcs.append(jnp.dot(tril, g1, preferred_element_type=F32)
                   + jnp.dot(tril, g2, preferred_element_type=F32)
                   + jnp.dot(tril, g3, preferred_element_type=F32))

    qts, kts, yss, qks, b_lasts = [], [], [], [], []
    for b in range(n_batch):
        q, k, bc = qs[b], kks[b], bcs[b]
        b_last = bc[C - 1:C, :]
        b_lasts.append(b_last)
        qts.append((q * jnp.exp(bc)).astype(BF16))
        kts.append((k * jnp.exp(b_last - bc)).astype(BF16))
        b3 = bc.reshape(C // 8, 8, -1)
        ys = []
        for sh in range(n_levels):
            half = 1 << sh
            if half >= 8:
                ref = jnp.concatenate(
                    [jnp.broadcast_to(bc[i + half - 1:i + half, :], (2 * half, bc.shape[1]))
                     for i in range(0, C, 2 * half)], axis=0)
            elif half == 4:
                ref = jnp.broadcast_to(b3[:, 3:4, :], b3.shape).reshape(bc.shape)
            elif half == 2:
                ref = jnp.where(sub_io >= 4, jnp.broadcast_to(b3[:, 5:6, :], b3.shape),
                                jnp.broadcast_to(b3[:, 1:2, :], b3.shape)).reshape(bc.shape)
            else:
                ref = jnp.where((row & 1) == 1, pltpu.roll(bc, 1, axis=0), bc)
            upper = (lax.shift_right_logical(row, sh) & 1) == 1
            ys.append((jnp.where(upper, q, k) * jnp.exp(-jnp.abs(bc - ref))).astype(BF16))
        yss.append(ys)
        qks.append(q * k)

    level_prods, inters = {}, {}
    for b, h in heads:
        level_prods[b, h] = [_nt_dot(yss[b][sh][:, ksl(h)], yss[b][sh][:, ksl(h)]) for sh in range(n_levels)]
    for b, h in heads:
        st = s_ref[b, h]
        inters[b, h] = jnp.dot(qts[b][:, ksl(h)], st.astype(BF16), preferred_element_type=F32)
        s_ref[b, h] = (_decay_column(jnp.exp(b_lasts[b][:, ksl(h)])) * st
                       + _tn_dot(kts[b][:, ksl(h)], vg_refs[b][:, vsl(h)]))

    for b, h in heads:
        a = jnp.where(t_io == s_io, jnp.sum(qks[b][:, ksl(h)], axis=-1, keepdims=True), 0.0)
        for sh in range(n_levels):
            a = jnp.where(level_masks[sh], level_prods[b, h][sh], a)
        o = jnp.dot(a.astype(BF16), vg_refs[b][:, vsl(h)], preferred_element_type=F32) + inters[b, h]
        o_ref[b, :, vsl(h)] = _gla_finish(o, gn, gate_refs[b][:, vsl(h)].astype(F32))


def _gla_prompt(p_all, lf_all, gla_norm_row, n_batch, n_chunks, meta_blk):
    skip = (FRONT_PAD + N_META) // GLA_CHUNK
    out_len = (n_chunks - skip) * GLA_CHUNK
    C = GLA_CHUNK
    per_blk = WINDOW // C

    def chunk(b):
        def f(c):
            return jnp.where(c < skip, meta_blk * per_blk + c, (meta_blk + 1) * per_blk + b * (n_chunks - skip) + c - skip)
        return f

    in_specs = []
    args = []
    for b in range(n_batch):
        f = chunk(b)
        in_specs += [
            pl.BlockSpec((C, GLA_KEY_WIDTH), lambda c, f=f: (f(c), COL_QG // GLA_KEY_WIDTH)),
            pl.BlockSpec((C, GLA_KEY_WIDTH), lambda c, f=f: (f(c), COL_KG // GLA_KEY_WIDTH)),
            pl.BlockSpec((C, GLA_WIDTH), lambda c, f=f: (f(c), COL_VG // GLA_WIDTH)),
            pl.BlockSpec((C, GLA_WIDTH), lambda c, f=f: (f(c), COL_GATE_G // GLA_WIDTH)),
            pl.BlockSpec((C, GLA_KEY_WIDTH), lambda c, f=f: (f(c), 0)),
        ]
        args += [p_all, p_all, p_all, p_all, lf_all]
    return pl.pallas_call(
        _gla_prompt_kernel,
        grid=(n_chunks,),
        in_specs=in_specs + [pl.BlockSpec((1, GLA_DV), lambda c: (0, 0))],
        out_specs=[
            pl.BlockSpec((n_batch, C, GLA_WIDTH), lambda c: (0, jnp.maximum(c - skip, 0), 0)),
            pl.BlockSpec((n_batch, GLA_HEADS, GLA_DK, GLA_DV), lambda c: (0, 0, 0, 0)),
        ],
        out_shape=[
            jax.ShapeDtypeStruct((n_batch, out_len, GLA_WIDTH), BF16),
            jax.ShapeDtypeStruct((n_batch, GLA_HEADS, GLA_DK, GLA_DV), F32),
        ],
        compiler_params=pltpu.CompilerParams(
            dimension_semantics=("arbitrary",), vmem_limit_bytes=VMEM_LIMIT),
        name="gla_prompt",
    )(*args, gla_norm_row)


OUT_PROJ_TM = 256
WO_CHUNK = 512


def _out_proj_kernel(map_ref, mgp_ref, xp_ref, mas_ref, mgs_ref, xs_ref, nw_ref, w_hbm,
                     yp_ref, ys_ref, wbf_ref, stage_ref, wsem, *, n_p):
    i = pl.program_id(0)

    def normalise_add(o, x_ref, y_ref):
        var = jnp.mean(o * o, axis=-1, keepdims=True)
        y_ref[...] = x_ref[...] + o * lax.rsqrt(var + EPS) * nw_ref[...]

    @pl.when(i == 0)
    def _first_step():
        chunks = [_group_major_pieces(0, d, WO_CHUNK) for d in range(0, ATT_WIDTH, WO_CHUNK)]
        chunks += [[(d, 0, WO_CHUNK)] for d in range(ATT_WIDTH, D_MODEL, WO_CHUNK)]
        partial = []

        def contract_chunk(c):
            r0 = c * WO_CHUNK
            src, off = (map_ref, r0) if r0 < ATT_WIDTH else (mgp_ref, r0 - ATT_WIDTH)
            part = jnp.dot(src[:, off:off + WO_CHUNK], wbf_ref[r0:r0 + WO_CHUNK, :], preferred_element_type=F32)
            partial[:] = [part if not partial else partial[0] + part]

        _load_rows_as_bf16(w_hbm, chunks, stage_ref, wsem, wbf_ref, WO_CHUNK, contract_chunk)
        normalise_add(partial[0], xp_ref, yp_ref)

    def finish(ma_ref, mg_ref, x_ref, y_ref):
        o = (jnp.dot(ma_ref[...], wbf_ref[:ATT_WIDTH, :], preferred_element_type=F32)
             + jnp.dot(mg_ref[...], wbf_ref[ATT_WIDTH:, :], preferred_element_type=F32))
        normalise_add(o, x_ref, y_ref)

    @pl.when((i > 0) & (i < n_p))
    def _():
        finish(map_ref, mgp_ref, xp_ref, yp_ref)

    @pl.when(i >= n_p)
    def _():
        finish(mas_ref, mgs_ref, xs_ref, ys_ref)


def _out_proj(ma_p, mg_p, x_p, ma_s, mg_s, x_s, w_out_l, norm_w):
    tm = OUT_PROJ_TM
    n_p = x_p.shape[0] // tm
    n_s = x_s.shape[0] // tm
    assert x_p.shape[0] % tm == 0 and x_s.shape[0] % tm == 0 and ATT_WIDTH + GLA_WIDTH == D_MODEL

    def p_blk(i):
        return (jnp.minimum(i, n_p - 1), 0)

    def s_blk(i):
        return (jnp.maximum(i - n_p, 0), 0)

    row_bytes = ATT_WIDTH * 2 + GLA_WIDTH * 2 + 2 * D_MODEL * 4
    vmem_bytes = (D_MODEL * D_MODEL * 2 + 2 * WO_CHUNK * D_MODEL * 4 + 2 * 2 * tm * row_bytes
                  + 3 * tm * D_MODEL * 4)
    return pl.pallas_call(
        functools.partial(_out_proj_kernel, n_p=n_p),
        grid=(n_p + n_s,),
        in_specs=[
            pl.BlockSpec((tm, ATT_WIDTH), p_blk),
            pl.BlockSpec((tm, GLA_WIDTH), p_blk),
            pl.BlockSpec((tm, D_MODEL), p_blk),
            pl.BlockSpec((tm, ATT_WIDTH), s_blk),
            pl.BlockSpec((tm, GLA_WIDTH), s_blk),
            pl.BlockSpec((tm, D_MODEL), s_blk),
            pl.BlockSpec((1, D_MODEL), lambda i: (0, 0)),
            pl.BlockSpec(memory_space=pl.ANY),
        ],
        out_specs=[
            pl.BlockSpec((tm, D_MODEL), p_blk),
            pl.BlockSpec((tm, D_MODEL), s_blk),
        ],
        out_shape=[
            jax.ShapeDtypeStruct(x_p.shape, F32),
            jax.ShapeDtypeStruct(x_s.shape, F32),
        ],
        scratch_shapes=[
            pltpu.VMEM((D_MODEL, D_MODEL), BF16),
            pltpu.VMEM((2, WO_CHUNK, D_MODEL), F32),
            pltpu.SemaphoreType.DMA((2,)),
        ],
        compiler_params=pltpu.CompilerParams(
            dimension_semantics=("arbitrary",), vmem_limit_bytes=vmem_bytes),
        name="out_proj",
    )(ma_p, mg_p, x_p, ma_s, mg_s, x_s, norm_w, w_out_l)


SWA_SAMPLE_SEQS = 8


def _swa_sample_kernel(rb_ref, sink_ref, q_ref, gate_ref, kvn_ref, ckt_ref, cvt_ref,
                       o_ref, kot_ref, vot_ref, tabc_ref, tabn_ref, sinkcol_ref, place_ref, *, t_new):
    step = pl.program_id(0)
    n_seq = q_ref.shape[0]
    W = ckt_ref.shape[2]
    rows = N_HEADS * t_new

    @pl.when(step == 0)
    def _build_tables():
        r = lax.broadcasted_iota(jnp.int32, (rows, W), 0)
        qi = r & (t_new - 1)
        kj = lax.broadcasted_iota(jnp.int32, (rows, W), 1)
        dist_c = qi + W - kj
        valid_c = (dist_c >= 0) & (dist_c < WINDOW)
        bucket_c = _t5_bucket(dist_c)
        rn = lax.broadcasted_iota(jnp.int32, (rows, t_new), 0)
        qn = rn & (t_new - 1)
        jn = lax.broadcasted_iota(jnp.int32, (rows, t_new), 1)
        dist_n = qn - jn
        valid_n = (dist_n >= 0) & (dist_n < WINDOW)
        bucket_n = _t5_bucket(dist_n)
        head_c = lax.shift_right_logical(r, int(math.log2(t_new)))
        head_n = lax.shift_right_logical(rn, int(math.log2(t_new)))
        head_s = lax.shift_right_logical(lax.broadcasted_iota(jnp.int32, (rows, 1), 0), int(math.log2(t_new)))
        tabc_ref[...] = jnp.full((rows, W), NEG, F32)
        tabn_ref[...] = jnp.full((rows, t_new), NEG, F32)
        sinkcol_ref[...] = jnp.zeros((rows, 1), F32)
        for g in range(GQA_GROUP):
            for h in range(N_KV_HEADS):
                hd = h * GQA_GROUP + g
                blk = g * N_KV_HEADS + h
                sinkcol_ref[...] = jnp.where(head_s == blk, sink_ref[hd], sinkcol_ref[...])

        def body(bk, carry):
            for g in range(GQA_GROUP):
                for h in range(N_KV_HEADS):
                    hd = h * GQA_GROUP + g
                    blk = g * N_KV_HEADS + h
                    val = rb_ref[bk, hd]
                    tabc_ref[...] = jnp.where((bucket_c == bk) & valid_c & (head_c == blk), val, tabc_ref[...])
                    tabn_ref[...] = jnp.where((bucket_n == bk) & valid_n & (head_n == blk), val, tabn_ref[...])
            return carry

        lax.fori_loop(0, N_BUCKETS, body, 0)
        src = lax.broadcasted_iota(jnp.int32, place_ref.shape, 0)
        dst = lax.broadcasted_iota(jnp.int32, place_ref.shape, 1)
        want = lax.shift_right_logical(src, int(math.log2(t_new))) * W + (W - t_new) + (src & (t_new - 1))
        place_ref[...] = (dst == want).astype(F32).astype(BF16)

    lane_head = lax.shift_right_logical(lax.broadcasted_iota(jnp.int32, (1, KV_WIDTH), 1), HEAD_SHIFT)
    scale = HEAD_DIM ** -0.5
    sink = sinkcol_ref[...]
    seqs = range(n_seq)
    scores = []
    for s in seqs:
        q = q_ref[s].astype(F32) * scale
        pieces = []
        for g in range(GQA_GROUP):
            qg = q[:, g * KV_WIDTH:(g + 1) * KV_WIDTH]
            for h in range(N_KV_HEADS):
                pieces.append(jnp.where(lane_head == h, qg, 0.0))
        lhs = jnp.concatenate(pieces, axis=0).astype(BF16)
        sc = jnp.dot(lhs, ckt_ref[s].astype(BF16), preferred_element_type=F32) + tabc_ref[...]
        sn = _nt_dot(lhs, kvn_ref[s, :, :KV_WIDTH].astype(BF16)) + tabn_ref[...]
        scores.append((sc, sn))
    probs = []
    for s in seqs:
        sc, sn = scores[s]
        m = jnp.maximum(jnp.maximum(jnp.max(sc, axis=-1, keepdims=True),
                                    jnp.max(sn, axis=-1, keepdims=True)), sink)
        ec = jnp.exp(sc - m)
        en = jnp.exp(sn - m)
        den = (jnp.sum(ec, axis=-1, keepdims=True) + jnp.sum(en, axis=-1, keepdims=True)
               + jnp.exp(sink - m))
        inv = 1.0 / den
        probs.append(((ec * inv).astype(BF16), (en * inv).astype(BF16)))
    for s in seqs:
        pc, pn = probs[s]
        o = (_nt_dot(pc, cvt_ref[s].astype(BF16))
             + jnp.dot(pn, kvn_ref[s, :, KV_WIDTH:].astype(BF16), preferred_element_type=F32))
        outs = []
        for g in range(GQA_GROUP):
            acc = jnp.zeros((t_new, KV_WIDTH), F32)
            for h in range(N_KV_HEADS):
                blk = g * N_KV_HEADS + h
                acc = jnp.where(lane_head == h, o[blk * t_new:(blk + 1) * t_new, :], acc)
            outs.append(acc)
        att = jnp.concatenate(outs, axis=1)
        gate = gate_ref[s].astype(F32)
        o_ref[s] = (att * (gate * _sigmoid(gate))).astype(BF16)

    new_rows = kvn_ref[...].reshape(n_seq * t_new, 2 * KV_WIDTH)
    hi = new_rows.astype(BF16)
    lo = (new_rows - hi.astype(F32)).astype(BF16)
    place = place_ref[...]
    is_new = lax.broadcasted_iota(jnp.int32, (1, W), 1) >= W - t_new
    for src_ref, dst_ref, cols in ((ckt_ref, kot_ref, slice(0, KV_WIDTH)), (cvt_ref, vot_ref, slice(KV_WIDTH, None))):
        placed = _tn_dot(hi[:, cols], place) + _tn_dot(lo[:, cols], place)
        for s in seqs:
            dst_ref[s] = jnp.where(is_new, placed[:, s * W:(s + 1) * W], pltpu.roll(src_ref[s], W - t_new, axis=1))


def _swa_sample(p3, kv3, cache_k, cache_v, rel_bias, sinks, seq0):
    t_new = p3.shape[1]
    db, _, w = cache_k.shape
    sb = SWA_SAMPLE_SEQS
    assert seq0 % sb == 0
    blk0 = seq0 // sb
    assert t_new & (t_new - 1) == 0 and w >= t_new and w <= WINDOW
    sb = SWA_SAMPLE_SEQS
    rows = N_HEADS * t_new
    return pl.pallas_call(
        functools.partial(_swa_sample_kernel, t_new=t_new),
        grid=(db // sb,),
        in_specs=[
            pl.BlockSpec(memory_space=pltpu.SMEM),
            pl.BlockSpec(memory_space=pltpu.SMEM),
            pl.BlockSpec((sb, t_new, ATT_WIDTH), lambda i: (blk0 + i, 0, COL_QA // ATT_WIDTH)),
            pl.BlockSpec((sb, t_new, ATT_WIDTH), lambda i: (blk0 + i, 0, COL_GATE_A // ATT_WIDTH)),
            pl.BlockSpec((sb, t_new, 2 * KV_WIDTH), lambda i: (blk0 + i, 0, 0)),
            pl.BlockSpec((sb, KV_WIDTH, w), lambda i: (i, 0, 0)),
            pl.BlockSpec((sb, KV_WIDTH, w), lambda i: (i, 0, 0)),
        ],
        out_specs=[
            pl.BlockSpec((sb, t_new, ATT_WIDTH), lambda i: (i, 0, 0)),
            pl.BlockSpec((sb, KV_WIDTH, w), lambda i: (i, 0, 0)),
            pl.BlockSpec((sb, KV_WIDTH, w), lambda i: (i, 0, 0)),
        ],
        out_shape=[
            jax.ShapeDtypeStruct((db, t_new, ATT_WIDTH), BF16),
            jax.ShapeDtypeStruct((db, KV_WIDTH, w), F32),
            jax.ShapeDtypeStruct((db, KV_WIDTH, w), F32),
        ],
        scratch_shapes=[
            pltpu.VMEM((rows, w), F32),
            pltpu.VMEM((rows, t_new), F32),
            pltpu.VMEM((rows, 1), F32),
            pltpu.VMEM((sb * t_new, sb * w), BF16),
        ],
        compiler_params=pltpu.CompilerParams(
            dimension_semantics=("arbitrary",), vmem_limit_bytes=VMEM_LIMIT),
        name="swa_sample",
    )(rel_bias, sinks, p3, p3, kv3, cache_k, cache_v)


GLA_SAMPLE_SEQS = 8


def _cumsum_rows8(g):
    row = lax.broadcasted_iota(jnp.int32, g.shape, 0)
    out = g
    for sh in (1, 2, 4):
        out = out + jnp.where(row >= sh, pltpu.roll(out, sh, axis=0), 0.0)
    return out


def _gla_sample_kernel(qg_ref, kg_ref, vg_ref, gate_ref, lf_ref, gn_ref, st_ref, o_ref, so_ref):
    T = qg_ref.shape[1]
    gn = gn_ref[...]
    t_col = lax.broadcasted_iota(jnp.int32, (T, 1), 0)
    zpad = jnp.zeros((8, GLA_DK), F32)
    seqs = range(qg_ref.shape[0])
    pairs = [(s, h) for s in seqs for h in range(GLA_HEADS)]
    zpad_v = jnp.zeros((8, GLA_DV), F32)

    def ksl(h):
        return slice(h * GLA_DK, (h + 1) * GLA_DK)

    def vsl(h):
        return slice(h * GLA_DV, (h + 1) * GLA_DV)

    qts, kts, b_lasts, vs_, cols = [], [], [], [], {}
    for s in seqs:
        q = qg_ref[s].astype(F32) * (GLA_DK ** -0.5)
        k = kg_ref[s].astype(F32)
        bc = _cumsum_rows8(lf_ref[s])
        b_last = bc[T - 1:T, :]
        b_lasts.append(b_last)
        qts.append(q * jnp.exp(bc))
        kts.append(k * jnp.exp(b_last - bc))
        vs_.append(vg_ref[s].astype(F32))
        for j in range(T):
            bs = jnp.broadcast_to(bc[j:j + 1, :], bc.shape)
            ks_ = jnp.broadcast_to(k[j:j + 1, :], k.shape)
            p = q * jnp.exp(jnp.minimum(bc - bs, 0.0)) * ks_
            for h in range(GLA_HEADS):
                col = jnp.sum(p[:, ksl(h)], axis=-1, keepdims=True)
                cols[s, h, j] = jnp.where(t_col >= j, col, 0.0)

    inters = {}
    for s, h in pairs:
        qt16 = jnp.concatenate([qts[s][:, ksl(h)], zpad], axis=0).astype(BF16)
        inters[s, h] = jnp.dot(qt16, st_ref[s, h].astype(BF16), preferred_element_type=F32)[:T, :]
    for s, h in pairs:
        kt16 = jnp.concatenate([kts[s][:, ksl(h)], zpad], axis=0).astype(BF16)
        v16 = jnp.concatenate([vs_[s][:, vsl(h)], zpad_v], axis=0).astype(BF16)
        so_ref[s, h] = (_decay_column(jnp.exp(b_lasts[s][:, ksl(h)])) * st_ref[s, h]
                        + _tn_dot(kt16, v16))

    for s, h in pairs:
        o = inters[s, h]
        v = vs_[s][:, vsl(h)]
        for j in range(T):
            o = o + cols[s, h, j] * v[j:j + 1, :]
        o_ref[s, :, vsl(h)] = _gla_finish(o, gn, gate_ref[s, :, vsl(h)].astype(F32))


def _gla_sample(p3, lf3, state, gla_norm_row, seq0):
    t_new = p3.shape[1]
    db = state.shape[0]
    sb = GLA_SAMPLE_SEQS
    assert t_new == 8 and seq0 % sb == 0
    blk0 = seq0 // sb
    return pl.pallas_call(
        _gla_sample_kernel,
        grid=(db // sb,),
        in_specs=[
            pl.BlockSpec((sb, t_new, GLA_KEY_WIDTH), lambda i: (blk0 + i, 0, COL_QG // GLA_KEY_WIDTH)),
            pl.BlockSpec((sb, t_new, GLA_KEY_WIDTH), lambda i: (blk0 + i, 0, COL_KG // GLA_KEY_WIDTH)),
            pl.BlockSpec((sb, t_new, GLA_WIDTH), lambda i: (blk0 + i, 0, COL_VG // GLA_WIDTH)),
            pl.BlockSpec((sb, t_new, GLA_WIDTH), lambda i: (blk0 + i, 0, COL_GATE_G // GLA_WIDTH)),
            pl.BlockSpec((sb, t_new, GLA_KEY_WIDTH), lambda i: (blk0 + i, 0, 0)),
            pl.BlockSpec((1, GLA_DV), lambda i: (0, 0)),
            pl.BlockSpec((sb, GLA_HEADS, GLA_DK, GLA_DV), lambda i: (i, 0, 0, 0)),
        ],
        out_specs=[
            pl.BlockSpec((sb, t_new, GLA_WIDTH), lambda i: (i, 0, 0)),
            pl.BlockSpec((sb, GLA_HEADS, GLA_DK, GLA_DV), lambda i: (i, 0, 0, 0)),
        ],
        out_shape=[
            jax.ShapeDtypeStruct((db, t_new, GLA_WIDTH), BF16),
            jax.ShapeDtypeStruct((db, GLA_HEADS, GLA_DK, GLA_DV), F32),
        ],
        compiler_params=pltpu.CompilerParams(
            dimension_semantics=("arbitrary",), vmem_limit_bytes=VMEM_LIMIT),
        name="gla_sample",
    )(p3, p3, p3, p3, lf3, gla_norm_row, state)


def _window_feature_major(win):
    db, w = win.shape[:2]
    return win.transpose(0, 2, 3, 1).reshape(db, KV_WIDTH, w)


def _window_from_feature_major(win_t):
    db, _, w = win_t.shape
    return win_t.reshape(db, N_KV_HEADS, HEAD_DIM, w).transpose(0, 3, 1, 2)


def kernel(x_prompt, x_sample, cache_k_win, cache_v_win, state_gla, meta_tokens, rel_bias,
           norm_pre, norm_post, w_in, w_a2, b_a, attn_sinks, gla_norm, w_out):
    n_batch, seq, _ = x_prompt.shape
    db, t_new, _ = x_sample.shape
    depth = w_in.shape[0]
    assert depth == 1
    l = 0
    lp = seq + WINDOW
    n_blocks = lp // WINDOW
    n_chunks = lp // GLA_CHUNK
    w_cache = cache_k_win.shape[2]

    w_in_t = w_in[l].T
    assert sum(IN_SIZES[:-1]) == N_MAIN
    w_a2p = jnp.pad(w_a2[l], ((0, LANES - GLA_RANK), (0, 0))).astype(BF16)
    npre = norm_pre[l][None, :]
    npost = norm_post[l][None, :]
    ba = b_a[l][None, :]
    gn = gla_norm[l][None, :]
    sinks = attn_sinks[l]

    tm = IN_PROJ_TM
    x_meta = jnp.concatenate([jnp.zeros((tm - N_META, D_MODEL), x_prompt.dtype),
                              meta_tokens.astype(x_prompt.dtype)], axis=0)
    p_all, kv_all, lf_all = _in_proj(
        x_meta, x_prompt.reshape(n_batch * seq, D_MODEL), x_sample.reshape(db * t_new, D_MODEL),
        w_in_t, npre, w_a2p, ba)
    meta_blk = tm // WINDOW - 1
    sample_seq0 = (tm + n_batch * seq) // t_new

    mixed_a = _swa_prompt(p_all, rel_bias, sinks, n_batch, n_blocks, meta_blk)
    mixed_g, s_fin = _gla_prompt(p_all, lf_all, gn, n_batch, n_chunks, meta_blk)
    kv_p = jnp.stack([kv_all[tm + (b + 1) * seq - WINDOW:tm + (b + 1) * seq] for b in range(n_batch)])
    k_win_p = kv_p[..., :KV_WIDTH].reshape(1, n_batch, WINDOW, N_KV_HEADS, HEAD_DIM)
    v_win_p = kv_p[..., KV_WIDTH:].reshape(1, n_batch, WINDOW, N_KV_HEADS, HEAD_DIM)
    gla_p = s_fin[None]

    ps3 = p_all.reshape(-1, t_new, N_MAIN)
    mixed_as, k_win_s, v_win_s = _swa_sample(
        ps3, kv_all.reshape(-1, t_new, 2 * KV_WIDTH),
        _window_feature_major(cache_k_win[l]), _window_feature_major(cache_v_win[l]),
        rel_bias, sinks, sample_seq0)
    mixed_gs, s_new = _gla_sample(ps3, lf_all.reshape(-1, t_new, GLA_KEY_WIDTH), state_gla[l], gn, sample_seq0)

    y_prompt, y_sample = _out_proj(
        mixed_a, mixed_g.reshape(n_batch * seq, GLA_WIDTH), x_prompt.reshape(n_batch * seq, D_MODEL),
        mixed_as.reshape(db * t_new, ATT_WIDTH), mixed_gs.reshape(db * t_new, GLA_WIDTH),
        x_sample.reshape(db * t_new, D_MODEL), w_out[l], npost)
    y_prompt = y_prompt.reshape(n_batch, seq, D_MODEL)
    y_sample = y_sample.reshape(db, t_new, D_MODEL)
    k_win_s = _window_from_feature_major(k_win_s)[None]
    v_win_s = _window_from_feature_major(v_win_s)[None]

    return (y_prompt, y_sample, k_win_p, v_win_p, gla_p, k_win_s, v_win_s, s_new[None])
```

```python
import functools
import math

import jax
import jax.numpy as jnp
from jax import lax
from jax.experimental import pallas as pl
from jax.experimental.pallas import tpu as pltpu

F32 = jnp.float32
BF16 = jnp.bfloat16

D_MODEL = 2048
N_META = 16
ATT_WIDTH = 1024
HEAD_DIM = 64
N_HEADS = 16
N_KV_HEADS = 4
GQA_GROUP = 4
KV_WIDTH = 256
WINDOW = 128
N_BUCKETS = 32
MAX_DISTANCE = 128
GLA_WIDTH = 1024
GLA_HEADS = 4
GLA_DV = 256
GLA_DK = 128
GLA_KEY_WIDTH = 512
GLA_RANK = 16
GLA_NORMALIZER = 16.0
GLA_CHUNK = 64
FRONT_PAD = WINDOW - N_META
EPS = 1e-6
PAST_LEN = 8192

IN_SIZES = (ATT_WIDTH, KV_WIDTH, KV_WIDTH, ATT_WIDTH,
            GLA_KEY_WIDTH, GLA_KEY_WIDTH, GLA_WIDTH, GLA_WIDTH, GLA_RANK)

N_MAIN = 5632
TN = 512
COL_QA = 0
COL_GATE_A = 1024
COL_QG = 2048
COL_KG = 2560
COL_VG = 3072
COL_GATE_G = 4096
COL_KA = 5120
COL_VA = 5376
SILU_COLS = ((COL_GATE_A, ATT_WIDTH), (COL_GATE_G, GLA_WIDTH))
LANES = 128
NEG = -1e30
LOG2E = math.log2(math.e)

VMEM_LIMIT = 48 * 1024 * 1024


def _sigmoid(x):
    return 1.0 / (1.0 + jnp.exp(-x))


def _log_sigmoid(x):
    return jnp.minimum(x, 0.0) - jnp.log1p(jnp.exp(-jnp.abs(x)))


def _nt_dot(a, b):
    return lax.dot_general(a, b, (((1,), (1,)), ((), ())), preferred_element_type=F32)


def _tn_dot(a, b):
    return lax.dot_general(a, b, (((0,), (0,)), ((), ())), preferred_element_type=F32)


def _group_major_pieces(src0, dst0, n_rows):
    pieces = []
    for r in range(dst0, dst0 + n_rows, HEAD_DIM):
        g, h = divmod(r // HEAD_DIM, N_KV_HEADS)
        pieces.append((src0 + (h * GQA_GROUP + g) * HEAD_DIM, r - dst0, HEAD_DIM))
    return pieces


def _load_rows_as_bf16(src_hbm, chunk_pieces, stage_ref, sem, dst_ref, chunk_rows, after_chunk=None):
    def copies(c):
        return [pltpu.make_async_copy(src_hbm.at[pl.ds(src, n), :], stage_ref.at[c % 2, pl.ds(dst, n), :],
                                      sem.at[c % 2]) for src, dst, n in chunk_pieces[c]]

    for cp in copies(0):
        cp.start()
    for c in range(len(chunk_pieces)):
        if c + 1 < len(chunk_pieces):
            for cp in copies(c + 1):
                cp.start()
        for cp in copies(c):
            cp.wait()
        dst_ref[c * chunk_rows:(c + 1) * chunk_rows, :] = stage_ref[c % 2].astype(BF16)
        if after_chunk is not None:
            after_chunk(c)


W_CHUNK = 256
IN_PROJ_TM = 256
GATE_HEAD_AFTER_TILES = 3


def _in_proj_weight_pieces():
    src_off = dict(zip(("q_a", "k_a", "v_a", "gate_a", "q_g", "k_g", "v_g", "gate_g"),
                       (sum(IN_SIZES[:i]) for i in range(8))))
    chunks = []
    for name in ("q_a", "gate_a"):
        for d in range(0, ATT_WIDTH, W_CHUNK):
            chunks.append(_group_major_pieces(src_off[name], d, W_CHUNK))
    for name, width in (("q_g", GLA_KEY_WIDTH), ("k_g", GLA_KEY_WIDTH), ("v_g", GLA_WIDTH),
                        ("gate_g", GLA_WIDTH), ("k_a", KV_WIDTH), ("v_a", KV_WIDTH)):
        for d in range(0, width, W_CHUNK):
            chunks.append([(src_off[name] + d, 0, W_CHUNK)])
    return chunks


def _in_proj_kernel(xm_hbm, xp_hbm, xs_hbm, wt_hbm, nw_ref, wa2_ref, ba_ref,
                    p_ref, kv_ref, lf_ref,
                    wbf_ref, wz_ref, stage_ref, zstage_ref, xbuf_ref, hn0_ref, hn1_ref, lf0_ref, lf1_ref,
                    wsem, zsem, xsem, *, n_p, n_s):
    i = pl.program_id(0)
    tm = xbuf_ref.shape[1]
    n_steps = 1 + n_p + n_s

    def x_copy(step, slot, fn):
        @pl.when(step == 0)
        def _():
            fn(pltpu.make_async_copy(xm_hbm, xbuf_ref.at[slot], xsem.at[slot]))

        @pl.when((step >= 1) & (step <= n_p))
        def _():
            fn(pltpu.make_async_copy(xp_hbm.at[pl.ds((step - 1) * tm, tm), :], xbuf_ref.at[slot], xsem.at[slot]))

        @pl.when(step > n_p)
        def _():
            fn(pltpu.make_async_copy(xs_hbm.at[pl.ds((step - 1 - n_p) * tm, tm), :], xbuf_ref.at[slot],
                                     xsem.at[slot]))

    slot = i % 2
    nxt = 1 - slot

    def rmsnorm_to(x_slot, hn_out):
        x = xbuf_ref[x_slot]
        var = jnp.mean(x * x, axis=-1, keepdims=True)
        hn_out[...] = (x * lax.rsqrt(var + EPS) * nw_ref[...]).astype(BF16)

    def gate_head_to(hn_in, lf_out):
        z = _nt_dot(hn_in[...], wz_ref[...])
        logits = jnp.dot(z.astype(BF16), wa2_ref[...], preferred_element_type=F32) + ba_ref[...]
        lf_out[...] = _log_sigmoid(logits) * (1.0 / GLA_NORMALIZER)

    def project_tile(hn, j):
        acc = _nt_dot(hn, wbf_ref[j * TN:(j + 1) * TN, :])
        if j * TN == COL_KA:
            kv_ref[...] = acc
        if any(c <= j * TN < c + w for c, w in SILU_COLS):
            acc = acc * _sigmoid(acc)
        if COL_QG <= j * TN < COL_QG + GLA_KEY_WIDTH:
            acc = acc * (GLA_DK ** -0.5)
        p_ref[:, j * TN:(j + 1) * TN] = acc.astype(BF16)

    @pl.when(i == 0)
    def _first_step():
        x_copy(0, 0, lambda cp: cp.start())
        x_copy(1, 1, lambda cp: cp.start())

        z_copy = pltpu.make_async_copy(wt_hbm.at[pl.ds(N_MAIN, GLA_RANK), :], zstage_ref, zsem.at[0])
        z_copy.start()
        wz_ref[...] = jnp.zeros(wz_ref.shape, BF16)
        z_copy.wait()
        wz_ref[:GLA_RANK, :] = zstage_ref[...].astype(BF16)
        x_copy(0, 0, lambda cp: cp.wait())
        rmsnorm_to(0, hn0_ref)
        gate_head_to(hn0_ref, lf0_ref)
        lf_ref[...] = lf0_ref[...]
        hn = hn0_ref[...]
        chunks_per_tile = TN // W_CHUNK

        def project_ready_tile(c):
            if (c + 1) % chunks_per_tile == 0:
                project_tile(hn, c // chunks_per_tile)

        _load_rows_as_bf16(wt_hbm, _in_proj_weight_pieces(), stage_ref, wsem, wbf_ref, W_CHUNK, project_ready_tile)

    @pl.when(i + 2 < n_steps)
    def _prefetch():
        x_copy(i + 2, slot, lambda cp: cp.start())

    @pl.when(i + 1 < n_steps)
    def _():
        x_copy(i + 1, nxt, lambda cp: cp.wait())

    def project(hn_cur, lf_cur, hn_nxt, lf_nxt):
        lf_ref[...] = lf_cur[...]
        hn = hn_cur[...]
        rmsnorm_to(nxt, hn_nxt)
        for j in range(N_MAIN // TN):
            if j == GATE_HEAD_AFTER_TILES:
                gate_head_to(hn_nxt, lf_nxt)
            project_tile(hn, j)

    @pl.when(i == 0)
    def _():
        rmsnorm_to(1, hn1_ref)
        gate_head_to(hn1_ref, lf1_ref)

    @pl.when((slot == 0) & (i > 0))
    def _():
        project(hn0_ref, lf0_ref, hn1_ref, lf1_ref)

    @pl.when(slot == 1)
    def _():
        project(hn1_ref, lf1_ref, hn0_ref, lf0_ref)


def _in_proj(x_meta, x_prompt, x_sample, w_in_t, norm_w, w_a2p, b_a):
    tm = IN_PROJ_TM
    n_p = x_prompt.shape[0] // tm
    n_s = x_sample.shape[0] // tm
    assert x_meta.shape[0] == tm and x_prompt.shape[0] % tm == 0 and x_sample.shape[0] % tm == 0
    assert COL_KA % TN == 0 and len(_in_proj_weight_pieces()) * W_CHUNK == N_MAIN
    m_all = (1 + n_p + n_s) * tm

    def const(i):
        return (0, 0)

    any_spec = pl.BlockSpec(memory_space=pl.ANY)
    buffers = (D_MODEL * N_MAIN * 2 + 2 * D_MODEL * W_CHUNK * 4 + 2 * tm * D_MODEL * 4
               + 2 * tm * N_MAIN * 2 + 2 * 2 * tm * GLA_KEY_WIDTH * 4
               + D_MODEL * LANES * 2 + D_MODEL * GLA_RANK * 4 + 2 * LANES * GLA_KEY_WIDTH * 2)
    buffers += 2 * tm * D_MODEL * 2 + 2 * tm * GLA_KEY_WIDTH * 4
    temps = tm * D_MODEL * (4 + 4) + 2 * tm * TN * 4 + 2 * tm * GLA_KEY_WIDTH * 4
    vmem_bytes = buffers + temps
    return pl.pallas_call(
        functools.partial(_in_proj_kernel, n_p=n_p, n_s=n_s),
        grid=(1 + n_p + n_s,),
        in_specs=[
            any_spec, any_spec, any_spec, any_spec,
            pl.BlockSpec((1, D_MODEL), const),
            pl.BlockSpec((LANES, GLA_KEY_WIDTH), const),
            pl.BlockSpec((1, GLA_KEY_WIDTH), const),
        ],
        out_specs=[
            pl.BlockSpec((tm, N_MAIN), lambda i: (i, 0)),
            pl.BlockSpec((tm, 2 * KV_WIDTH), lambda i: (i, 0)),
            pl.BlockSpec((tm, GLA_KEY_WIDTH), lambda i: (i, 0)),
        ],
        out_shape=[
            jax.ShapeDtypeStruct((m_all, N_MAIN), BF16),
            jax.ShapeDtypeStruct((m_all, 2 * KV_WIDTH), F32),
            jax.ShapeDtypeStruct((m_all, GLA_KEY_WIDTH), F32),
        ],
        scratch_shapes=[
            pltpu.VMEM((N_MAIN, D_MODEL), BF16),
            pltpu.VMEM((LANES, D_MODEL), BF16),
            pltpu.VMEM((2, W_CHUNK, D_MODEL), F32),
            pltpu.VMEM((GLA_RANK, D_MODEL), F32),
            pltpu.VMEM((2, tm, D_MODEL), F32),
            pltpu.VMEM((tm, D_MODEL), BF16),
            pltpu.VMEM((tm, D_MODEL), BF16),
            pltpu.VMEM((tm, GLA_KEY_WIDTH), F32),
            pltpu.VMEM((tm, GLA_KEY_WIDTH), F32),
            pltpu.SemaphoreType.DMA((2,)),
            pltpu.SemaphoreType.DMA((1,)),
            pltpu.SemaphoreType.DMA((2,)),
        ],
        compiler_params=pltpu.CompilerParams(
            dimension_semantics=("arbitrary",), vmem_limit_bytes=vmem_bytes),
        name="in_proj",
    )(x_meta, x_prompt, x_sample, w_in_t, norm_w, w_a2p, b_a)


def _t5_bucket(dist):
    max_exact = N_BUCKETS // 2
    d = jnp.maximum(dist, 0)
    ratio = jnp.maximum(d, max_exact).astype(F32) / max_exact
    val = jnp.log(ratio) / math.log(MAX_DISTANCE / max_exact) * (N_BUCKETS - max_exact)
    large = jnp.full(dist.shape, max_exact, jnp.int32)
    for step in range(1, N_BUCKETS - max_exact):
        large = large + (val >= step).astype(jnp.int32)
    return jnp.where(d < max_exact, d, large)


SWA_BLOCKS_PER_STEP = 2
HEAD_SHIFT = HEAD_DIM.bit_length() - 1


def _swa_prompt_kernel(rb_ref, sink_ref, qg_ref, kvp_ref, kvc_ref, o_ref, tab_ref):
    b = pl.program_id(0)
    m = pl.program_id(1)
    W2 = 2 * WINDOW
    NL = N_KV_HEADS * W2
    n_sub = qg_ref.shape[0] // WINDOW

    @pl.when((b == 0) & (m == 0))
    def _build_bias_tables():
        qi = lax.broadcasted_iota(jnp.int32, (WINDOW, W2), 0)
        kj = lax.broadcasted_iota(jnp.int32, (WINDOW, W2), 1)
        dist = qi - kj + WINDOW
        valid = (dist >= 0) & (dist < WINDOW)
        bucket = _t5_bucket(dist)
        for g in range(GQA_GROUP):
            tab_ref[1, g] = jnp.full((WINDOW, NL), NEG, F32)

        def body(bk, carry):
            hit = (bucket == bk) & valid
            for g in range(GQA_GROUP):
                for h in range(N_KV_HEADS):
                    cur = tab_ref[1, g, :, h * W2:(h + 1) * W2]
                    tab_ref[1, g, :, h * W2:(h + 1) * W2] = jnp.where(
                        hit, rb_ref[bk, h * GQA_GROUP + g] * LOG2E, cur)
            return carry

        lax.fori_loop(0, N_BUCKETS, body, 0)
        kvalid = kj - FRONT_PAD >= 0
        for g in range(GQA_GROUP):
            for h in range(N_KV_HEADS):
                tab_ref[0, g, :, h * W2:(h + 1) * W2] = jnp.where(
                    kvalid, tab_ref[1, g, :, h * W2:(h + 1) * W2], NEG)

    lane_head = lax.shift_right_logical(lax.broadcasted_iota(jnp.int32, (1, KV_WIDTH), 1), HEAD_SHIFT)
    zero = jnp.zeros((), BF16)

    def per_head(x):
        return [jnp.where(lane_head == h, x, zero) for h in range(N_KV_HEADS)]

    def rows(r):
        return slice(r * WINDOW, (r + 1) * WINDOW)

    km = [per_head(kvp_ref[:, :KV_WIDTH])] + [per_head(kvc_ref[rows(r), :KV_WIDTH]) for r in range(n_sub)]
    vm = [per_head(kvp_ref[:, KV_WIDTH:])] + [per_head(kvc_ref[rows(r), KV_WIDTH:]) for r in range(n_sub)]

    def stacked(parts, r):
        return jnp.concatenate([blk for h in range(N_KV_HEADS) for blk in (parts[r][h], parts[r + 1][h])], axis=0)

    scale = HEAD_DIM ** -0.5 * LOG2E
    scores = {}
    for r in range(n_sub):
        kkm = stacked(km, r)
        tv = jnp.where(m == 0, 0, 1) if r == 0 else 1
        for g in range(GQA_GROUP):
            qg = (qg_ref[rows(r), g * KV_WIDTH:(g + 1) * KV_WIDTH].astype(F32) * scale).astype(BF16)
            scores[r, g] = _nt_dot(qg, kkm) + tab_ref[tv, g]
    heads = [(g, h) for g in range(GQA_GROUP) for h in range(N_KV_HEADS)]
    sinks = {(g, h): sink_ref[h * GQA_GROUP + g] * LOG2E for g, h in heads}
    for r in range(n_sub):
        vvm = stacked(vm, r)
        sl = {(g, h): scores[r, g][:, h * W2:(h + 1) * W2] for g, h in heads}
        mx = {gh: jnp.maximum(jnp.max(sl[gh], axis=-1, keepdims=True), sinks[gh]) for gh in heads}
        ex = {gh: jnp.exp2(sl[gh] - mx[gh]) for gh in heads}
        den = {gh: jnp.sum(ex[gh], axis=-1, keepdims=True) + jnp.exp2(sinks[gh] - mx[gh]) for gh in heads}
        for g in range(GQA_GROUP):
            p = jnp.concatenate([ex[g, h].astype(BF16) for h in range(N_KV_HEADS)], axis=1)
            o = jnp.dot(p, vvm, preferred_element_type=F32)
            inv = jnp.zeros((WINDOW, KV_WIDTH), F32)
            for h in range(N_KV_HEADS):
                inv = jnp.where(lane_head == h, 1.0 / den[g, h], inv)
            gate = qg_ref[rows(r), ATT_WIDTH + g * KV_WIDTH:ATT_WIDTH + (g + 1) * KV_WIDTH].astype(F32)
            o_ref[rows(r), g * KV_WIDTH:(g + 1) * KV_WIDTH] = (o * inv * gate).astype(BF16)


def _swa_prompt(p_all, rel_bias, sinks, n_batch, n_blocks, meta_blk):
    nb = SWA_BLOCKS_PER_STEP
    n_real = n_blocks - 1
    rows = nb * WINDOW
    assert n_real % nb == 0 and (meta_blk + 1) % nb == 0
    assert COL_GATE_A == COL_QA + ATT_WIDTH and COL_VA == COL_KA + KV_WIDTH and COL_QA == 0
    steps = n_real // nb
    first = (meta_blk + 1) // nb

    def cur(b, m):
        return first + b * steps + m

    def prev(b, m):
        return jnp.where(m == 0, meta_blk, meta_blk + b * n_real + nb * m)

    return pl.pallas_call(
        _swa_prompt_kernel,
        grid=(n_batch, steps),
        in_specs=[
            pl.BlockSpec(memory_space=pltpu.SMEM),
            pl.BlockSpec(memory_space=pltpu.SMEM),
            pl.BlockSpec((rows, 2 * ATT_WIDTH), lambda b, m: (cur(b, m), 0)),
            pl.BlockSpec((WINDOW, 2 * KV_WIDTH), lambda b, m: (prev(b, m), COL_KA // (2 * KV_WIDTH))),
            pl.BlockSpec((rows, 2 * KV_WIDTH), lambda b, m: (cur(b, m), COL_KA // (2 * KV_WIDTH))),
        ],
        out_specs=pl.BlockSpec((rows, ATT_WIDTH), lambda b, m: (b * steps + m, 0)),
        out_shape=jax.ShapeDtypeStruct((n_batch * n_real * WINDOW, ATT_WIDTH), BF16),
        scratch_shapes=[pltpu.VMEM((2, GQA_GROUP, WINDOW, N_KV_HEADS * 2 * WINDOW), F32)],
        compiler_params=pltpu.CompilerParams(
            dimension_semantics=("arbitrary", "arbitrary"), vmem_limit_bytes=VMEM_LIMIT),
        name="swa_prompt",
    )(rel_bias, sinks, p_all, p_all, p_all)


def _diag_scores(q3, k3, b3, width):
    nb = q3.shape[0]
    n_heads = q3.shape[2] // GLA_DK
    r_io = lax.broadcasted_iota(jnp.int32, (nb, 8, width), 0)
    t_io = lax.broadcasted_iota(jnp.int32, (nb, 8, width), 1)
    l_io = lax.broadcasted_iota(jnp.int32, (nb, 8, width), 2)
    s_of_lane = l_io - 8 * r_io
    outs = [jnp.zeros((nb, 8, width), F32) for _ in range(n_heads)]
    for s in range(8):
        bs = jnp.broadcast_to(b3[:, s:s + 1, :], b3.shape)
        ks = jnp.broadcast_to(k3[:, s:s + 1, :], k3.shape)
        e = jnp.exp(jnp.minimum(b3 - bs, 0.0))
        p = q3 * e * ks
        for h in range(n_heads):
            col = jnp.sum(p[:, :, h * GLA_DK:(h + 1) * GLA_DK], axis=-1, keepdims=True)
            outs[h] = jnp.where(s_of_lane == s, col, outs[h])
    causal = (s_of_lane >= 0) & (s_of_lane <= t_io) & (s_of_lane < 8)
    return [jnp.where(causal, o, 0.0) for o in outs]


def _gla_finish(o, gn, gate):
    var = jnp.mean(o * o, axis=-1, keepdims=True)
    on = o * lax.rsqrt(var + EPS) * gn
    return (on * gate).astype(BF16)


def _decay_column(row):
    eye = (lax.broadcasted_iota(jnp.int32, (GLA_DK, GLA_DK), 0)
           == lax.broadcasted_iota(jnp.int32, (GLA_DK, GLA_DK), 1))
    return jnp.sum(jnp.where(eye, jnp.broadcast_to(row, (GLA_DK, GLA_DK)), 0.0), axis=1, keepdims=True)


def _gla_prompt_kernel(*refs):
    o_ref, s_ref = refs[-2:]
    gn_ref = refs[-3]
    n_batch = (len(refs) - 3) // 5
    qg_refs, kg_refs, vg_refs, gate_refs, lf_refs = (refs[j:5 * n_batch:5] for j in range(5))
    c = pl.program_id(0)
    C = GLA_CHUNK

    @pl.when(c == 0)
    def _():
        s_ref[...] = jnp.zeros(s_ref.shape, F32)

    row = lax.broadcasted_iota(jnp.int32, (C, 1), 0)
    valid = (c * C + row) >= FRONT_PAD
    tril = (lax.broadcasted_iota(jnp.int32, (C, C), 0)
            >= lax.broadcasted_iota(jnp.int32, (C, C), 1)).astype(F32).astype(BF16)
    t_io = lax.broadcasted_iota(jnp.int32, (C, C), 0)
    s_io = lax.broadcasted_iota(jnp.int32, (C, C), 1)
    n_levels = int(math.log2(C))
    level_masks = [(t_io > s_io) & (lax.shift_right_logical(t_io ^ s_io, sh) == 1) for sh in range(n_levels)]
    sub_io = lax.broadcasted_iota(jnp.int32, (C // 8, 8, 1), 1)
    gn = gn_ref[...]

    heads = [(b, h) for b in range(n_batch) for h in range(GLA_HEADS)]

    def ksl(h):
        return slice(h * GLA_DK, (h + 1) * GLA_DK)

    def vsl(h):
        return slice(h * GLA_DV, (h + 1) * GLA_DV)

    qs, kks, bcs = [], [], []
    for b in range(n_batch):
        qs.append(qg_refs[b][...].astype(F32))
        kks.append(jnp.where(valid, kg_refs[b][...].astype(F32), 0.0))
        g = jnp.where(valid, lf_refs[b][...], 0.0)
        g1 = g.astype(BF16)
        r1 = g - g1.astype(F32)
        g2 = r1.astype(BF16)
        g3 = (r1 - g2.astype(F32)).astype(BF16)
        bcs.append(jnp.dot(tril, g1, preferred_element_type=F32)
                   + jnp.dot(tril, g2, preferred_element_type=F32)
                   + jnp.dot(tril, g3, preferred_element_type=F32))

    qts, kts, yss, qks, b_lasts = [], [], [], [], []
    for b in range(n_batch):
        q, k, bc = qs[b], kks[b], bcs[b]
        b_last = bc[C - 1:C, :]
        b_lasts.append(b_last)
        qts.append((q * jnp.exp(bc)).astype(BF16))
        kts.append((k * jnp.exp(b_last - bc)).astype(BF16))
        b3 = bc.reshape(C // 8, 8, -1)
        ys = []
        for sh in range(n_levels):
            half = 1 << sh
            if half >= 8:
                ref = jnp.concatenate(
                    [jnp.broadcast_to(bc[i + half - 1:i + half, :], (2 * half, bc.shape[1]))
                     for i in range(0, C, 2 * half)], axis=0)
            elif half == 4:
                ref = jnp.broadcast_to(b3[:, 3:4, :], b3.shape).reshape(bc.shape)
            elif half == 2:
                ref = jnp.where(sub_io >= 4, jnp.broadcast_to(b3[:, 5:6, :], b3.shape),
                                jnp.broadcast_to(b3[:, 1:2, :], b3.shape)).reshape(bc.shape)
            else:
                ref = jnp.where((row & 1) == 1, pltpu.roll(bc, 1, axis=0), bc)
            upper = (lax.shift_right_logical(row, sh) & 1) == 1
            ys.append((jnp.where(upper, q, k) * jnp.exp(-jnp.abs(bc - ref))).astype(BF16))
        yss.append(ys)
        qks.append(q * k)

    level_prods, inters = {}, {}
    for b, h in heads:
        level_prods[b, h] = [_nt_dot(yss[b][sh][:, ksl(h)], yss[b][sh][:, ksl(h)]) for sh in range(n_levels)]
    for b, h in heads:
        st = s_ref[b, h]
        inters[b, h] = jnp.dot(qts[b][:, ksl(h)], st.astype(BF16), preferred_element_type=F32)
        s_ref[b, h] = (_decay_column(jnp.exp(b_lasts[b][:, ksl(h)])) * st
                       + _tn_dot(kts[b][:, ksl(h)], vg_refs[b][:, vsl(h)]))

    for b, h in heads:
        a = jnp.where(t_io == s_io, jnp.sum(qks[b][:, ksl(h)], axis=-1, keepdims=True), 0.0)
        for sh in range(n_levels):
            a = jnp.where(level_masks[sh], level_prods[b, h][sh], a)
        o = jnp.dot(a.astype(BF16), vg_refs[b][:, vsl(h)], preferred_element_type=F32) + inters[b, h]
        o_ref[b, :, vsl(h)] = _gla_finish(o, gn, gate_refs[b][:, vsl(h)].astype(F32))


def _gla_prompt(p_all, lf_all, gla_norm_row, n_batch, n_chunks, meta_blk):
    skip = (FRONT_PAD + N_META) // GLA_CHUNK
    out_len = (n_chunks - skip) * GLA_CHUNK
    C = GLA_CHUNK
    per_blk = WINDOW // C

    def chunk(b):
        def f(c):
            return jnp.where(c < skip, meta_blk * per_blk + c, (meta_blk + 1) * per_blk + b * (n_chunks - skip) + c - skip)
        return f

    in_specs = []
    args = []
    for b in range(n_batch):
        f = chunk(b)
        in_specs += [
            pl.BlockSpec((C, GLA_KEY_WIDTH), lambda c, f=f: (f(c), COL_QG // GLA_KEY_WIDTH)),
            pl.BlockSpec((C, GLA_KEY_WIDTH), lambda c, f=f: (f(c), COL_KG // GLA_KEY_WIDTH)),
            pl.BlockSpec((C, GLA_WIDTH), lambda c, f=f: (f(c), COL_VG // GLA_WIDTH)),
            pl.BlockSpec((C, GLA_WIDTH), lambda c, f=f: (f(c), COL_GATE_G // GLA_WIDTH)),
            pl.BlockSpec((C, GLA_KEY_WIDTH), lambda c, f=f: (f(c), 0)),
        ]
        args += [p_all, p_all, p_all, p_all, lf_all]
    return pl.pallas_call(
        _gla_prompt_kernel,
        grid=(n_chunks,),
        in_specs=in_specs + [pl.BlockSpec((1, GLA_DV), lambda c: (0, 0))],
        out_specs=[
            pl.BlockSpec((n_batch, C, GLA_WIDTH), lambda c: (0, jnp.maximum(c - skip, 0), 0)),
            pl.BlockSpec((n_batch, GLA_HEADS, GLA_DK, GLA_DV), lambda c: (0, 0, 0, 0)),
        ],
        out_shape=[
            jax.ShapeDtypeStruct((n_batch, out_len, GLA_WIDTH), BF16),
            jax.ShapeDtypeStruct((n_batch, GLA_HEADS, GLA_DK, GLA_DV), F32),
        ],
        compiler_params=pltpu.CompilerParams(
            dimension_semantics=("arbitrary",), vmem_limit_bytes=VMEM_LIMIT),
        name="gla_prompt",
    )(*args, gla_norm_row)


OUT_PROJ_TM = 256
WO_CHUNK = 512


def _out_proj_kernel(map_ref, mgp_ref, xp_ref, mas_ref, mgs_ref, xs_ref, nw_ref, w_hbm,
                     yp_ref, ys_ref, wbf_ref, stage_ref, wsem, *, n_p):
    i = pl.program_id(0)

    def normalise_add(o, x_ref, y_ref):
        var = jnp.mean(o * o, axis=-1, keepdims=True)
        y_ref[...] = x_ref[...] + o * lax.rsqrt(var + EPS) * nw_ref[...]

    @pl.when(i == 0)
    def _first_step():
        chunks = [_group_major_pieces(0, d, WO_CHUNK) for d in range(0, ATT_WIDTH, WO_CHUNK)]
        chunks += [[(d, 0, WO_CHUNK)] for d in range(ATT_WIDTH, D_MODEL, WO_CHUNK)]
        partial = []

        def contract_chunk(c):
            r0 = c * WO_CHUNK
            src, off = (map_ref, r0) if r0 < ATT_WIDTH else (mgp_ref, r0 - ATT_WIDTH)
            part = jnp.dot(src[:, off:off + WO_CHUNK], wbf_ref[r0:r0 + WO_CHUNK, :], preferred_element_type=F32)
            partial[:] = [part if not partial else partial[0] + part]

        _load_rows_as_bf16(w_hbm, chunks, stage_ref, wsem, wbf_ref, WO_CHUNK, contract_chunk)
        normalise_add(partial[0], xp_ref, yp_ref)

    def finish(ma_ref, mg_ref, x_ref, y_ref):
        o = (jnp.dot(ma_ref[...], wbf_ref[:ATT_WIDTH, :], preferred_element_type=F32)
             + jnp.dot(mg_ref[...], wbf_ref[ATT_WIDTH:, :], preferred_element_type=F32))
        normalise_add(o, x_ref, y_ref)

    @pl.when((i > 0) & (i < n_p))
    def _():
        finish(map_ref, mgp_ref, xp_ref, yp_ref)

    @pl.when(i >= n_p)
    def _():
        finish(mas_ref, mgs_ref, xs_ref, ys_ref)


def _out_proj(ma_p, mg_p, x_p, ma_s, mg_s, x_s, w_out_l, norm_w):
    tm = OUT_PROJ_TM
    n_p = x_p.shape[0] // tm
    n_s = x_s.shape[0] // tm
    assert x_p.shape[0] % tm == 0 and x_s.shape[0] % tm == 0 and ATT_WIDTH + GLA_WIDTH == D_MODEL

    def p_blk(i):
        return (jnp.minimum(i, n_p - 1), 0)

    def s_blk(i):
        return (jnp.maximum(i - n_p, 0), 0)

    row_bytes = ATT_WIDTH * 2 + GLA_WIDTH * 2 + 2 * D_MODEL * 4
    vmem_bytes = (D_MODEL * D_MODEL * 2 + 2 * WO_CHUNK * D_MODEL * 4 + 2 * 2 * tm * row_bytes
                  + 3 * tm * D_MODEL * 4)
    return pl.pallas_call(
        functools.partial(_out_proj_kernel, n_p=n_p),
        grid=(n_p + n_s,),
        in_specs=[
            pl.BlockSpec((tm, ATT_WIDTH), p_blk),
            pl.BlockSpec((tm, GLA_WIDTH), p_blk),
            pl.BlockSpec((tm, D_MODEL), p_blk),
            pl.BlockSpec((tm, ATT_WIDTH), s_blk),
            pl.BlockSpec((tm, GLA_WIDTH), s_blk),
            pl.BlockSpec((tm, D_MODEL), s_blk),
            pl.BlockSpec((1, D_MODEL), lambda i: (0, 0)),
            pl.BlockSpec(memory_space=pl.ANY),
        ],
        out_specs=[
            pl.BlockSpec((tm, D_MODEL), p_blk),
            pl.BlockSpec((tm, D_MODEL), s_blk),
        ],
        out_shape=[
            jax.ShapeDtypeStruct(x_p.shape, F32),
            jax.ShapeDtypeStruct(x_s.shape, F32),
        ],
        scratch_shapes=[
            pltpu.VMEM((D_MODEL, D_MODEL), BF16),
            pltpu.VMEM((2, WO_CHUNK, D_MODEL), F32),
            pltpu.SemaphoreType.DMA((2,)),
        ],
        compiler_params=pltpu.CompilerParams(
            dimension_semantics=("arbitrary",), vmem_limit_bytes=vmem_bytes),
        name="out_proj",
    )(ma_p, mg_p, x_p, ma_s, mg_s, x_s, norm_w, w_out_l)


SWA_SAMPLE_SEQS = 8


def _swa_sample_kernel(rb_ref, sink_ref, q_ref, gate_ref, kvn_ref, ckt_ref, cvt_ref,
                       o_ref, kot_ref, vot_ref, tabc_ref, tabn_ref, sinkcol_ref, place_ref, *, t_new):
    step = pl.program_id(0)
    n_seq = q_ref.shape[0]
    W = ckt_ref.shape[2]
    rows = N_HEADS * t_new

    @pl.when(step == 0)
    def _build_tables():
        r = lax.broadcasted_iota(jnp.int32, (rows, W), 0)
        qi = r & (t_new - 1)
        kj = lax.broadcasted_iota(jnp.int32, (rows, W), 1)
        dist_c = qi + W - kj
        valid_c = (dist_c >= 0) & (dist_c < WINDOW)
        bucket_c = _t5_bucket(dist_c)
        rn = lax.broadcasted_iota(jnp.int32, (rows, t_new), 0)
        qn = rn & (t_new - 1)
        jn = lax.broadcasted_iota(jnp.int32, (rows, t_new), 1)
        dist_n = qn - jn
        valid_n = (dist_n >= 0) & (dist_n < WINDOW)
        bucket_n = _t5_bucket(dist_n)
        head_c = lax.shift_right_logical(r, int(math.log2(t_new)))
        head_n = lax.shift_right_logical(rn, int(math.log2(t_new)))
        head_s = lax.shift_right_logical(lax.broadcasted_iota(jnp.int32, (rows, 1), 0), int(math.log2(t_new)))
        tabc_ref[...] = jnp.full((rows, W), NEG, F32)
        tabn_ref[...] = jnp.full((rows, t_new), NEG, F32)
        sinkcol_ref[...] = jnp.zeros((rows, 1), F32)
        for g in range(GQA_GROUP):
            for h in range(N_KV_HEADS):
                hd = h * GQA_GROUP + g
                blk = g * N_KV_HEADS + h
                sinkcol_ref[...] = jnp.where(head_s == blk, sink_ref[hd], sinkcol_ref[...])

        def body(bk, carry):
            for g in range(GQA_GROUP):
                for h in range(N_KV_HEADS):
                    hd = h * GQA_GROUP + g
                    blk = g * N_KV_HEADS + h
                    val = rb_ref[bk, hd]
                    tabc_ref[...] = jnp.where((bucket_c == bk) & valid_c & (head_c == blk), val, tabc_ref[...])
                    tabn_ref[...] = jnp.where((bucket_n == bk) & valid_n & (head_n == blk), val, tabn_ref[...])
            return carry

        lax.fori_loop(0, N_BUCKETS, body, 0)
        src = lax.broadcasted_iota(jnp.int32, place_ref.shape, 0)
        dst = lax.broadcasted_iota(jnp.int32, place_ref.shape, 1)
        want = lax.shift_right_logical(src, int(math.log2(t_new))) * W + (W - t_new) + (src & (t_new - 1))
        place_ref[...] = (dst == want).astype(F32).astype(BF16)

    lane_head = lax.shift_right_logical(lax.broadcasted_iota(jnp.int32, (1, KV_WIDTH), 1), HEAD_SHIFT)
    scale = HEAD_DIM ** -0.5
    sink = sinkcol_ref[...]
    seqs = range(n_seq)
    scores = []
    for s in seqs:
        q = q_ref[s].astype(F32) * scale
        pieces = []
        for g in range(GQA_GROUP):
            qg = q[:, g * KV_WIDTH:(g + 1) * KV_WIDTH]
            for h in range(N_KV_HEADS):
                pieces.append(jnp.where(lane_head == h, qg, 0.0))
        lhs = jnp.concatenate(pieces, axis=0).astype(BF16)
        sc = jnp.dot(lhs, ckt_ref[s].astype(BF16), preferred_element_type=F32) + tabc_ref[...]
        sn = _nt_dot(lhs, kvn_ref[s, :, :KV_WIDTH].astype(BF16)) + tabn_ref[...]
        scores.append((sc, sn))
    probs = []
    for s in seqs:
        sc, sn = scores[s]
        m = jnp.maximum(jnp.maximum(jnp.max(sc, axis=-1, keepdims=True),
                                    jnp.max(sn, axis=-1, keepdims=True)), sink)
        ec = jnp.exp(sc - m)
        en = jnp.exp(sn - m)
        den = (jnp.sum(ec, axis=-1, keepdims=True) + jnp.sum(en, axis=-1, keepdims=True)
               + jnp.exp(sink - m))
        inv = 1.0 / den
        probs.append(((ec * inv).astype(BF16), (en * inv).astype(BF16)))
    for s in seqs:
        pc, pn = probs[s]
        o = (_nt_dot(pc, cvt_ref[s].astype(BF16))
             + jnp.dot(pn, kvn_ref[s, :, KV_WIDTH:].astype(BF16), preferred_element_type=F32))
        outs = []
        for g in range(GQA_GROUP):
            acc = jnp.zeros((t_new, KV_WIDTH), F32)
            for h in range(N_KV_HEADS):
                blk = g * N_KV_HEADS + h
                acc = jnp.where(lane_head == h, o[blk * t_new:(blk + 1) * t_new, :], acc)
            outs.append(acc)
        att = jnp.concatenate(outs, axis=1)
        gate = gate_ref[s].astype(F32)
        o_ref[s] = (att * gate).astype(BF16)

    new_rows = kvn_ref[...].reshape(n_seq * t_new, 2 * KV_WIDTH)
    hi = new_rows.astype(BF16)
    lo = (new_rows - hi.astype(F32)).astype(BF16)
    place = place_ref[...]
    is_new = lax.broadcasted_iota(jnp.int32, (1, W), 1) >= W - t_new
    for src_ref, dst_ref, cols in ((ckt_ref, kot_ref, slice(0, KV_WIDTH)), (cvt_ref, vot_ref, slice(KV_WIDTH, None))):
        placed = _tn_dot(hi[:, cols], place) + _tn_dot(lo[:, cols], place)
        for s in seqs:
            dst_ref[s] = jnp.where(is_new, placed[:, s * W:(s + 1) * W], pltpu.roll(src_ref[s], W - t_new, axis=1))


def _swa_sample(p3, kv3, cache_k, cache_v, rel_bias, sinks, seq0):
    t_new = p3.shape[1]
    db, _, w = cache_k.shape
    sb = SWA_SAMPLE_SEQS
    assert seq0 % sb == 0
    blk0 = seq0 // sb
    assert t_new & (t_new - 1) == 0 and w >= t_new and w <= WINDOW
    sb = SWA_SAMPLE_SEQS
    rows = N_HEADS * t_new
    return pl.pallas_call(
        functools.partial(_swa_sample_kernel, t_new=t_new),
        grid=(db // sb,),
        in_specs=[
            pl.BlockSpec(memory_space=pltpu.SMEM),
            pl.BlockSpec(memory_space=pltpu.SMEM),
            pl.BlockSpec((sb, t_new, ATT_WIDTH), lambda i: (blk0 + i, 0, COL_QA // ATT_WIDTH)),
            pl.BlockSpec((sb, t_new, ATT_WIDTH), lambda i: (blk0 + i, 0, COL_GATE_A // ATT_WIDTH)),
            pl.BlockSpec((sb, t_new, 2 * KV_WIDTH), lambda i: (blk0 + i, 0, 0)),
            pl.BlockSpec((sb, KV_WIDTH, w), lambda i: (i, 0, 0)),
            pl.BlockSpec((sb, KV_WIDTH, w), lambda i: (i, 0, 0)),
        ],
        out_specs=[
            pl.BlockSpec((sb, t_new, ATT_WIDTH), lambda i: (i, 0, 0)),
            pl.BlockSpec((sb, KV_WIDTH, w), lambda i: (i, 0, 0)),
            pl.BlockSpec((sb, KV_WIDTH, w), lambda i: (i, 0, 0)),
        ],
        out_shape=[
            jax.ShapeDtypeStruct((db, t_new, ATT_WIDTH), BF16),
            jax.ShapeDtypeStruct((db, KV_WIDTH, w), F32),
            jax.ShapeDtypeStruct((db, KV_WIDTH, w), F32),
        ],
        scratch_shapes=[
            pltpu.VMEM((rows, w), F32),
            pltpu.VMEM((rows, t_new), F32),
            pltpu.VMEM((rows, 1), F32),
            pltpu.VMEM((sb * t_new, sb * w), BF16),
        ],
        compiler_params=pltpu.CompilerParams(
            dimension_semantics=("arbitrary",), vmem_limit_bytes=VMEM_LIMIT),
        name="swa_sample",
    )(rel_bias, sinks, p3, p3, kv3, cache_k, cache_v)


GLA_SAMPLE_SEQS = 8


def _cumsum_rows8(g):
    row = lax.broadcasted_iota(jnp.int32, g.shape, 0)
    out = g
    for sh in (1, 2, 4):
        out = out + jnp.where(row >= sh, pltpu.roll(out, sh, axis=0), 0.0)
    return out


def _gla_sample_kernel(qg_ref, kg_ref, vg_ref, gate_ref, lf_ref, gn_ref, st_ref, o_ref, so_ref):
    T = qg_ref.shape[1]
    gn = gn_ref[...]
    t_col = lax.broadcasted_iota(jnp.int32, (T, 1), 0)
    zpad = jnp.zeros((8, GLA_DK), F32)
    seqs = range(qg_ref.shape[0])
    pairs = [(s, h) for s in seqs for h in range(GLA_HEADS)]
    zpad_v = jnp.zeros((8, GLA_DV), F32)

    def ksl(h):
        return slice(h * GLA_DK, (h + 1) * GLA_DK)

    def vsl(h):
        return slice(h * GLA_DV, (h + 1) * GLA_DV)

    qts, kts, b_lasts, vs_, cols = [], [], [], [], {}
    for s in seqs:
        q = qg_ref[s].astype(F32)
        k = kg_ref[s].astype(F32)
        bc = _cumsum_rows8(lf_ref[s])
        b_last = bc[T - 1:T, :]
        b_lasts.append(b_last)
        qts.append(q * jnp.exp(bc))
        kts.append(k * jnp.exp(b_last - bc))
        vs_.append(vg_ref[s].astype(F32))
        for j in range(T):
            bs = jnp.broadcast_to(bc[j:j + 1, :], bc.shape)
            ks_ = jnp.broadcast_to(k[j:j + 1, :], k.shape)
            p = q * jnp.exp(jnp.minimum(bc - bs, 0.0)) * ks_
            for h in range(GLA_HEADS):
                col = jnp.sum(p[:, ksl(h)], axis=-1, keepdims=True)
                cols[s, h, j] = jnp.where(t_col >= j, col, 0.0)

    inters = {}
    for s, h in pairs:
        qt16 = jnp.concatenate([qts[s][:, ksl(h)], zpad], axis=0).astype(BF16)
        inters[s, h] = jnp.dot(qt16, st_ref[s, h].astype(BF16), preferred_element_type=F32)[:T, :]
    for s, h in pairs:
        kt16 = jnp.concatenate([kts[s][:, ksl(h)], zpad], axis=0).astype(BF16)
        v16 = jnp.concatenate([vs_[s][:, vsl(h)], zpad_v], axis=0).astype(BF16)
        so_ref[s, h] = (_decay_column(jnp.exp(b_lasts[s][:, ksl(h)])) * st_ref[s, h]
                        + _tn_dot(kt16, v16))

    for s, h in pairs:
        o = inters[s, h]
        v = vs_[s][:, vsl(h)]
        for j in range(T):
            o = o + cols[s, h, j] * v[j:j + 1, :]
        o_ref[s, :, vsl(h)] = _gla_finish(o, gn, gate_ref[s, :, vsl(h)].astype(F32))


def _gla_sample(p3, lf3, state, gla_norm_row, seq0):
    t_new = p3.shape[1]
    db = state.shape[0]
    sb = GLA_SAMPLE_SEQS
    assert t_new == 8 and seq0 % sb == 0
    blk0 = seq0 // sb
    return pl.pallas_call(
        _gla_sample_kernel,
        grid=(db // sb,),
        in_specs=[
            pl.BlockSpec((sb, t_new, GLA_KEY_WIDTH), lambda i: (blk0 + i, 0, COL_QG // GLA_KEY_WIDTH)),
            pl.BlockSpec((sb, t_new, GLA_KEY_WIDTH), lambda i: (blk0 + i, 0, COL_KG // GLA_KEY_WIDTH)),
            pl.BlockSpec((sb, t_new, GLA_WIDTH), lambda i: (blk0 + i, 0, COL_VG // GLA_WIDTH)),
            pl.BlockSpec((sb, t_new, GLA_WIDTH), lambda i: (blk0 + i, 0, COL_GATE_G // GLA_WIDTH)),
            pl.BlockSpec((sb, t_new, GLA_KEY_WIDTH), lambda i: (blk0 + i, 0, 0)),
            pl.BlockSpec((1, GLA_DV), lambda i: (0, 0)),
            pl.BlockSpec((sb, GLA_HEADS, GLA_DK, GLA_DV), lambda i: (i, 0, 0, 0)),
        ],
        out_specs=[
            pl.BlockSpec((sb, t_new, GLA_WIDTH), lambda i: (i, 0, 0)),
            pl.BlockSpec((sb, GLA_HEADS, GLA_DK, GLA_DV), lambda i: (i, 0, 0, 0)),
        ],
        out_shape=[
            jax.ShapeDtypeStruct((db, t_new, GLA_WIDTH), BF16),
            jax.ShapeDtypeStruct((db, GLA_HEADS, GLA_DK, GLA_DV), F32),
        ],
        compiler_params=pltpu.CompilerParams(
            dimension_semantics=("arbitrary",), vmem_limit_bytes=VMEM_LIMIT),
        name="gla_sample",
    )(p3, p3, p3, p3, lf3, gla_norm_row, state)


def _window_feature_major(win):
    db, w = win.shape[:2]
    return win.transpose(0, 2, 3, 1).reshape(db, KV_WIDTH, w)


def _window_from_feature_major(win_t):
    db, _, w = win_t.shape
    return win_t.reshape(db, N_KV_HEADS, HEAD_DIM, w).transpose(0, 3, 1, 2)


def kernel(x_prompt, x_sample, cache_k_win, cache_v_win, state_gla, meta_tokens, rel_bias,
           norm_pre, norm_post, w_in, w_a2, b_a, attn_sinks, gla_norm, w_out):
    n_batch, seq, _ = x_prompt.shape
    db, t_new, _ = x_sample.shape
    depth = w_in.shape[0]
    assert depth == 1
    l = 0
    lp = seq + WINDOW
    n_blocks = lp // WINDOW
    n_chunks = lp // GLA_CHUNK
    w_cache = cache_k_win.shape[2]

    w_in_t = w_in[l].T
    assert sum(IN_SIZES[:-1]) == N_MAIN
    w_a2p = jnp.pad(w_a2[l], ((0, LANES - GLA_RANK), (0, 0))).astype(BF16)
    npre = norm_pre[l][None, :]
    npost = norm_post[l][None, :]
    ba = b_a[l][None, :]
    gn = gla_norm[l][None, :]
    sinks = attn_sinks[l]

    tm = IN_PROJ_TM
    x_meta = jnp.concatenate([jnp.zeros((tm - N_META, D_MODEL), x_prompt.dtype),
                              meta_tokens.astype(x_prompt.dtype)], axis=0)
    p_all, kv_all, lf_all = _in_proj(
        x_meta, x_prompt.reshape(n_batch * seq, D_MODEL), x_sample.reshape(db * t_new, D_MODEL),
        w_in_t, npre, w_a2p, ba)
    meta_blk = tm // WINDOW - 1
    sample_seq0 = (tm + n_batch * seq) // t_new

    mixed_a = _swa_prompt(p_all, rel_bias, sinks, n_batch, n_blocks, meta_blk)
    mixed_g, s_fin = _gla_prompt(p_all, lf_all, gn, n_batch, n_chunks, meta_blk)
    kv_p = jnp.stack([kv_all[tm + (b + 1) * seq - WINDOW:tm + (b + 1) * seq] for b in range(n_batch)])
    k_win_p = kv_p[..., :KV_WIDTH].reshape(1, n_batch, WINDOW, N_KV_HEADS, HEAD_DIM)
    v_win_p = kv_p[..., KV_WIDTH:].reshape(1, n_batch, WINDOW, N_KV_HEADS, HEAD_DIM)
    gla_p = s_fin[None]

    ps3 = p_all.reshape(-1, t_new, N_MAIN)
    mixed_as, k_win_s, v_win_s = _swa_sample(
        ps3, kv_all.reshape(-1, t_new, 2 * KV_WIDTH),
        _window_feature_major(cache_k_win[l]), _window_feature_major(cache_v_win[l]),
        rel_bias, sinks, sample_seq0)
    mixed_gs, s_new = _gla_sample(ps3, lf_all.reshape(-1, t_new, GLA_KEY_WIDTH), state_gla[l], gn, sample_seq0)

    y_prompt, y_sample = _out_proj(
        mixed_a, mixed_g.reshape(n_batch * seq, GLA_WIDTH), x_prompt.reshape(n_batch * seq, D_MODEL),
        mixed_as.reshape(db * t_new, ATT_WIDTH), mixed_gs.reshape(db * t_new, GLA_WIDTH),
        x_sample.reshape(db * t_new, D_MODEL), w_out[l], npost)
    y_prompt = y_prompt.reshape(n_batch, seq, D_MODEL)
    y_sample = y_sample.reshape(db, t_new, D_MODEL)
    k_win_s = _window_from_feature_major(k_win_s)[None]
    v_win_s = _window_from_feature_major(v_win_s)[None]

    return (y_prompt, y_sample, k_win_p, v_win_p, gla_p, k_win_s, v_win_s, s_new[None])
```

```python
import functools
import math

import jax
import jax.numpy as jnp
from jax import lax
from jax.experimental import pallas as pl
from jax.experimental.pallas import tpu as pltpu

F32 = jnp.float32
BF16 = jnp.bfloat16

D_MODEL = 2048
N_META = 16
ATT_WIDTH = 1024
HEAD_DIM = 64
N_HEADS = 16
N_KV_HEADS = 4
GQA_GROUP = 4
KV_WIDTH = 256
WINDOW = 128
N_BUCKETS = 32
MAX_DISTANCE = 128
GLA_WIDTH = 1024
GLA_HEADS = 4
GLA_DV = 256
GLA_DK = 128
GLA_KEY_WIDTH = 512
GLA_RANK = 16
GLA_NORMALIZER = 16.0
GLA_CHUNK = 64
FRONT_PAD = WINDOW - N_META
EPS = 1e-6
PAST_LEN = 8192

IN_SIZES = (ATT_WIDTH, KV_WIDTH, KV_WIDTH, ATT_WIDTH,
            GLA_KEY_WIDTH, GLA_KEY_WIDTH, GLA_WIDTH, GLA_WIDTH, GLA_RANK)

N_MAIN = 5632
TN = 512
COL_QA = 0
COL_GATE_A = 1024
COL_QG = 2048
COL_KG = 2560
COL_VG = 3072
COL_GATE_G = 4096
COL_KA = 5120
COL_VA = 5376
SILU_COLS = ((COL_GATE_A, ATT_WIDTH), (COL_GATE_G, GLA_WIDTH))
LANES = 128
NEG = -1e30
LOG2E = math.log2(math.e)

VMEM_LIMIT = 48 * 1024 * 1024


def _sigmoid(x):
    return 1.0 / (1.0 + jnp.exp(-x))


def _log_sigmoid(x):
    return jnp.minimum(x, 0.0) - jnp.log1p(jnp.exp(-jnp.abs(x)))


def _nt_dot(a, b):
    return lax.dot_general(a, b, (((1,), (1,)), ((), ())), preferred_element_type=F32)


def _tn_dot(a, b):
    return lax.dot_general(a, b, (((0,), (0,)), ((), ())), preferred_element_type=F32)


def _group_major_pieces(src0, dst0, n_rows):
    pieces = []
    for r in range(dst0, dst0 + n_rows, HEAD_DIM):
        g, h = divmod(r // HEAD_DIM, N_KV_HEADS)
        pieces.append((src0 + (h * GQA_GROUP + g) * HEAD_DIM, r - dst0, HEAD_DIM))
    return pieces


def _load_rows_as_bf16(src_hbm, chunk_pieces, stage_ref, sem, dst_ref, chunk_rows, after_chunk=None):
    def copies(c):
        return [pltpu.make_async_copy(src_hbm.at[pl.ds(src, n), :], stage_ref.at[c % 2, pl.ds(dst, n), :],
                                      sem.at[c % 2]) for src, dst, n in chunk_pieces[c]]

    for cp in copies(0):
        cp.start()
    for c in range(len(chunk_pieces)):
        if c + 1 < len(chunk_pieces):
            for cp in copies(c + 1):
                cp.start()
        for cp in copies(c):
            cp.wait()
        dst_ref[c * chunk_rows:(c + 1) * chunk_rows, :] = stage_ref[c % 2].astype(BF16)
        if after_chunk is not None:
            after_chunk(c)


W_CHUNK = 256
IN_PROJ_TM = 256
GATE_HEAD_AFTER_TILES = 3


def _in_proj_weight_pieces():
    src_off = dict(zip(("q_a", "k_a", "v_a", "gate_a", "q_g", "k_g", "v_g", "gate_g"),
                       (sum(IN_SIZES[:i]) for i in range(8))))
    chunks = []
    for name in ("q_a", "gate_a"):
        for d in range(0, ATT_WIDTH, W_CHUNK):
            chunks.append(_group_major_pieces(src_off[name], d, W_CHUNK))
    for name, width in (("q_g", GLA_KEY_WIDTH), ("k_g", GLA_KEY_WIDTH), ("v_g", GLA_WIDTH),
                        ("gate_g", GLA_WIDTH), ("k_a", KV_WIDTH), ("v_a", KV_WIDTH)):
        for d in range(0, width, W_CHUNK):
            chunks.append([(src_off[name] + d, 0, W_CHUNK)])
    return chunks


def _in_proj_kernel(xm_hbm, xp_hbm, xs_hbm, wt_hbm, nw_ref, wa2_ref, ba_ref,
                    p_ref, kv_ref, lf_ref,
                    wbf_ref, wz_ref, stage_ref, zstage_ref, xbuf_ref, hn0_ref, hn1_ref, lf0_ref, lf1_ref,
                    wsem, zsem, xsem, *, n_p, n_s):
    i = pl.program_id(0)
    tm = xbuf_ref.shape[1]
    n_steps = 1 + n_p + n_s

    def x_copy(step, slot, fn):
        @pl.when(step == 0)
        def _():
            fn(pltpu.make_async_copy(xm_hbm, xbuf_ref.at[slot], xsem.at[slot]))

        @pl.when((step >= 1) & (step <= n_p))
        def _():
            fn(pltpu.make_async_copy(xp_hbm.at[pl.ds((step - 1) * tm, tm), :], xbuf_ref.at[slot], xsem.at[slot]))

        @pl.when(step > n_p)
        def _():
            fn(pltpu.make_async_copy(xs_hbm.at[pl.ds((step - 1 - n_p) * tm, tm), :], xbuf_ref.at[slot],
                                     xsem.at[slot]))

    slot = i % 2
    nxt = 1 - slot

    def rmsnorm_to(x_slot, hn_out):
        x = xbuf_ref[x_slot]
        var = jnp.mean(x * x, axis=-1, keepdims=True)
        hn_out[...] = (x * lax.rsqrt(var + EPS) * nw_ref[...]).astype(BF16)

    def gate_head_to(hn_in, lf_out):
        z = _nt_dot(hn_in[...], wz_ref[...])
        logits = jnp.dot(z.astype(BF16), wa2_ref[...], preferred_element_type=F32) + ba_ref[...]
        lf_out[...] = _log_sigmoid(logits) * (1.0 / GLA_NORMALIZER)

    def project_tile(hn, j):
        acc = _nt_dot(hn, wbf_ref[j * TN:(j + 1) * TN, :])
        if j * TN == COL_KA:
            kv_ref[...] = acc
        if any(c <= j * TN < c + w for c, w in SILU_COLS):
            acc = acc * _sigmoid(acc)
        if COL_QG <= j * TN < COL_QG + GLA_KEY_WIDTH:
            acc = acc * (GLA_DK ** -0.5)
        p_ref[:, j * TN:(j + 1) * TN] = acc.astype(BF16)

    @pl.when(i == 0)
    def _first_step():
        x_copy(0, 0, lambda cp: cp.start())
        x_copy(1, 1, lambda cp: cp.start())

        z_copy = pltpu.make_async_copy(wt_hbm.at[pl.ds(N_MAIN, GLA_RANK), :], zstage_ref, zsem.at[0])
        z_copy.start()
        wz_ref[...] = jnp.zeros(wz_ref.shape, BF16)
        z_copy.wait()
        wz_ref[:GLA_RANK, :] = zstage_ref[...].astype(BF16)
        x_copy(0, 0, lambda cp: cp.wait())
        rmsnorm_to(0, hn0_ref)
        gate_head_to(hn0_ref, lf0_ref)
        lf_ref[...] = lf0_ref[...]
        hn = hn0_ref[...]
        chunks_per_tile = TN // W_CHUNK

        def project_ready_tile(c):
            if (c + 1) % chunks_per_tile == 0:
                project_tile(hn, c // chunks_per_tile)

        _load_rows_as_bf16(wt_hbm, _in_proj_weight_pieces(), stage_ref, wsem, wbf_ref, W_CHUNK, project_ready_tile)

    @pl.when(i + 2 < n_steps)
    def _prefetch():
        x_copy(i + 2, slot, lambda cp: cp.start())

    @pl.when(i + 1 < n_steps)
    def _():
        x_copy(i + 1, nxt, lambda cp: cp.wait())

    def project(hn_cur, lf_cur, hn_nxt, lf_nxt):
        lf_ref[...] = lf_cur[...]
        hn = hn_cur[...]
        rmsnorm_to(nxt, hn_nxt)
        for j in range(N_MAIN // TN):
            if j == GATE_HEAD_AFTER_TILES:
                gate_head_to(hn_nxt, lf_nxt)
            project_tile(hn, j)

    @pl.when(i == 0)
    def _():
        rmsnorm_to(1, hn1_ref)
        gate_head_to(hn1_ref, lf1_ref)

    @pl.when((slot == 0) & (i > 0))
    def _():
        project(hn0_ref, lf0_ref, hn1_ref, lf1_ref)

    @pl.when(slot == 1)
    def _():
        project(hn1_ref, lf1_ref, hn0_ref, lf0_ref)


def _in_proj(x_meta, x_prompt, x_sample, w_in_t, norm_w, w_a2p, b_a):
    tm = IN_PROJ_TM
    n_p = x_prompt.shape[0] // tm
    n_s = x_sample.shape[0] // tm
    assert x_meta.shape[0] == tm and x_prompt.shape[0] % tm == 0 and x_sample.shape[0] % tm == 0
    assert COL_KA % TN == 0 and len(_in_proj_weight_pieces()) * W_CHUNK == N_MAIN
    m_all = (1 + n_p + n_s) * tm

    def const(i):
        return (0, 0)

    any_spec = pl.BlockSpec(memory_space=pl.ANY)
    buffers = (D_MODEL * N_MAIN * 2 + 2 * D_MODEL * W_CHUNK * 4 + 2 * tm * D_MODEL * 4
               + 2 * tm * N_MAIN * 2 + 2 * 2 * tm * GLA_KEY_WIDTH * 4
               + D_MODEL * LANES * 2 + D_MODEL * GLA_RANK * 4 + 2 * LANES * GLA_KEY_WIDTH * 2)
    buffers += 2 * tm * D_MODEL * 2 + 2 * tm * GLA_KEY_WIDTH * 4
    temps = tm * D_MODEL * (4 + 4) + 2 * tm * TN * 4 + 2 * tm * GLA_KEY_WIDTH * 4
    vmem_bytes = buffers + temps
    return pl.pallas_call(
        functools.partial(_in_proj_kernel, n_p=n_p, n_s=n_s),
        grid=(1 + n_p + n_s,),
        in_specs=[
            any_spec, any_spec, any_spec, any_spec,
            pl.BlockSpec((1, D_MODEL), const),
            pl.BlockSpec((LANES, GLA_KEY_WIDTH), const),
            pl.BlockSpec((1, GLA_KEY_WIDTH), const),
        ],
        out_specs=[
            pl.BlockSpec((tm, N_MAIN), lambda i: (i, 0)),
            pl.BlockSpec((tm, 2 * KV_WIDTH), lambda i: (i, 0)),
            pl.BlockSpec((tm, GLA_KEY_WIDTH), lambda i: (i, 0)),
        ],
        out_shape=[
            jax.ShapeDtypeStruct((m_all, N_MAIN), BF16),
            jax.ShapeDtypeStruct((m_all, 2 * KV_WIDTH), F32),
            jax.ShapeDtypeStruct((m_all, GLA_KEY_WIDTH), F32),
        ],
        scratch_shapes=[
            pltpu.VMEM((N_MAIN, D_MODEL), BF16),
            pltpu.VMEM((LANES, D_MODEL), BF16),
            pltpu.VMEM((2, W_CHUNK, D_MODEL), F32),
            pltpu.VMEM((GLA_RANK, D_MODEL), F32),
            pltpu.VMEM((2, tm, D_MODEL), F32),
            pltpu.VMEM((tm, D_MODEL), BF16),
            pltpu.VMEM((tm, D_MODEL), BF16),
            pltpu.VMEM((tm, GLA_KEY_WIDTH), F32),
            pltpu.VMEM((tm, GLA_KEY_WIDTH), F32),
            pltpu.SemaphoreType.DMA((2,)),
            pltpu.SemaphoreType.DMA((1,)),
            pltpu.SemaphoreType.DMA((2,)),
        ],
        compiler_params=pltpu.CompilerParams(
            dimension_semantics=("arbitrary",), vmem_limit_bytes=vmem_bytes),
        name="in_proj",
    )(x_meta, x_prompt, x_sample, w_in_t, norm_w, w_a2p, b_a)


def _t5_bucket(dist):
    max_exact = N_BUCKETS // 2
    d = jnp.maximum(dist, 0)
    ratio = jnp.maximum(d, max_exact).astype(F32) / max_exact
    val = jnp.log(ratio) / math.log(MAX_DISTANCE / max_exact) * (N_BUCKETS - max_exact)
    large = jnp.full(dist.shape, max_exact, jnp.int32)
    for step in range(1, N_BUCKETS - max_exact):
        large = large + (val >= step).astype(jnp.int32)
    return jnp.where(d < max_exact, d, large)


SWA_BLOCKS_PER_STEP = 2
HEAD_SHIFT = HEAD_DIM.bit_length() - 1


def _swa_prompt_kernel(rb_ref, sink_ref, qg_ref, kvp_ref, kvc_ref, o_ref, tab_ref):
    b = pl.program_id(0)
    m = pl.program_id(1)
    W2 = 2 * WINDOW
    NL = N_KV_HEADS * W2
    n_sub = qg_ref.shape[0] // WINDOW

    @pl.when((b == 0) & (m == 0))
    def _build_bias_tables():
        qi = lax.broadcasted_iota(jnp.int32, (WINDOW, W2), 0)
        kj = lax.broadcasted_iota(jnp.int32, (WINDOW, W2), 1)
        dist = qi - kj + WINDOW
        valid = (dist >= 0) & (dist < WINDOW)
        bucket = _t5_bucket(dist)
        for g in range(GQA_GROUP):
            tab_ref[1, g] = jnp.full((WINDOW, NL), NEG, F32)

        def body(bk, carry):
            hit = (bucket == bk) & valid
            for g in range(GQA_GROUP):
                for h in range(N_KV_HEADS):
                    cur = tab_ref[1, g, :, h * W2:(h + 1) * W2]
                    tab_ref[1, g, :, h * W2:(h + 1) * W2] = jnp.where(
                        hit, rb_ref[bk, h * GQA_GROUP + g] * LOG2E, cur)
            return carry

        lax.fori_loop(0, N_BUCKETS, body, 0)
        kvalid = kj - FRONT_PAD >= 0
        for g in range(GQA_GROUP):
            for h in range(N_KV_HEADS):
                tab_ref[0, g, :, h * W2:(h + 1) * W2] = jnp.where(
                    kvalid, tab_ref[1, g, :, h * W2:(h + 1) * W2], NEG)

    lane_head = lax.shift_right_logical(lax.broadcasted_iota(jnp.int32, (1, KV_WIDTH), 1), HEAD_SHIFT)
    zero = jnp.zeros((), BF16)

    def per_head(x):
        return [jnp.where(lane_head == h, x, zero) for h in range(N_KV_HEADS)]

    def rows(r):
        return slice(r * WINDOW, (r + 1) * WINDOW)

    km = [per_head(kvp_ref[:, :KV_WIDTH])] + [per_head(kvc_ref[rows(r), :KV_WIDTH]) for r in range(n_sub)]
    vm = [per_head(kvp_ref[:, KV_WIDTH:])] + [per_head(kvc_ref[rows(r), KV_WIDTH:]) for r in range(n_sub)]

    def stacked(parts, r):
        return jnp.concatenate([blk for h in range(N_KV_HEADS) for blk in (parts[r][h], parts[r + 1][h])], axis=0)

    scale = HEAD_DIM ** -0.5 * LOG2E
    scores = {}
    for r in range(n_sub):
        kkm = stacked(km, r)
        tv = jnp.where(m == 0, 0, 1) if r == 0 else 1
        for g in range(GQA_GROUP):
            qg = (qg_ref[rows(r), g * KV_WIDTH:(g + 1) * KV_WIDTH].astype(F32) * scale).astype(BF16)
            scores[r, g] = _nt_dot(qg, kkm) + tab_ref[tv, g]
    heads = [(g, h) for g in range(GQA_GROUP) for h in range(N_KV_HEADS)]
    sinks = {(g, h): sink_ref[h * GQA_GROUP + g] * LOG2E for g, h in heads}
    for r in range(n_sub):
        vvm = stacked(vm, r)
        sl = {(g, h): scores[r, g][:, h * W2:(h + 1) * W2] for g, h in heads}
        mx = {gh: jnp.maximum(jnp.max(sl[gh], axis=-1, keepdims=True), sinks[gh]) for gh in heads}
        ex = {gh: jnp.exp2(sl[gh] - mx[gh]) for gh in heads}
        den = {gh: jnp.sum(ex[gh], axis=-1, keepdims=True) + jnp.exp2(sinks[gh] - mx[gh]) for gh in heads}
        for g in range(GQA_GROUP):
            p = jnp.concatenate([ex[g, h].astype(BF16) for h in range(N_KV_HEADS)], axis=1)
            o = jnp.dot(p, vvm, preferred_element_type=F32)
            inv = jnp.zeros((WINDOW, KV_WIDTH), F32)
            for h in range(N_KV_HEADS):
                inv = jnp.where(lane_head == h, 1.0 / den[g, h], inv)
            gate = qg_ref[rows(r), ATT_WIDTH + g * KV_WIDTH:ATT_WIDTH + (g + 1) * KV_WIDTH].astype(F32)
            o_ref[rows(r), g * KV_WIDTH:(g + 1) * KV_WIDTH] = (o * inv * gate).astype(BF16)


def _swa_prompt(p_all, rel_bias, sinks, n_batch, n_blocks, meta_blk):
    nb = SWA_BLOCKS_PER_STEP
    n_real = n_blocks - 1
    rows = nb * WINDOW
    assert n_real % nb == 0 and (meta_blk + 1) % nb == 0
    assert COL_GATE_A == COL_QA + ATT_WIDTH and COL_VA == COL_KA + KV_WIDTH and COL_QA == 0
    steps = n_real // nb
    first = (meta_blk + 1) // nb

    def cur(b, m):
        return first + b * steps + m

    def prev(b, m):
        return jnp.where(m == 0, meta_blk, meta_blk + b * n_real + nb * m)

    return pl.pallas_call(
        _swa_prompt_kernel,
        grid=(n_batch, steps),
        in_specs=[
            pl.BlockSpec(memory_space=pltpu.SMEM),
            pl.BlockSpec(memory_space=pltpu.SMEM),
            pl.BlockSpec((rows, 2 * ATT_WIDTH), lambda b, m: (cur(b, m), 0)),
            pl.BlockSpec((WINDOW, 2 * KV_WIDTH), lambda b, m: (prev(b, m), COL_KA // (2 * KV_WIDTH))),
            pl.BlockSpec((rows, 2 * KV_WIDTH), lambda b, m: (cur(b, m), COL_KA // (2 * KV_WIDTH))),
        ],
        out_specs=pl.BlockSpec((rows, ATT_WIDTH), lambda b, m: (b * steps + m, 0)),
        out_shape=jax.ShapeDtypeStruct((n_batch * n_real * WINDOW, ATT_WIDTH), BF16),
        scratch_shapes=[pltpu.VMEM((2, GQA_GROUP, WINDOW, N_KV_HEADS * 2 * WINDOW), F32)],
        compiler_params=pltpu.CompilerParams(
            dimension_semantics=("arbitrary", "arbitrary"), vmem_limit_bytes=VMEM_LIMIT),
        name="swa_prompt",
    )(rel_bias, sinks, p_all, p_all, p_all)


def _diag_scores(q3, k3, b3, width):
    nb = q3.shape[0]
    n_heads = q3.shape[2] // GLA_DK
    r_io = lax.broadcasted_iota(jnp.int32, (nb, 8, width), 0)
    t_io = lax.broadcasted_iota(jnp.int32, (nb, 8, width), 1)
    l_io = lax.broadcasted_iota(jnp.int32, (nb, 8, width), 2)
    s_of_lane = l_io - 8 * r_io
    outs = [jnp.zeros((nb, 8, width), F32) for _ in range(n_heads)]
    for s in range(8):
        bs = jnp.broadcast_to(b3[:, s:s + 1, :], b3.shape)
        ks = jnp.broadcast_to(k3[:, s:s + 1, :], k3.shape)
        e = jnp.exp(jnp.minimum(b3 - bs, 0.0))
        p = q3 * e * ks
        for h in range(n_heads):
            col = jnp.sum(p[:, :, h * GLA_DK:(h + 1) * GLA_DK], axis=-1, keepdims=True)
            outs[h] = jnp.where(s_of_lane == s, col, outs[h])
    causal = (s_of_lane >= 0) & (s_of_lane <= t_io) & (s_of_lane < 8)
    return [jnp.where(causal, o, 0.0) for o in outs]


def _gla_finish(o, gn, gate):
    var = jnp.mean(o * o, axis=-1, keepdims=True)
    on = o * lax.rsqrt(var + EPS) * gn
    return (on * gate).astype(BF16)


def _decay_column(row):
    eye = (lax.broadcasted_iota(jnp.int32, (GLA_DK, GLA_DK), 0)
           == lax.broadcasted_iota(jnp.int32, (GLA_DK, GLA_DK), 1))
    return jnp.sum(jnp.where(eye, jnp.broadcast_to(row, (GLA_DK, GLA_DK)), 0.0), axis=1, keepdims=True)


def _gla_prompt_kernel(*refs):
    o_ref, s_ref = refs[-2:]
    gn_ref = refs[-3]
    n_batch = (len(refs) - 3) // 5
    qg_refs, kg_refs, vg_refs, gate_refs, lf_refs = (refs[j:5 * n_batch:5] for j in range(5))
    c = pl.program_id(0)
    C = GLA_CHUNK

    @pl.when(c == 0)
    def _():
        s_ref[...] = jnp.zeros(s_ref.shape, F32)

    row = lax.broadcasted_iota(jnp.int32, (C, 1), 0)
    valid = (c * C + row) >= FRONT_PAD
    tril = (lax.broadcasted_iota(jnp.int32, (C, C), 0)
            >= lax.broadcasted_iota(jnp.int32, (C, C), 1)).astype(F32).astype(BF16)
    t_io = lax.broadcasted_iota(jnp.int32, (C, C), 0)
    s_io = lax.broadcasted_iota(jnp.int32, (C, C), 1)
    n_levels = int(math.log2(C))
    level_masks = [(t_io > s_io) & (lax.shift_right_logical(t_io ^ s_io, sh) == 1) for sh in range(n_levels)]
    sub_io = lax.broadcasted_iota(jnp.int32, (C // 8, 8, 1), 1)
    gn = gn_ref[...]

    heads = [(b, h) for b in range(n_batch) for h in range(GLA_HEADS)]

    def ksl(h):
        return slice(h * GLA_DK, (h + 1) * GLA_DK)

    def vsl(h):
        return slice(h * GLA_DV, (h + 1) * GLA_DV)

    qs, kks, bcs = [], [], []
    for b in range(n_batch):
        qs.append(qg_refs[b][...].astype(F32))
        kks.append(jnp.where(valid, kg_refs[b][...].astype(F32), 0.0))
        g = jnp.where(valid, lf_refs[b][...], 0.0)
        g1 = g.astype(BF16)
        r1 = g - g1.astype(F32)
        g2 = r1.astype(BF16)
        g3 = (r1 - g2.astype(F32)).astype(BF16)
        bcs.append(jnp.dot(tril, g1, preferred_element_type=F32)
                   + jnp.dot(tril, g2, preferred_element_type=F32)
                   + jnp.dot(tril, g3, preferred_element_type=F32))

    qts, kts, yss, qks, b_lasts = [], [], [], [], []
    for b in range(n_batch):
        q, k, bc = qs[b], kks[b], bcs[b]
        b_last = bc[C - 1:C, :]
        b_lasts.append(b_last)
        qts.append((q * jnp.exp(bc)).astype(BF16))
        kts.append((k * jnp.exp(b_last - bc)).astype(BF16))
        b3 = bc.reshape(C // 8, 8, -1)
        ys = []
        for sh in range(n_levels):
            half = 1 << sh
            if half >= 8:
                ref = jnp.concatenate(
                    [jnp.broadcast_to(bc[i + half - 1:i + half, :], (2 * half, bc.shape[1]))
                     for i in range(0, C, 2 * half)], axis=0)
            elif half == 4:
                ref = jnp.broadcast_to(b3[:, 3:4, :], b3.shape).reshape(bc.shape)
            elif half == 2:
                ref = jnp.where(sub_io >= 4, jnp.broadcast_to(b3[:, 5:6, :], b3.shape),
                                jnp.broadcast_to(b3[:, 1:2, :], b3.shape)).reshape(bc.shape)
            else:
                ref = jnp.where((row & 1) == 1, pltpu.roll(bc, 1, axis=0), bc)
            upper = (lax.shift_right_logical(row, sh) & 1) == 1
            ys.append((jnp.where(upper, q, k) * jnp.exp(-jnp.abs(bc - ref))).astype(BF16))
        yss.append(ys)
        qks.append(q * k)

    level_prods, inters = {}, {}
    for b, h in heads:
        level_prods[b, h] = [_nt_dot(yss[b][sh][:, ksl(h)], yss[b][sh][:, ksl(h)]) for sh in range(n_levels)]
    for b, h in heads:
        st = s_ref[b, h]
        inters[b, h] = jnp.dot(qts[b][:, ksl(h)], st.astype(BF16), preferred_element_type=F32)
        s_ref[b, h] = (_decay_column(jnp.exp(b_lasts[b][:, ksl(h)])) * st
                       + _tn_dot(kts[b][:, ksl(h)], vg_refs[b][:, vsl(h)]))

    for b, h in heads:
        a = jnp.where(t_io == s_io, jnp.sum(qks[b][:, ksl(h)], axis=-1, keepdims=True), 0.0)
        for sh in range(n_levels):
            a = jnp.where(level_masks[sh], level_prods[b, h][sh], a)
        o = jnp.dot(a.astype(BF16), vg_refs[b][:, vsl(h)], preferred_element_type=F32) + inters[b, h]
        o_ref[b, :, vsl(h)] = _gla_finish(o, gn, gate_refs[b][:, vsl(h)].astype(F32))


def _gla_prompt(p_all, lf_all, gla_norm_row, n_batch, n_chunks, meta_blk):
    skip = (FRONT_PAD + N_META) // GLA_CHUNK
    out_len = (n_chunks - skip) * GLA_CHUNK
    C = GLA_CHUNK
    per_blk = WINDOW // C

    def chunk(b):
        def f(c):
            return jnp.where(c < skip, meta_blk * per_blk + c, (meta_blk + 1) * per_blk + b * (n_chunks - skip) + c - skip)
        return f

    in_specs = []
    args = []
    for b in range(n_batch):
        f = chunk(b)
        in_specs += [
            pl.BlockSpec((C, GLA_KEY_WIDTH), lambda c, f=f: (f(c), COL_QG // GLA_KEY_WIDTH)),
            pl.BlockSpec((C, GLA_KEY_WIDTH), lambda c, f=f: (f(c), COL_KG // GLA_KEY_WIDTH)),
            pl.BlockSpec((C, GLA_WIDTH), lambda c, f=f: (f(c), COL_VG // GLA_WIDTH)),
            pl.BlockSpec((C, GLA_WIDTH), lambda c, f=f: (f(c), COL_GATE_G // GLA_WIDTH)),
            pl.BlockSpec((C, GLA_KEY_WIDTH), lambda c, f=f: (f(c), 0)),
        ]
        args += [p_all, p_all, p_all, p_all, lf_all]
    return pl.pallas_call(
        _gla_prompt_kernel,
        grid=(n_chunks,),
        in_specs=in_specs + [pl.BlockSpec((1, GLA_DV), lambda c: (0, 0))],
        out_specs=[
            pl.BlockSpec((n_batch, C, GLA_WIDTH), lambda c: (0, jnp.maximum(c - skip, 0), 0)),
            pl.BlockSpec((n_batch, GLA_HEADS, GLA_DK, GLA_DV), lambda c: (0, 0, 0, 0)),
        ],
        out_shape=[
            jax.ShapeDtypeStruct((n_batch, out_len, GLA_WIDTH), BF16),
            jax.ShapeDtypeStruct((n_batch, GLA_HEADS, GLA_DK, GLA_DV), F32),
        ],
        compiler_params=pltpu.CompilerParams(
            dimension_semantics=("arbitrary",), vmem_limit_bytes=VMEM_LIMIT),
        name="gla_prompt",
    )(*args, gla_norm_row)


OUT_PROJ_TM = 256
WO_CHUNK = 512


def _out_proj_kernel(map_ref, mgp_ref, xp_ref, mas_ref, mgs_ref, xs_ref, nw_ref, w_hbm,
                     yp_ref, ys_ref, wbf_ref, stage_ref, wsem, *, n_p):
    i = pl.program_id(0)

    def normalise_add(o, x_ref, y_ref):
        var = jnp.mean(o * o, axis=-1, keepdims=True)
        y_ref[...] = x_ref[...] + o * lax.rsqrt(var + EPS) * nw_ref[...]

    @pl.when(i == 0)
    def _first_step():
        chunks = [_group_major_pieces(0, d, WO_CHUNK) for d in range(0, ATT_WIDTH, WO_CHUNK)]
        chunks += [[(d, 0, WO_CHUNK)] for d in range(ATT_WIDTH, D_MODEL, WO_CHUNK)]
        partial = []

        def contract_chunk(c):
            r0 = c * WO_CHUNK
            src, off = (map_ref, r0) if r0 < ATT_WIDTH else (mgp_ref, r0 - ATT_WIDTH)
            part = jnp.dot(src[:, off:off + WO_CHUNK], wbf_ref[r0:r0 + WO_CHUNK, :], preferred_element_type=F32)
            partial[:] = [part if not partial else partial[0] + part]

        _load_rows_as_bf16(w_hbm, chunks, stage_ref, wsem, wbf_ref, WO_CHUNK, contract_chunk)
        normalise_add(partial[0], xp_ref, yp_ref)

    def finish(ma_ref, mg_ref, x_ref, y_ref):
        o = (jnp.dot(ma_ref[...], wbf_ref[:ATT_WIDTH, :], preferred_element_type=F32)
             + jnp.dot(mg_ref[...], wbf_ref[ATT_WIDTH:, :], preferred_element_type=F32))
        normalise_add(o, x_ref, y_ref)

    @pl.when((i > 0) & (i < n_p))
    def _():
        finish(map_ref, mgp_ref, xp_ref, yp_ref)

    @pl.when(i >= n_p)
    def _():
        finish(mas_ref, mgs_ref, xs_ref, ys_ref)


def _out_proj(ma_p, mg_p, x_p, ma_s, mg_s, x_s, w_out_l, norm_w):
    tm = OUT_PROJ_TM
    n_p = x_p.shape[0] // tm
    n_s = x_s.shape[0] // tm
    assert x_p.shape[0] % tm == 0 and x_s.shape[0] % tm == 0 and ATT_WIDTH + GLA_WIDTH == D_MODEL

    def p_blk(i):
        return (jnp.minimum(i, n_p - 1), 0)

    def s_blk(i):
        return (jnp.maximum(i - n_p, 0), 0)

    row_bytes = ATT_WIDTH * 2 + GLA_WIDTH * 2 + 2 * D_MODEL * 4
    vmem_bytes = (D_MODEL * D_MODEL * 2 + 2 * WO_CHUNK * D_MODEL * 4 + 2 * 2 * tm * row_bytes
                  + 3 * tm * D_MODEL * 4)
    return pl.pallas_call(
        functools.partial(_out_proj_kernel, n_p=n_p),
        grid=(n_p + n_s,),
        in_specs=[
            pl.BlockSpec((tm, ATT_WIDTH), p_blk),
            pl.BlockSpec((tm, GLA_WIDTH), p_blk),
            pl.BlockSpec((tm, D_MODEL), p_blk),
            pl.BlockSpec((tm, ATT_WIDTH), s_blk),
            pl.BlockSpec((tm, GLA_WIDTH), s_blk),
            pl.BlockSpec((tm, D_MODEL), s_blk),
            pl.BlockSpec((1, D_MODEL), lambda i: (0, 0)),
            pl.BlockSpec(memory_space=pl.ANY),
        ],
        out_specs=[
            pl.BlockSpec((tm, D_MODEL), p_blk),
            pl.BlockSpec((tm, D_MODEL), s_blk),
        ],
        out_shape=[
            jax.ShapeDtypeStruct(x_p.shape, F32),
            jax.ShapeDtypeStruct(x_s.shape, F32),
        ],
        scratch_shapes=[
            pltpu.VMEM((D_MODEL, D_MODEL), BF16),
            pltpu.VMEM((2, WO_CHUNK, D_MODEL), F32),
            pltpu.SemaphoreType.DMA((2,)),
        ],
        compiler_params=pltpu.CompilerParams(
            dimension_semantics=("arbitrary",), vmem_limit_bytes=vmem_bytes),
        name="out_proj",
    )(ma_p, mg_p, x_p, ma_s, mg_s, x_s, norm_w, w_out_l)


SWA_SAMPLE_SEQS = 8


def _swa_sample_kernel(rb_ref, sink_ref, q_ref, gate_ref, kvn_ref, ckt_ref, cvt_ref,
                       o_ref, kot_ref, vot_ref, tabc_ref, tabn_ref, sinkrow_ref, place_ref, *, t_new):
    step = pl.program_id(0)
    n_seq = q_ref.shape[0]
    W = ckt_ref.shape[2]
    rows = N_HEADS * t_new
    t_shift = int(math.log2(t_new))
    t_pad = 2 * t_new

    @pl.when(step == 0)
    def _build_tables():
        kj = lax.broadcasted_iota(jnp.int32, (W, rows), 0)
        r = lax.broadcasted_iota(jnp.int32, (W, rows), 1)
        qi = r & (t_new - 1)
        dist_c = qi + W - kj
        valid_c = (dist_c >= 0) & (dist_c < WINDOW)
        bucket_c = _t5_bucket(dist_c)
        jn = lax.broadcasted_iota(jnp.int32, (t_pad, rows), 0)
        rn = lax.broadcasted_iota(jnp.int32, (t_pad, rows), 1)
        dist_n = (rn & (t_new - 1)) - jn
        valid_n = (jn < t_new) & (dist_n >= 0) & (dist_n < WINDOW)
        bucket_n = _t5_bucket(dist_n)
        head_c = lax.shift_right_logical(r, t_shift)
        head_n = lax.shift_right_logical(rn, t_shift)
        head_s = lax.shift_right_logical(lax.broadcasted_iota(jnp.int32, (1, rows), 1), t_shift)
        tabc_ref[...] = jnp.full((W, rows), NEG, F32)
        tabn_ref[...] = jnp.full((t_pad, rows), NEG, F32)
        sinkrow_ref[...] = jnp.zeros((1, rows), F32)
        for g in range(GQA_GROUP):
            for h in range(N_KV_HEADS):
                hd = h * GQA_GROUP + g
                blk = g * N_KV_HEADS + h
                sinkrow_ref[...] = jnp.where(head_s == blk, sink_ref[hd], sinkrow_ref[...])

        def body(bk, carry):
            for g in range(GQA_GROUP):
                for h in range(N_KV_HEADS):
                    hd = h * GQA_GROUP + g
                    blk = g * N_KV_HEADS + h
                    val = rb_ref[bk, hd]
                    tabc_ref[...] = jnp.where((bucket_c == bk) & valid_c & (head_c == blk), val, tabc_ref[...])
                    tabn_ref[...] = jnp.where((bucket_n == bk) & valid_n & (head_n == blk), val, tabn_ref[...])
            return carry

        lax.fori_loop(0, N_BUCKETS, body, 0)
        src = lax.broadcasted_iota(jnp.int32, place_ref.shape, 0)
        dst = lax.broadcasted_iota(jnp.int32, place_ref.shape, 1)
        want = lax.shift_right_logical(src, t_shift) * W + (W - t_new) + (src & (t_new - 1))
        place_ref[...] = (dst == want).astype(F32).astype(BF16)

    lane_head = lax.shift_right_logical(lax.broadcasted_iota(jnp.int32, (1, KV_WIDTH), 1), HEAD_SHIFT)
    scale = HEAD_DIM ** -0.5
    sink = sinkrow_ref[...]
    seqs = range(n_seq)
    zpad = jnp.zeros((t_pad - t_new, KV_WIDTH), F32)
    keys = [ckt_ref[s].T.astype(BF16) for s in seqs]
    news = []
    for s in seqs:
        kn = jnp.concatenate([kvn_ref[s, :, :KV_WIDTH], zpad], axis=0).astype(BF16)
        vn = jnp.concatenate([kvn_ref[s, :, KV_WIDTH:], zpad], axis=0).astype(BF16)
        news.append((kn, vn))
    scores = []
    for s in seqs:
        q = q_ref[s].astype(F32) * scale
        pieces = []
        for g in range(GQA_GROUP):
            qg = q[:, g * KV_WIDTH:(g + 1) * KV_WIDTH]
            for h in range(N_KV_HEADS):
                pieces.append(jnp.where(lane_head == h, qg, 0.0))
        lhs = jnp.concatenate(pieces, axis=0).astype(BF16)
        sc = _nt_dot(keys[s], lhs) + tabc_ref[...]
        sn = _nt_dot(news[s][0], lhs) + tabn_ref[...]
        scores.append((sc, sn))
    probs = []
    for s in seqs:
        sc, sn = scores[s]
        m = jnp.maximum(jnp.maximum(jnp.max(sc, axis=0, keepdims=True),
                                    jnp.max(sn, axis=0, keepdims=True)), sink)
        ec = jnp.exp(sc - m)
        en = jnp.exp(sn - m)
        den = (jnp.sum(ec, axis=0, keepdims=True) + jnp.sum(en, axis=0, keepdims=True)
               + jnp.exp(sink - m))
        inv = 1.0 / den
        probs.append(((ec * inv).astype(BF16), (en * inv).astype(BF16)))
    outs_t = []
    for s in seqs:
        pc, pn = probs[s]
        outs_t.append(jnp.dot(cvt_ref[s].astype(BF16), pc, preferred_element_type=F32)
                      + _tn_dot(news[s][1], pn))
    for s in seqs:
        o = outs_t[s].T
        outs = []
        for g in range(GQA_GROUP):
            acc = jnp.zeros((t_new, KV_WIDTH), F32)
            for h in range(N_KV_HEADS):
                blk = g * N_KV_HEADS + h
                acc = jnp.where(lane_head == h, o[blk * t_new:(blk + 1) * t_new, :], acc)
            outs.append(acc)
        att = jnp.concatenate(outs, axis=1)
        o_ref[s] = (att * gate_ref[s].astype(F32)).astype(BF16)

    new_rows = kvn_ref[...].reshape(n_seq * t_new, 2 * KV_WIDTH)
    hi = new_rows.astype(BF16)
    lo = (new_rows - hi.astype(F32)).astype(BF16)
    place = place_ref[...]
    is_new = lax.broadcasted_iota(jnp.int32, (1, W), 1) >= W - t_new
    for src_ref, dst_ref, cols in ((ckt_ref, kot_ref, slice(0, KV_WIDTH)), (cvt_ref, vot_ref, slice(KV_WIDTH, None))):
        placed = _tn_dot(hi[:, cols], place) + _tn_dot(lo[:, cols], place)
        for s in seqs:
            dst_ref[s] = jnp.where(is_new, placed[:, s * W:(s + 1) * W], pltpu.roll(src_ref[s], W - t_new, axis=1))


def _swa_sample(p3, kv3, cache_k, cache_v, rel_bias, sinks, seq0):
    t_new = p3.shape[1]
    db, _, w = cache_k.shape
    sb = SWA_SAMPLE_SEQS
    assert seq0 % sb == 0
    blk0 = seq0 // sb
    assert t_new & (t_new - 1) == 0 and w >= t_new and w <= WINDOW
    sb = SWA_SAMPLE_SEQS
    rows = N_HEADS * t_new
    return pl.pallas_call(
        functools.partial(_swa_sample_kernel, t_new=t_new),
        grid=(db // sb,),
        in_specs=[
            pl.BlockSpec(memory_space=pltpu.SMEM),
            pl.BlockSpec(memory_space=pltpu.SMEM),
            pl.BlockSpec((sb, t_new, ATT_WIDTH), lambda i: (blk0 + i, 0, COL_QA // ATT_WIDTH)),
            pl.BlockSpec((sb, t_new, ATT_WIDTH), lambda i: (blk0 + i, 0, COL_GATE_A // ATT_WIDTH)),
            pl.BlockSpec((sb, t_new, 2 * KV_WIDTH), lambda i: (blk0 + i, 0, 0)),
            pl.BlockSpec((sb, KV_WIDTH, w), lambda i: (i, 0, 0)),
            pl.BlockSpec((sb, KV_WIDTH, w), lambda i: (i, 0, 0)),
        ],
        out_specs=[
            pl.BlockSpec((sb, t_new, ATT_WIDTH), lambda i: (i, 0, 0)),
            pl.BlockSpec((sb, KV_WIDTH, w), lambda i: (i, 0, 0)),
            pl.BlockSpec((sb, KV_WIDTH, w), lambda i: (i, 0, 0)),
        ],
        out_shape=[
            jax.ShapeDtypeStruct((db, t_new, ATT_WIDTH), BF16),
            jax.ShapeDtypeStruct((db, KV_WIDTH, w), F32),
            jax.ShapeDtypeStruct((db, KV_WIDTH, w), F32),
        ],
        scratch_shapes=[
            pltpu.VMEM((w, rows), F32),
            pltpu.VMEM((2 * t_new, rows), F32),
            pltpu.VMEM((1, rows), F32),
            pltpu.VMEM((sb * t_new, sb * w), BF16),
        ],
        compiler_params=pltpu.CompilerParams(
            dimension_semantics=("arbitrary",), vmem_limit_bytes=VMEM_LIMIT),
        name="swa_sample",
    )(rel_bias, sinks, p3, p3, kv3, cache_k, cache_v)


GLA_SAMPLE_SEQS = 8


def _cumsum_rows8(g):
    row = lax.broadcasted_iota(jnp.int32, g.shape, 0)
    out = g
    for sh in (1, 2, 4):
        out = out + jnp.where(row >= sh, pltpu.roll(out, sh, axis=0), 0.0)
    return out


def _gla_sample_kernel(qg_ref, kg_ref, vg_ref, gate_ref, lf_ref, gn_ref, st_ref, o_ref, so_ref):
    T = qg_ref.shape[1]
    gn = gn_ref[...]
    t_col = lax.broadcasted_iota(jnp.int32, (T, 1), 0)
    zpad = jnp.zeros((8, GLA_DK), F32)
    seqs = range(qg_ref.shape[0])
    pairs = [(s, h) for s in seqs for h in range(GLA_HEADS)]
    zpad_v = jnp.zeros((8, GLA_DV), F32)

    def ksl(h):
        return slice(h * GLA_DK, (h + 1) * GLA_DK)

    def vsl(h):
        return slice(h * GLA_DV, (h + 1) * GLA_DV)

    qts, kts, b_lasts, vs_, cols = [], [], [], [], {}
    for s in seqs:
        q = qg_ref[s].astype(F32)
        k = kg_ref[s].astype(F32)
        bc = _cumsum_rows8(lf_ref[s])
        b_last = bc[T - 1:T, :]
        b_lasts.append(b_last)
        qts.append(q * jnp.exp(bc))
        kts.append(k * jnp.exp(b_last - bc))
        vs_.append(vg_ref[s].astype(F32))
        for j in range(T):
            bs = jnp.broadcast_to(bc[j:j + 1, :], bc.shape)
            ks_ = jnp.broadcast_to(k[j:j + 1, :], k.shape)
            p = q * jnp.exp(jnp.minimum(bc - bs, 0.0)) * ks_
            for h in range(GLA_HEADS):
                col = jnp.sum(p[:, ksl(h)], axis=-1, keepdims=True)
                cols[s, h, j] = jnp.where(t_col >= j, col, 0.0)

    inters = {}
    for s, h in pairs:
        qt16 = jnp.concatenate([qts[s][:, ksl(h)], zpad], axis=0).astype(BF16)
        inters[s, h] = jnp.dot(qt16, st_ref[s, h].astype(BF16), preferred_element_type=F32)[:T, :]
    for s, h in pairs:
        kt16 = jnp.concatenate([kts[s][:, ksl(h)], zpad], axis=0).astype(BF16)
        v16 = jnp.concatenate([vs_[s][:, vsl(h)], zpad_v], axis=0).astype(BF16)
        so_ref[s, h] = (_decay_column(jnp.exp(b_lasts[s][:, ksl(h)])) * st_ref[s, h]
                        + _tn_dot(kt16, v16))

    for s, h in pairs:
        o = inters[s, h]
        v = vs_[s][:, vsl(h)]
        for j in range(T):
            o = o + cols[s, h, j] * v[j:j + 1, :]
        o_ref[s, :, vsl(h)] = _gla_finish(o, gn, gate_ref[s, :, vsl(h)].astype(F32))


def _gla_sample(p3, lf3, state, gla_norm_row, seq0):
    t_new = p3.shape[1]
    db = state.shape[0]
    sb = GLA_SAMPLE_SEQS
    assert t_new == 8 and seq0 % sb == 0
    blk0 = seq0 // sb
    return pl.pallas_call(
        _gla_sample_kernel,
        grid=(db // sb,),
        in_specs=[
            pl.BlockSpec((sb, t_new, GLA_KEY_WIDTH), lambda i: (blk0 + i, 0, COL_QG // GLA_KEY_WIDTH)),
            pl.BlockSpec((sb, t_new, GLA_KEY_WIDTH), lambda i: (blk0 + i, 0, COL_KG // GLA_KEY_WIDTH)),
            pl.BlockSpec((sb, t_new, GLA_WIDTH), lambda i: (blk0 + i, 0, COL_VG // GLA_WIDTH)),
            pl.BlockSpec((sb, t_new, GLA_WIDTH), lambda i: (blk0 + i, 0, COL_GATE_G // GLA_WIDTH)),
            pl.BlockSpec((sb, t_new, GLA_KEY_WIDTH), lambda i: (blk0 + i, 0, 0)),
            pl.BlockSpec((1, GLA_DV), lambda i: (0, 0)),
            pl.BlockSpec((sb, GLA_HEADS, GLA_DK, GLA_DV), lambda i: (i, 0, 0, 0)),
        ],
        out_specs=[
            pl.BlockSpec((sb, t_new, GLA_WIDTH), lambda i: (i, 0, 0)),
            pl.BlockSpec((sb, GLA_HEADS, GLA_DK, GLA_DV), lambda i: (i, 0, 0, 0)),
        ],
        out_shape=[
            jax.ShapeDtypeStruct((db, t_new, GLA_WIDTH), BF16),
            jax.ShapeDtypeStruct((db, GLA_HEADS, GLA_DK, GLA_DV), F32),
        ],
        compiler_params=pltpu.CompilerParams(
            dimension_semantics=("arbitrary",), vmem_limit_bytes=VMEM_LIMIT),
        name="gla_sample",
    )(p3, p3, p3, p3, lf3, gla_norm_row, state)


def _window_feature_major(win):
    db, w = win.shape[:2]
    return win.transpose(0, 2, 3, 1).reshape(db, KV_WIDTH, w)


def _window_from_feature_major(win_t):
    db, _, w = win_t.shape
    return win_t.reshape(db, N_KV_HEADS, HEAD_DIM, w).transpose(0, 3, 1, 2)


def kernel(x_prompt, x_sample, cache_k_win, cache_v_win, state_gla, meta_tokens, rel_bias,
           norm_pre, norm_post, w_in, w_a2, b_a, attn_sinks, gla_norm, w_out):
    n_batch, seq, _ = x_prompt.shape
    db, t_new, _ = x_sample.shape
    depth = w_in.shape[0]
    assert depth == 1
    l = 0
    lp = seq + WINDOW
    n_blocks = lp // WINDOW
    n_chunks = lp // GLA_CHUNK
    w_cache = cache_k_win.shape[2]

    w_in_t = w_in[l].T
    assert sum(IN_SIZES[:-1]) == N_MAIN
    w_a2p = jnp.pad(w_a2[l], ((0, LANES - GLA_RANK), (0, 0))).astype(BF16)
    npre = norm_pre[l][None, :]
    npost = norm_post[l][None, :]
    ba = b_a[l][None, :]
    gn = gla_norm[l][None, :]
    sinks = attn_sinks[l]

    tm = IN_PROJ_TM
    x_meta = jnp.concatenate([jnp.zeros((tm - N_META, D_MODEL), x_prompt.dtype),
                              meta_tokens.astype(x_prompt.dtype)], axis=0)
    p_all, kv_all, lf_all = _in_proj(
        x_meta, x_prompt.reshape(n_batch * seq, D_MODEL), x_sample.reshape(db * t_new, D_MODEL),
        w_in_t, npre, w_a2p, ba)
    meta_blk = tm // WINDOW - 1
    sample_seq0 = (tm + n_batch * seq) // t_new

    mixed_a = _swa_prompt(p_all, rel_bias, sinks, n_batch, n_blocks, meta_blk)
    mixed_g, s_fin = _gla_prompt(p_all, lf_all, gn, n_batch, n_chunks, meta_blk)
    kv_p = jnp.stack([kv_all[tm + (b + 1) * seq - WINDOW:tm + (b + 1) * seq] for b in range(n_batch)])
    k_win_p = kv_p[..., :KV_WIDTH].reshape(1, n_batch, WINDOW, N_KV_HEADS, HEAD_DIM)
    v_win_p = kv_p[..., KV_WIDTH:].reshape(1, n_batch, WINDOW, N_KV_HEADS, HEAD_DIM)
    gla_p = s_fin[None]

    ps3 = p_all.reshape(-1, t_new, N_MAIN)
    mixed_as, k_win_s, v_win_s = _swa_sample(
        ps3, kv_all.reshape(-1, t_new, 2 * KV_WIDTH),
        _window_feature_major(cache_k_win[l]), _window_feature_major(cache_v_win[l]),
        rel_bias, sinks, sample_seq0)
    mixed_gs, s_new = _gla_sample(ps3, lf_all.reshape(-1, t_new, GLA_KEY_WIDTH), state_gla[l], gn, sample_seq0)

    y_prompt, y_sample = _out_proj(
        mixed_a, mixed_g.reshape(n_batch * seq, GLA_WIDTH), x_prompt.reshape(n_batch * seq, D_MODEL),
        mixed_as.reshape(db * t_new, ATT_WIDTH), mixed_gs.reshape(db * t_new, GLA_WIDTH),
        x_sample.reshape(db * t_new, D_MODEL), w_out[l], npost)
    y_prompt = y_prompt.reshape(n_batch, seq, D_MODEL)
    y_sample = y_sample.reshape(db, t_new, D_MODEL)
    k_win_s = _window_from_feature_major(k_win_s)[None]
    v_win_s = _window_from_feature_major(v_win_s)[None]

    return (y_prompt, y_sample, k_win_p, v_win_p, gla_p, k_win_s, v_win_s, s_new[None])
```

```python
import functools
import math

import jax
import jax.numpy as jnp
from jax import lax
from jax.experimental import pallas as pl
from jax.experimental.pallas import tpu as pltpu

F32 = jnp.float32
BF16 = jnp.bfloat16

D_MODEL = 2048
N_META = 16
ATT_WIDTH = 1024
HEAD_DIM = 64
N_HEADS = 16
N_KV_HEADS = 4
GQA_GROUP = 4
KV_WIDTH = 256
WINDOW = 128
N_BUCKETS = 32
MAX_DISTANCE = 128
GLA_WIDTH = 1024
GLA_HEADS = 4
GLA_DV = 256
GLA_DK = 128
GLA_KEY_WIDTH = 512
GLA_RANK = 16
GLA_NORMALIZER = 16.0
GLA_CHUNK = 64
FRONT_PAD = WINDOW - N_META
EPS = 1e-6
PAST_LEN = 8192

IN_SIZES = (ATT_WIDTH, KV_WIDTH, KV_WIDTH, ATT_WIDTH,
            GLA_KEY_WIDTH, GLA_KEY_WIDTH, GLA_WIDTH, GLA_WIDTH, GLA_RANK)

N_MAIN = 5632
TN = 512
COL_QA = 0
COL_GATE_A = 1024
COL_QG = 2048
COL_KG = 2560
COL_VG = 3072
COL_GATE_G = 4096
COL_KA = 5120
COL_VA = 5376
SILU_COLS = ((COL_GATE_A, ATT_WIDTH), (COL_GATE_G, GLA_WIDTH))
LANES = 128
NEG = -1e30
LOG2E = math.log2(math.e)

VMEM_LIMIT = 48 * 1024 * 1024


def _sigmoid(x):
    return 1.0 / (1.0 + jnp.exp(-x))


def _log_sigmoid(x):
    return jnp.minimum(x, 0.0) - jnp.log1p(jnp.exp(-jnp.abs(x)))


def _nt_dot(a, b):
    return lax.dot_general(a, b, (((1,), (1,)), ((), ())), preferred_element_type=F32)


def _tn_dot(a, b):
    return lax.dot_general(a, b, (((0,), (0,)), ((), ())), preferred_element_type=F32)


def _group_major_pieces(src0, dst0, n_rows):
    pieces = []
    for r in range(dst0, dst0 + n_rows, HEAD_DIM):
        g, h = divmod(r // HEAD_DIM, N_KV_HEADS)
        pieces.append((src0 + (h * GQA_GROUP + g) * HEAD_DIM, r - dst0, HEAD_DIM))
    return pieces


def _load_rows_as_bf16(src_hbm, chunk_pieces, stage_ref, sem, dst_ref, chunk_rows, after_chunk=None):
    def copies(c):
        return [pltpu.make_async_copy(src_hbm.at[pl.ds(src, n), :], stage_ref.at[c % 2, pl.ds(dst, n), :],
                                      sem.at[c % 2]) for src, dst, n in chunk_pieces[c]]

    for cp in copies(0):
        cp.start()
    for c in range(len(chunk_pieces)):
        if c + 1 < len(chunk_pieces):
            for cp in copies(c + 1):
                cp.start()
        for cp in copies(c):
            cp.wait()
        dst_ref[c * chunk_rows:(c + 1) * chunk_rows, :] = stage_ref[c % 2].astype(BF16)
        if after_chunk is not None:
            after_chunk(c)


W_CHUNK = 256
IN_PROJ_TM = 256
GATE_HEAD_AFTER_TILES = 3


def _in_proj_weight_pieces():
    src_off = dict(zip(("q_a", "k_a", "v_a", "gate_a", "q_g", "k_g", "v_g", "gate_g"),
                       (sum(IN_SIZES[:i]) for i in range(8))))
    chunks = []
    for name in ("q_a", "gate_a"):
        for d in range(0, ATT_WIDTH, W_CHUNK):
            chunks.append(_group_major_pieces(src_off[name], d, W_CHUNK))
    for name, width in (("q_g", GLA_KEY_WIDTH), ("k_g", GLA_KEY_WIDTH), ("v_g", GLA_WIDTH),
                        ("gate_g", GLA_WIDTH), ("k_a", KV_WIDTH), ("v_a", KV_WIDTH)):
        for d in range(0, width, W_CHUNK):
            chunks.append([(src_off[name] + d, 0, W_CHUNK)])
    return chunks


def _in_proj_kernel(xm_hbm, xp_hbm, xs_hbm, wt_hbm, nw_ref, wa2_ref, ba_ref,
                    p_ref, kv_ref, lf_ref,
                    wbf_ref, wz_ref, stage_ref, zstage_ref, xbuf_ref, hn0_ref, hn1_ref, lf0_ref, lf1_ref,
                    wsem, zsem, xsem, *, n_p, n_s):
    i = pl.program_id(0)
    tm = xbuf_ref.shape[1]
    n_steps = 1 + n_p + n_s

    def x_copy(step, slot, fn):
        @pl.when(step == 0)
        def _():
            fn(pltpu.make_async_copy(xm_hbm, xbuf_ref.at[slot], xsem.at[slot]))

        @pl.when((step >= 1) & (step <= n_p))
        def _():
            fn(pltpu.make_async_copy(xp_hbm.at[pl.ds((step - 1) * tm, tm), :], xbuf_ref.at[slot], xsem.at[slot]))

        @pl.when(step > n_p)
        def _():
            fn(pltpu.make_async_copy(xs_hbm.at[pl.ds((step - 1 - n_p) * tm, tm), :], xbuf_ref.at[slot],
                                     xsem.at[slot]))

    slot = i % 2
    nxt = 1 - slot

    def rmsnorm_to(x_slot, hn_out):
        x = xbuf_ref[x_slot]
        var = jnp.mean(x * x, axis=-1, keepdims=True)
        hn_out[...] = (x * lax.rsqrt(var + EPS) * nw_ref[...]).astype(BF16)

    def gate_head_to(hn_in, lf_out):
        z = _nt_dot(hn_in[...], wz_ref[...])
        logits = jnp.dot(z.astype(BF16), wa2_ref[...], preferred_element_type=F32) + ba_ref[...]
        lf_out[...] = _log_sigmoid(logits) * (1.0 / GLA_NORMALIZER)

    def project_tile(hn, j):
        acc = _nt_dot(hn, wbf_ref[j * TN:(j + 1) * TN, :])
        if j * TN == COL_KA:
            kv_ref[...] = acc
        if any(c <= j * TN < c + w for c, w in SILU_COLS):
            acc = acc * _sigmoid(acc)
        if COL_QG <= j * TN < COL_QG + GLA_KEY_WIDTH:
            acc = acc * (GLA_DK ** -0.5)
        p_ref[:, j * TN:(j + 1) * TN] = acc.astype(BF16)

    @pl.when(i == 0)
    def _first_step():
        x_copy(0, 0, lambda cp: cp.start())
        x_copy(1, 1, lambda cp: cp.start())

        z_copy = pltpu.make_async_copy(wt_hbm.at[pl.ds(N_MAIN, GLA_RANK), :], zstage_ref, zsem.at[0])
        z_copy.start()
        wz_ref[...] = jnp.zeros(wz_ref.shape, BF16)
        z_copy.wait()
        wz_ref[:GLA_RANK, :] = zstage_ref[...].astype(BF16)
        x_copy(0, 0, lambda cp: cp.wait())
        rmsnorm_to(0, hn0_ref)
        gate_head_to(hn0_ref, lf0_ref)
        lf_ref[...] = lf0_ref[...]
        hn = hn0_ref[...]
        chunks_per_tile = TN // W_CHUNK

        def project_ready_tile(c):
            if (c + 1) % chunks_per_tile == 0:
                project_tile(hn, c // chunks_per_tile)

        _load_rows_as_bf16(wt_hbm, _in_proj_weight_pieces(), stage_ref, wsem, wbf_ref, W_CHUNK, project_ready_tile)

    @pl.when(i + 2 < n_steps)
    def _prefetch():
        x_copy(i + 2, slot, lambda cp: cp.start())

    @pl.when(i + 1 < n_steps)
    def _():
        x_copy(i + 1, nxt, lambda cp: cp.wait())

    def project(hn_cur, lf_cur, hn_nxt, lf_nxt):
        lf_ref[...] = lf_cur[...]
        hn = hn_cur[...]
        rmsnorm_to(nxt, hn_nxt)
        for j in range(N_MAIN // TN):
            if j == GATE_HEAD_AFTER_TILES:
                gate_head_to(hn_nxt, lf_nxt)
            project_tile(hn, j)

    @pl.when(i == 0)
    def _():
        rmsnorm_to(1, hn1_ref)
        gate_head_to(hn1_ref, lf1_ref)

    @pl.when((slot == 0) & (i > 0))
    def _():
        project(hn0_ref, lf0_ref, hn1_ref, lf1_ref)

    @pl.when(slot == 1)
    def _():
        project(hn1_ref, lf1_ref, hn0_ref, lf0_ref)


def _in_proj(x_meta, x_prompt, x_sample, w_in_t, norm_w, w_a2p, b_a):
    tm = IN_PROJ_TM
    n_p = x_prompt.shape[0] // tm
    n_s = x_sample.shape[0] // tm
    assert x_meta.shape[0] == tm and x_prompt.shape[0] % tm == 0 and x_sample.shape[0] % tm == 0
    assert COL_KA % TN == 0 and len(_in_proj_weight_pieces()) * W_CHUNK == N_MAIN
    m_all = (1 + n_p + n_s) * tm

    def const(i):
        return (0, 0)

    any_spec = pl.BlockSpec(memory_space=pl.ANY)
    buffers = (D_MODEL * N_MAIN * 2 + 2 * D_MODEL * W_CHUNK * 4 + 2 * tm * D_MODEL * 4
               + 2 * tm * N_MAIN * 2 + 2 * 2 * tm * GLA_KEY_WIDTH * 4
               + D_MODEL * LANES * 2 + D_MODEL * GLA_RANK * 4 + 2 * LANES * GLA_KEY_WIDTH * 2)
    buffers += 2 * tm * D_MODEL * 2 + 2 * tm * GLA_KEY_WIDTH * 4
    temps = tm * D_MODEL * (4 + 4) + 2 * tm * TN * 4 + 2 * tm * GLA_KEY_WIDTH * 4
    vmem_bytes = buffers + temps
    return pl.pallas_call(
        functools.partial(_in_proj_kernel, n_p=n_p, n_s=n_s),
        grid=(1 + n_p + n_s,),
        in_specs=[
            any_spec, any_spec, any_spec, any_spec,
            pl.BlockSpec((1, D_MODEL), const),
            pl.BlockSpec((LANES, GLA_KEY_WIDTH), const),
            pl.BlockSpec((1, GLA_KEY_WIDTH), const),
        ],
        out_specs=[
            pl.BlockSpec((tm, N_MAIN), lambda i: (i, 0)),
            pl.BlockSpec((tm, 2 * KV_WIDTH), lambda i: (i, 0)),
            pl.BlockSpec((tm, GLA_KEY_WIDTH), lambda i: (i, 0)),
        ],
        out_shape=[
            jax.ShapeDtypeStruct((m_all, N_MAIN), BF16),
            jax.ShapeDtypeStruct((m_all, 2 * KV_WIDTH), F32),
            jax.ShapeDtypeStruct((m_all, GLA_KEY_WIDTH), F32),
        ],
        scratch_shapes=[
            pltpu.VMEM((N_MAIN, D_MODEL), BF16),
            pltpu.VMEM((LANES, D_MODEL), BF16),
            pltpu.VMEM((2, W_CHUNK, D_MODEL), F32),
            pltpu.VMEM((GLA_RANK, D_MODEL), F32),
            pltpu.VMEM((2, tm, D_MODEL), F32),
            pltpu.VMEM((tm, D_MODEL), BF16),
            pltpu.VMEM((tm, D_MODEL), BF16),
            pltpu.VMEM((tm, GLA_KEY_WIDTH), F32),
            pltpu.VMEM((tm, GLA_KEY_WIDTH), F32),
            pltpu.SemaphoreType.DMA((2,)),
            pltpu.SemaphoreType.DMA((1,)),
            pltpu.SemaphoreType.DMA((2,)),
        ],
        compiler_params=pltpu.CompilerParams(
            dimension_semantics=("arbitrary",), vmem_limit_bytes=vmem_bytes),
        name="in_proj",
    )(x_meta, x_prompt, x_sample, w_in_t, norm_w, w_a2p, b_a)


def _t5_bucket(dist):
    max_exact = N_BUCKETS // 2
    d = jnp.maximum(dist, 0)
    ratio = jnp.maximum(d, max_exact).astype(F32) / max_exact
    val = jnp.log(ratio) / math.log(MAX_DISTANCE / max_exact) * (N_BUCKETS - max_exact)
    large = jnp.full(dist.shape, max_exact, jnp.int32)
    for step in range(1, N_BUCKETS - max_exact):
        large = large + (val >= step).astype(jnp.int32)
    return jnp.where(d < max_exact, d, large)


SWA_BLOCKS_PER_STEP = 2
HEAD_SHIFT = HEAD_DIM.bit_length() - 1


def _swa_prompt_kernel(rb_ref, sink_ref, qg_ref, kvp_ref, kvc_ref, o_ref, tab_ref):
    b = pl.program_id(0)
    m = pl.program_id(1)
    W2 = 2 * WINDOW
    NL = N_KV_HEADS * W2
    n_sub = qg_ref.shape[0] // WINDOW

    @pl.when((b == 0) & (m == 0))
    def _build_bias_tables():
        qi = lax.broadcasted_iota(jnp.int32, (WINDOW, W2), 0)
        kj = lax.broadcasted_iota(jnp.int32, (WINDOW, W2), 1)
        dist = qi - kj + WINDOW
        valid = (dist >= 0) & (dist < WINDOW)
        bucket = _t5_bucket(dist)
        for g in range(GQA_GROUP):
            tab_ref[1, g] = jnp.full((WINDOW, NL), NEG, F32)

        def body(bk, carry):
            hit = (bucket == bk) & valid
            for g in range(GQA_GROUP):
                for h in range(N_KV_HEADS):
                    cur = tab_ref[1, g, :, h * W2:(h + 1) * W2]
                    tab_ref[1, g, :, h * W2:(h + 1) * W2] = jnp.where(
                        hit, rb_ref[bk, h * GQA_GROUP + g] * LOG2E, cur)
            return carry

        lax.fori_loop(0, N_BUCKETS, body, 0)
        kvalid = kj - FRONT_PAD >= 0
        for g in range(GQA_GROUP):
            for h in range(N_KV_HEADS):
                tab_ref[0, g, :, h * W2:(h + 1) * W2] = jnp.where(
                    kvalid, tab_ref[1, g, :, h * W2:(h + 1) * W2], NEG)

    lane_head = lax.shift_right_logical(lax.broadcasted_iota(jnp.int32, (1, KV_WIDTH), 1), HEAD_SHIFT)
    zero = jnp.zeros((), BF16)

    def per_head(x):
        return [jnp.where(lane_head == h, x, zero) for h in range(N_KV_HEADS)]

    def rows(r):
        return slice(r * WINDOW, (r + 1) * WINDOW)

    km = [per_head(kvp_ref[:, :KV_WIDTH])] + [per_head(kvc_ref[rows(r), :KV_WIDTH]) for r in range(n_sub)]
    vm = [per_head(kvp_ref[:, KV_WIDTH:])] + [per_head(kvc_ref[rows(r), KV_WIDTH:]) for r in range(n_sub)]

    def stacked(parts, r):
        return jnp.concatenate([blk for h in range(N_KV_HEADS) for blk in (parts[r][h], parts[r + 1][h])], axis=0)

    scale = HEAD_DIM ** -0.5 * LOG2E
    scores = {}
    for r in range(n_sub):
        kkm = stacked(km, r)
        tv = jnp.where(m == 0, 0, 1) if r == 0 else 1
        for g in range(GQA_GROUP):
            qg = (qg_ref[rows(r), g * KV_WIDTH:(g + 1) * KV_WIDTH].astype(F32) * scale).astype(BF16)
            scores[r, g] = _nt_dot(qg, kkm) + tab_ref[tv, g]
    heads = [(g, h) for g in range(GQA_GROUP) for h in range(N_KV_HEADS)]
    sinks = {(g, h): sink_ref[h * GQA_GROUP + g] * LOG2E for g, h in heads}
    for r in range(n_sub):
        vvm = stacked(vm, r)
        sl = {(g, h): scores[r, g][:, h * W2:(h + 1) * W2] for g, h in heads}
        mx = {gh: jnp.maximum(jnp.max(sl[gh], axis=-1, keepdims=True), sinks[gh]) for gh in heads}
        ex = {gh: jnp.exp2(sl[gh] - mx[gh]) for gh in heads}
        den = {gh: jnp.sum(ex[gh], axis=-1, keepdims=True) + jnp.exp2(sinks[gh] - mx[gh]) for gh in heads}
        for g in range(GQA_GROUP):
            p = jnp.concatenate([ex[g, h].astype(BF16) for h in range(N_KV_HEADS)], axis=1)
            o = jnp.dot(p, vvm, preferred_element_type=F32)
            inv = jnp.zeros((WINDOW, KV_WIDTH), F32)
            for h in range(N_KV_HEADS):
                inv = jnp.where(lane_head == h, 1.0 / den[g, h], inv)
            gate = qg_ref[rows(r), ATT_WIDTH + g * KV_WIDTH:ATT_WIDTH + (g + 1) * KV_WIDTH].astype(F32)
            o_ref[rows(r), g * KV_WIDTH:(g + 1) * KV_WIDTH] = (o * inv * gate).astype(BF16)


def _swa_prompt(p_all, rel_bias, sinks, n_batch, n_blocks, meta_blk):
    nb = SWA_BLOCKS_PER_STEP
    n_real = n_blocks - 1
    rows = nb * WINDOW
    assert n_real % nb == 0 and (meta_blk + 1) % nb == 0
    assert COL_GATE_A == COL_QA + ATT_WIDTH and COL_VA == COL_KA + KV_WIDTH and COL_QA == 0
    steps = n_real // nb
    first = (meta_blk + 1) // nb

    def cur(b, m):
        return first + b * steps + m

    def prev(b, m):
        return jnp.where(m == 0, meta_blk, meta_blk + b * n_real + nb * m)

    return pl.pallas_call(
        _swa_prompt_kernel,
        grid=(n_batch, steps),
        in_specs=[
            pl.BlockSpec(memory_space=pltpu.SMEM),
            pl.BlockSpec(memory_space=pltpu.SMEM),
            pl.BlockSpec((rows, 2 * ATT_WIDTH), lambda b, m: (cur(b, m), 0)),
            pl.BlockSpec((WINDOW, 2 * KV_WIDTH), lambda b, m: (prev(b, m), COL_KA // (2 * KV_WIDTH))),
            pl.BlockSpec((rows, 2 * KV_WIDTH), lambda b, m: (cur(b, m), COL_KA // (2 * KV_WIDTH))),
        ],
        out_specs=pl.BlockSpec((rows, ATT_WIDTH), lambda b, m: (b * steps + m, 0)),
        out_shape=jax.ShapeDtypeStruct((n_batch * n_real * WINDOW, ATT_WIDTH), BF16),
        scratch_shapes=[pltpu.VMEM((2, GQA_GROUP, WINDOW, N_KV_HEADS * 2 * WINDOW), F32)],
        compiler_params=pltpu.CompilerParams(
            dimension_semantics=("arbitrary", "arbitrary"), vmem_limit_bytes=VMEM_LIMIT),
        name="swa_prompt",
    )(rel_bias, sinks, p_all, p_all, p_all)


def _diag_scores(q3, k3, b3, width):
    nb = q3.shape[0]
    n_heads = q3.shape[2] // GLA_DK
    r_io = lax.broadcasted_iota(jnp.int32, (nb, 8, width), 0)
    t_io = lax.broadcasted_iota(jnp.int32, (nb, 8, width), 1)
    l_io = lax.broadcasted_iota(jnp.int32, (nb, 8, width), 2)
    s_of_lane = l_io - 8 * r_io
    outs = [jnp.zeros((nb, 8, width), F32) for _ in range(n_heads)]
    for s in range(8):
        bs = jnp.broadcast_to(b3[:, s:s + 1, :], b3.shape)
        ks = jnp.broadcast_to(k3[:, s:s + 1, :], k3.shape)
        e = jnp.exp(jnp.minimum(b3 - bs, 0.0))
        p = q3 * e * ks
        for h in range(n_heads):
            col = jnp.sum(p[:, :, h * GLA_DK:(h + 1) * GLA_DK], axis=-1, keepdims=True)
            outs[h] = jnp.where(s_of_lane == s, col, outs[h])
    causal = (s_of_lane >= 0) & (s_of_lane <= t_io) & (s_of_lane < 8)
    return [jnp.where(causal, o, 0.0) for o in outs]


def _gla_finish(o, gn, gate):
    var = jnp.mean(o * o, axis=-1, keepdims=True)
    on = o * lax.rsqrt(var + EPS) * gn
    return (on * gate).astype(BF16)


def _decay_column(row):
    eye = (lax.broadcasted_iota(jnp.int32, (GLA_DK, GLA_DK), 0)
           == lax.broadcasted_iota(jnp.int32, (GLA_DK, GLA_DK), 1))
    return jnp.sum(jnp.where(eye, jnp.broadcast_to(row, (GLA_DK, GLA_DK)), 0.0), axis=1, keepdims=True)


def _gla_prompt_kernel(*refs):
    o_ref, s_ref = refs[-2:]
    gn_ref = refs[-3]
    n_batch = (len(refs) - 3) // 5
    qg_refs, kg_refs, vg_refs, gate_refs, lf_refs = (refs[j:5 * n_batch:5] for j in range(5))
    c = pl.program_id(0)
    C = GLA_CHUNK

    @pl.when(c == 0)
    def _():
        s_ref[...] = jnp.zeros(s_ref.shape, F32)

    row = lax.broadcasted_iota(jnp.int32, (C, 1), 0)
    valid = (c * C + row) >= FRONT_PAD
    tril = (lax.broadcasted_iota(jnp.int32, (C, C), 0)
            >= lax.broadcasted_iota(jnp.int32, (C, C), 1)).astype(F32).astype(BF16)
    t_io = lax.broadcasted_iota(jnp.int32, (C, C), 0)
    s_io = lax.broadcasted_iota(jnp.int32, (C, C), 1)
    n_levels = int(math.log2(C))
    level_masks = [(t_io > s_io) & (lax.shift_right_logical(t_io ^ s_io, sh) == 1) for sh in range(n_levels)]
    sub_io = lax.broadcasted_iota(jnp.int32, (C // 8, 8, 1), 1)
    gn = gn_ref[...]

    heads = [(b, h) for b in range(n_batch) for h in range(GLA_HEADS)]

    def ksl(h):
        return slice(h * GLA_DK, (h + 1) * GLA_DK)

    def vsl(h):
        return slice(h * GLA_DV, (h + 1) * GLA_DV)

    qs, kks, bcs = [], [], []
    for b in range(n_batch):
        qs.append(qg_refs[b][...].astype(F32))
        kks.append(jnp.where(valid, kg_refs[b][...].astype(F32), 0.0))
        g = jnp.where(valid, lf_refs[b][...], 0.0)
        g1 = g.astype(BF16)
        r1 = g - g1.astype(F32)
        g2 = r1.astype(BF16)
        g3 = (r1 - g2.astype(F32)).astype(BF16)
        bcs.append(jnp.dot(tril, g1, preferred_element_type=F32)
                   + jnp.dot(tril, g2, preferred_element_type=F32)
                   + jnp.dot(tril, g3, preferred_element_type=F32))

    qts, kts, yss, qks, b_lasts = [], [], [], [], []
    for b in range(n_batch):
        q, k, bc = qs[b], kks[b], bcs[b]
        b_last = bc[C - 1:C, :]
        b_lasts.append(b_last)
        qts.append((q * jnp.exp(bc)).astype(BF16))
        kts.append((k * jnp.exp(b_last - bc)).astype(BF16))
        b3 = bc.reshape(C // 8, 8, -1)
        ys = []
        for sh in range(n_levels):
            half = 1 << sh
            if half >= 8:
                ref = jnp.concatenate(
                    [jnp.broadcast_to(bc[i + half - 1:i + half, :], (2 * half, bc.shape[1]))
                     for i in range(0, C, 2 * half)], axis=0)
            elif half == 4:
                ref = jnp.broadcast_to(b3[:, 3:4, :], b3.shape).reshape(bc.shape)
            elif half == 2:
                ref = jnp.where(sub_io >= 4, jnp.broadcast_to(b3[:, 5:6, :], b3.shape),
                                jnp.broadcast_to(b3[:, 1:2, :], b3.shape)).reshape(bc.shape)
            else:
                ref = jnp.where((row & 1) == 1, pltpu.roll(bc, 1, axis=0), bc)
            upper = (lax.shift_right_logical(row, sh) & 1) == 1
            ys.append((jnp.where(upper, q, k) * jnp.exp(-jnp.abs(bc - ref))).astype(BF16))
        yss.append(ys)
        qks.append(q * k)

    level_prods, inters = {}, {}
    for b, h in heads:
        level_prods[b, h] = [_nt_dot(yss[b][sh][:, ksl(h)], yss[b][sh][:, ksl(h)]) for sh in range(n_levels)]
    for b, h in heads:
        st = s_ref[b, h]
        inters[b, h] = jnp.dot(qts[b][:, ksl(h)], st.astype(BF16), preferred_element_type=F32)
        s_ref[b, h] = (_decay_column(jnp.exp(b_lasts[b][:, ksl(h)])) * st
                       + _tn_dot(kts[b][:, ksl(h)], vg_refs[b][:, vsl(h)]))

    for b, h in heads:
        a = jnp.where(t_io == s_io, jnp.sum(qks[b][:, ksl(h)], axis=-1, keepdims=True), 0.0)
        for sh in range(n_levels):
            a = jnp.where(level_masks[sh], level_prods[b, h][sh], a)
        o = jnp.dot(a.astype(BF16), vg_refs[b][:, vsl(h)], preferred_element_type=F32) + inters[b, h]
        o_ref[b, :, vsl(h)] = _gla_finish(o, gn, gate_refs[b][:, vsl(h)].astype(F32))


def _gla_prompt(p_all, lf_all, gla_norm_row, n_batch, n_chunks, meta_blk):
    skip = (FRONT_PAD + N_META) // GLA_CHUNK
    out_len = (n_chunks - skip) * GLA_CHUNK
    C = GLA_CHUNK
    per_blk = WINDOW // C

    def chunk(b):
        def f(c):
            return jnp.where(c < skip, meta_blk * per_blk + c, (meta_blk + 1) * per_blk + b * (n_chunks - skip) + c - skip)
        return f

    in_specs = []
    args = []
    for b in range(n_batch):
        f = chunk(b)
        in_specs += [
            pl.BlockSpec((C, GLA_KEY_WIDTH), lambda c, f=f: (f(c), COL_QG // GLA_KEY_WIDTH)),
            pl.BlockSpec((C, GLA_KEY_WIDTH), lambda c, f=f: (f(c), COL_KG // GLA_KEY_WIDTH)),
            pl.BlockSpec((C, GLA_WIDTH), lambda c, f=f: (f(c), COL_VG // GLA_WIDTH)),
            pl.BlockSpec((C, GLA_WIDTH), lambda c, f=f: (f(c), COL_GATE_G // GLA_WIDTH)),
            pl.BlockSpec((C, GLA_KEY_WIDTH), lambda c, f=f: (f(c), 0)),
        ]
        args += [p_all, p_all, p_all, p_all, lf_all]
    return pl.pallas_call(
        _gla_prompt_kernel,
        grid=(n_chunks,),
        in_specs=in_specs + [pl.BlockSpec((1, GLA_DV), lambda c: (0, 0))],
        out_specs=[
            pl.BlockSpec((n_batch, C, GLA_WIDTH), lambda c: (0, jnp.maximum(c - skip, 0), 0)),
            pl.BlockSpec((n_batch, GLA_HEADS, GLA_DK, GLA_DV), lambda c: (0, 0, 0, 0)),
        ],
        out_shape=[
            jax.ShapeDtypeStruct((n_batch, out_len, GLA_WIDTH), BF16),
            jax.ShapeDtypeStruct((n_batch, GLA_HEADS, GLA_DK, GLA_DV), F32),
        ],
        compiler_params=pltpu.CompilerParams(
            dimension_semantics=("arbitrary",), vmem_limit_bytes=VMEM_LIMIT),
        name="gla_prompt",
    )(*args, gla_norm_row)


OUT_PROJ_TM = 256
WO_CHUNK = 512


def _out_proj_kernel(map_ref, mgp_ref, xp_ref, mas_ref, mgs_ref, xs_ref, nw_ref, w_hbm,
                     yp_ref, ys_ref, wbf_ref, stage_ref, wsem, *, n_p):
    i = pl.program_id(0)

    def normalise_add(o, x_ref, y_ref):
        var = jnp.mean(o * o, axis=-1, keepdims=True)
        y_ref[...] = x_ref[...] + o * lax.rsqrt(var + EPS) * nw_ref[...]

    @pl.when(i == 0)
    def _first_step():
        chunks = [_group_major_pieces(0, d, WO_CHUNK) for d in range(0, ATT_WIDTH, WO_CHUNK)]
        chunks += [[(d, 0, WO_CHUNK)] for d in range(ATT_WIDTH, D_MODEL, WO_CHUNK)]
        partial = []

        def contract_chunk(c):
            r0 = c * WO_CHUNK
            src, off = (map_ref, r0) if r0 < ATT_WIDTH else (mgp_ref, r0 - ATT_WIDTH)
            part = jnp.dot(src[:, off:off + WO_CHUNK], wbf_ref[r0:r0 + WO_CHUNK, :], preferred_element_type=F32)
            partial[:] = [part if not partial else partial[0] + part]

        _load_rows_as_bf16(w_hbm, chunks, stage_ref, wsem, wbf_ref, WO_CHUNK, contract_chunk)
        normalise_add(partial[0], xp_ref, yp_ref)

    def finish(ma_ref, mg_ref, x_ref, y_ref):
        o = (jnp.dot(ma_ref[...], wbf_ref[:ATT_WIDTH, :], preferred_element_type=F32)
             + jnp.dot(mg_ref[...], wbf_ref[ATT_WIDTH:, :], preferred_element_type=F32))
        normalise_add(o, x_ref, y_ref)

    @pl.when((i > 0) & (i < n_p))
    def _():
        finish(map_ref, mgp_ref, xp_ref, yp_ref)

    @pl.when(i >= n_p)
    def _():
        finish(mas_ref, mgs_ref, xs_ref, ys_ref)


def _out_proj(ma_p, mg_p, x_p, ma_s, mg_s, x_s, w_out_l, norm_w):
    tm = OUT_PROJ_TM
    n_p = x_p.shape[0] // tm
    n_s = x_s.shape[0] // tm
    assert x_p.shape[0] % tm == 0 and x_s.shape[0] % tm == 0 and ATT_WIDTH + GLA_WIDTH == D_MODEL

    def p_blk(i):
        return (jnp.minimum(i, n_p - 1), 0)

    def s_blk(i):
        return (jnp.maximum(i - n_p, 0), 0)

    row_bytes = ATT_WIDTH * 2 + GLA_WIDTH * 2 + 2 * D_MODEL * 4
    vmem_bytes = (D_MODEL * D_MODEL * 2 + 2 * WO_CHUNK * D_MODEL * 4 + 2 * 2 * tm * row_bytes
                  + 3 * tm * D_MODEL * 4)
    return pl.pallas_call(
        functools.partial(_out_proj_kernel, n_p=n_p),
        grid=(n_p + n_s,),
        in_specs=[
            pl.BlockSpec((tm, ATT_WIDTH), p_blk),
            pl.BlockSpec((tm, GLA_WIDTH), p_blk),
            pl.BlockSpec((tm, D_MODEL), p_blk),
            pl.BlockSpec((tm, ATT_WIDTH), s_blk),
            pl.BlockSpec((tm, GLA_WIDTH), s_blk),
            pl.BlockSpec((tm, D_MODEL), s_blk),
            pl.BlockSpec((1, D_MODEL), lambda i: (0, 0)),
            pl.BlockSpec(memory_space=pl.ANY),
        ],
        out_specs=[
            pl.BlockSpec((tm, D_MODEL), p_blk),
            pl.BlockSpec((tm, D_MODEL), s_blk),
        ],
        out_shape=[
            jax.ShapeDtypeStruct(x_p.shape, F32),
            jax.ShapeDtypeStruct(x_s.shape, F32),
        ],
        scratch_shapes=[
            pltpu.VMEM((D_MODEL, D_MODEL), BF16),
            pltpu.VMEM((2, WO_CHUNK, D_MODEL), F32),
            pltpu.SemaphoreType.DMA((2,)),
        ],
        compiler_params=pltpu.CompilerParams(
            dimension_semantics=("arbitrary",), vmem_limit_bytes=vmem_bytes),
        name="out_proj",
    )(ma_p, mg_p, x_p, ma_s, mg_s, x_s, norm_w, w_out_l)


def _swa_sample_kernel(rb_ref, sink_ref, q_ref, gate_ref, kvn_ref, ckt_ref, cvt_ref,
                       o_ref, kot_ref, vot_ref, tabc_ref, tabn_ref, sinkrow_ref, place_ref, *, t_new):
    step = pl.program_id(0)
    n_seq = q_ref.shape[0]
    W = ckt_ref.shape[2]
    rows = N_HEADS * t_new
    t_shift = int(math.log2(t_new))
    t_pad = 2 * t_new

    @pl.when(step == 0)
    def _build_tables():
        kj = lax.broadcasted_iota(jnp.int32, (W, rows), 0)
        r = lax.broadcasted_iota(jnp.int32, (W, rows), 1)
        qi = r & (t_new - 1)
        dist_c = qi + W - kj
        valid_c = (dist_c >= 0) & (dist_c < WINDOW)
        bucket_c = _t5_bucket(dist_c)
        jn = lax.broadcasted_iota(jnp.int32, (t_pad, rows), 0)
        rn = lax.broadcasted_iota(jnp.int32, (t_pad, rows), 1)
        dist_n = (rn & (t_new - 1)) - jn
        valid_n = (jn < t_new) & (dist_n >= 0) & (dist_n < WINDOW)
        bucket_n = _t5_bucket(dist_n)
        head_c = lax.shift_right_logical(r, t_shift)
        head_n = lax.shift_right_logical(rn, t_shift)
        head_s = lax.shift_right_logical(lax.broadcasted_iota(jnp.int32, (1, rows), 1), t_shift)
        tabc_ref[...] = jnp.full((W, rows), NEG, F32)
        tabn_ref[...] = jnp.full((t_pad, rows), NEG, F32)
        sinkrow_ref[...] = jnp.zeros((1, rows), F32)
        for g in range(GQA_GROUP):
            for h in range(N_KV_HEADS):
                hd = h * GQA_GROUP + g
                blk = g * N_KV_HEADS + h
                sinkrow_ref[...] = jnp.where(head_s == blk, sink_ref[hd], sinkrow_ref[...])

        def body(bk, carry):
            for g in range(GQA_GROUP):
                for h in range(N_KV_HEADS):
                    hd = h * GQA_GROUP + g
                    blk = g * N_KV_HEADS + h
                    val = rb_ref[bk, hd]
                    tabc_ref[...] = jnp.where((bucket_c == bk) & valid_c & (head_c == blk), val, tabc_ref[...])
                    tabn_ref[...] = jnp.where((bucket_n == bk) & valid_n & (head_n == blk), val, tabn_ref[...])
            return carry

        lax.fori_loop(0, N_BUCKETS, body, 0)
        src = lax.broadcasted_iota(jnp.int32, place_ref.shape, 0)
        dst = lax.broadcasted_iota(jnp.int32, place_ref.shape, 1)
        want = lax.shift_right_logical(src, t_shift) * W + (W - t_new) + (src & (t_new - 1))
        place_ref[...] = (dst == want).astype(F32).astype(BF16)

    lane_head = lax.shift_right_logical(lax.broadcasted_iota(jnp.int32, (1, KV_WIDTH), 1), HEAD_SHIFT)
    scale = HEAD_DIM ** -0.5
    sink = sinkrow_ref[...]
    seqs = range(n_seq)
    zpad = jnp.zeros((t_pad - t_new, KV_WIDTH), F32)
    keys = [ckt_ref[s].T.astype(BF16) for s in seqs]
    news = []
    for s in seqs:
        kn = jnp.concatenate([kvn_ref[s, :, :KV_WIDTH], zpad], axis=0).astype(BF16)
        vn = jnp.concatenate([kvn_ref[s, :, KV_WIDTH:], zpad], axis=0).astype(BF16)
        news.append((kn, vn))
    scores = []
    for s in seqs:
        q = q_ref[s].astype(F32) * scale
        pieces = []
        for g in range(GQA_GROUP):
            qg = q[:, g * KV_WIDTH:(g + 1) * KV_WIDTH]
            for h in range(N_KV_HEADS):
                pieces.append(jnp.where(lane_head == h, qg, 0.0))
        lhs = jnp.concatenate(pieces, axis=0).astype(BF16)
        sc = _nt_dot(keys[s], lhs) + tabc_ref[...]
        sn = _nt_dot(news[s][0], lhs) + tabn_ref[...]
        scores.append((sc, sn))
    probs = []
    for s in seqs:
        sc, sn = scores[s]
        m = jnp.maximum(jnp.maximum(jnp.max(sc, axis=0, keepdims=True),
                                    jnp.max(sn, axis=0, keepdims=True)), sink)
        ec = jnp.exp(sc - m)
        en = jnp.exp(sn - m)
        den = (jnp.sum(ec, axis=0, keepdims=True) + jnp.sum(en, axis=0, keepdims=True)
               + jnp.exp(sink - m))
        inv = 1.0 / den
        probs.append(((ec * inv).astype(BF16), (en * inv).astype(BF16)))
    outs_t = []
    for s in seqs:
        pc, pn = probs[s]
        outs_t.append(jnp.dot(cvt_ref[s].astype(BF16), pc, preferred_element_type=F32)
                      + _tn_dot(news[s][1], pn))
    for s in seqs:
        o = outs_t[s].T
        outs = []
        for g in range(GQA_GROUP):
            acc = jnp.zeros((t_new, KV_WIDTH), F32)
            for h in range(N_KV_HEADS):
                blk = g * N_KV_HEADS + h
                acc = jnp.where(lane_head == h, o[blk * t_new:(blk + 1) * t_new, :], acc)
            outs.append(acc)
        att = jnp.concatenate(outs, axis=1)
        o_ref[s] = (att * gate_ref[s].astype(F32)).astype(BF16)

    new_rows = kvn_ref[...].reshape(n_seq * t_new, 2 * KV_WIDTH)
    hi = new_rows.astype(BF16)
    lo = (new_rows - hi.astype(F32)).astype(BF16)
    place = place_ref[...]
    is_new = lax.broadcasted_iota(jnp.int32, (1, W), 1) >= W - t_new
    for src_ref, dst_ref, cols in ((ckt_ref, kot_ref, slice(0, KV_WIDTH)), (cvt_ref, vot_ref, slice(KV_WIDTH, None))):
        placed = _tn_dot(hi[:, cols], place) + _tn_dot(lo[:, cols], place)
        for s in seqs:
            dst_ref[s] = jnp.where(is_new, placed[:, s * W:(s + 1) * W], pltpu.roll(src_ref[s], W - t_new, axis=1))


def _cumsum_rows8(g):
    row = lax.broadcasted_iota(jnp.int32, g.shape, 0)
    out = g
    for sh in (1, 2, 4):
        out = out + jnp.where(row >= sh, pltpu.roll(out, sh, axis=0), 0.0)
    return out


def _gla_sample_kernel(qg_ref, kg_ref, vg_ref, gate_ref, lf_ref, gn_ref, st_ref, o_ref, so_ref):
    T = qg_ref.shape[1]
    gn = gn_ref[...]
    t_col = lax.broadcasted_iota(jnp.int32, (T, 1), 0)
    zpad = jnp.zeros((8, GLA_DK), F32)
    seqs = range(qg_ref.shape[0])
    pairs = [(s, h) for s in seqs for h in range(GLA_HEADS)]
    zpad_v = jnp.zeros((8, GLA_DV), F32)

    def ksl(h):
        return slice(h * GLA_DK, (h + 1) * GLA_DK)

    def vsl(h):
        return slice(h * GLA_DV, (h + 1) * GLA_DV)

    qts, kts, b_lasts, vs_, cols = [], [], [], [], {}
    for s in seqs:
        q = qg_ref[s].astype(F32)
        k = kg_ref[s].astype(F32)
        bc = _cumsum_rows8(lf_ref[s])
        b_last = bc[T - 1:T, :]
        b_lasts.append(b_last)
        qts.append(q * jnp.exp(bc))
        kts.append(k * jnp.exp(b_last - bc))
        vs_.append(vg_ref[s].astype(F32))
        for j in range(T):
            bs = jnp.broadcast_to(bc[j:j + 1, :], bc.shape)
            ks_ = jnp.broadcast_to(k[j:j + 1, :], k.shape)
            p = q * jnp.exp(jnp.minimum(bc - bs, 0.0)) * ks_
            for h in range(GLA_HEADS):
                col = jnp.sum(p[:, ksl(h)], axis=-1, keepdims=True)
                cols[s, h, j] = jnp.where(t_col >= j, col, 0.0)

    inters = {}
    for s, h in pairs:
        qt16 = jnp.concatenate([qts[s][:, ksl(h)], zpad], axis=0).astype(BF16)
        inters[s, h] = jnp.dot(qt16, st_ref[s, h].astype(BF16), preferred_element_type=F32)[:T, :]
    for s, h in pairs:
        kt16 = jnp.concatenate([kts[s][:, ksl(h)], zpad], axis=0).astype(BF16)
        v16 = jnp.concatenate([vs_[s][:, vsl(h)], zpad_v], axis=0).astype(BF16)
        so_ref[s, h] = (_decay_column(jnp.exp(b_lasts[s][:, ksl(h)])) * st_ref[s, h]
                        + _tn_dot(kt16, v16))

    for s, h in pairs:
        o = inters[s, h]
        v = vs_[s][:, vsl(h)]
        for j in range(T):
            o = o + cols[s, h, j] * v[j:j + 1, :]
        o_ref[s, :, vsl(h)] = _gla_finish(o, gn, gate_ref[s, :, vsl(h)].astype(F32))


SAMPLE_SEQS = 8


def _sample_mixers_kernel(*refs, t_new):
    swa_in, gla_in = refs[:7], refs[7:14]
    swa_out, gla_out = refs[14:17], refs[17:19]
    scratch = refs[19:]
    _swa_sample_kernel(*swa_in, *swa_out, *scratch, t_new=t_new)
    _gla_sample_kernel(*gla_in, *gla_out)


def _sample_mixers(p3, kv3, lf3, cache_k, cache_v, state, rel_bias, sinks, gla_norm_row, seq0):
    t_new = p3.shape[1]
    db, _, w = cache_k.shape
    sb = SAMPLE_SEQS
    assert t_new == 8 and w >= t_new and w <= WINDOW and seq0 % sb == 0 and db % sb == 0
    blk0 = seq0 // sb
    rows = N_HEADS * t_new

    def tok(width, col):
        return pl.BlockSpec((sb, t_new, width), lambda i: (blk0 + i, 0, col // width))

    def per_seq(*shape):
        return pl.BlockSpec((sb,) + shape, lambda i: (i,) + (0,) * len(shape))

    state_bytes = sb * GLA_HEADS * GLA_DK * GLA_DV * 4
    window_bytes = sb * KV_WIDTH * w * 4
    vmem_bytes = (2 * 2 * state_bytes + 2 * 4 * window_bytes + 2 * 2 * sb * t_new * N_MAIN * 2
                  + 2 * state_bytes + 6 * window_bytes)
    return pl.pallas_call(
        functools.partial(_sample_mixers_kernel, t_new=t_new),
        grid=(db // sb,),
        in_specs=[
            pl.BlockSpec(memory_space=pltpu.SMEM),
            pl.BlockSpec(memory_space=pltpu.SMEM),
            tok(ATT_WIDTH, COL_QA), tok(ATT_WIDTH, COL_GATE_A), tok(2 * KV_WIDTH, 0),
            per_seq(KV_WIDTH, w), per_seq(KV_WIDTH, w),
            tok(GLA_KEY_WIDTH, COL_QG), tok(GLA_KEY_WIDTH, COL_KG), tok(GLA_WIDTH, COL_VG), tok(GLA_WIDTH, COL_GATE_G),
            tok(GLA_KEY_WIDTH, 0),
            pl.BlockSpec((1, GLA_DV), lambda i: (0, 0)),
            per_seq(GLA_HEADS, GLA_DK, GLA_DV),
        ],
        out_specs=[
            per_seq(t_new, ATT_WIDTH), per_seq(KV_WIDTH, w), per_seq(KV_WIDTH, w),
            per_seq(t_new, GLA_WIDTH), per_seq(GLA_HEADS, GLA_DK, GLA_DV),
        ],
        out_shape=[
            jax.ShapeDtypeStruct((db, t_new, ATT_WIDTH), BF16),
            jax.ShapeDtypeStruct((db, KV_WIDTH, w), F32),
            jax.ShapeDtypeStruct((db, KV_WIDTH, w), F32),
            jax.ShapeDtypeStruct((db, t_new, GLA_WIDTH), BF16),
            jax.ShapeDtypeStruct((db, GLA_HEADS, GLA_DK, GLA_DV), F32),
        ],
        scratch_shapes=[
            pltpu.VMEM((w, rows), F32),
            pltpu.VMEM((2 * t_new, rows), F32),
            pltpu.VMEM((1, rows), F32),
            pltpu.VMEM((sb * t_new, sb * w), BF16),
        ],
        compiler_params=pltpu.CompilerParams(
            dimension_semantics=("arbitrary",), vmem_limit_bytes=vmem_bytes),
        name="sample_mixers",
    )(rel_bias, sinks, p3, p3, kv3, cache_k, cache_v, p3, p3, p3, p3, lf3, gla_norm_row, state)


def _window_feature_major(win):
    db, w = win.shape[:2]
    return win.transpose(0, 2, 3, 1).reshape(db, KV_WIDTH, w)


def _window_from_feature_major(win_t):
    db, _, w = win_t.shape
    return win_t.reshape(db, N_KV_HEADS, HEAD_DIM, w).transpose(0, 3, 1, 2)


def kernel(x_prompt, x_sample, cache_k_win, cache_v_win, state_gla, meta_tokens, rel_bias,
           norm_pre, norm_post, w_in, w_a2, b_a, attn_sinks, gla_norm, w_out):
    n_batch, seq, _ = x_prompt.shape
    db, t_new, _ = x_sample.shape
    depth = w_in.shape[0]
    assert depth == 1
    l = 0
    lp = seq + WINDOW
    n_blocks = lp // WINDOW
    n_chunks = lp // GLA_CHUNK
    w_cache = cache_k_win.shape[2]

    w_in_t = w_in[l].T
    assert sum(IN_SIZES[:-1]) == N_MAIN
    w_a2p = jnp.pad(w_a2[l], ((0, LANES - GLA_RANK), (0, 0))).astype(BF16)
    npre = norm_pre[l][None, :]
    npost = norm_post[l][None, :]
    ba = b_a[l][None, :]
    gn = gla_norm[l][None, :]
    sinks = attn_sinks[l]

    tm = IN_PROJ_TM
    x_meta = jnp.concatenate([jnp.zeros((tm - N_META, D_MODEL), x_prompt.dtype),
                              meta_tokens.astype(x_prompt.dtype)], axis=0)
    p_all, kv_all, lf_all = _in_proj(
        x_meta, x_prompt.reshape(n_batch * seq, D_MODEL), x_sample.reshape(db * t_new, D_MODEL),
        w_in_t, npre, w_a2p, ba)
    meta_blk = tm // WINDOW - 1
    sample_seq0 = (tm + n_batch * seq) // t_new

    mixed_a = _swa_prompt(p_all, rel_bias, sinks, n_batch, n_blocks, meta_blk)
    mixed_g, s_fin = _gla_prompt(p_all, lf_all, gn, n_batch, n_chunks, meta_blk)
    kv_p = jnp.stack([kv_all[tm + (b + 1) * seq - WINDOW:tm + (b + 1) * seq] for b in range(n_batch)])
    k_win_p = kv_p[..., :KV_WIDTH].reshape(1, n_batch, WINDOW, N_KV_HEADS, HEAD_DIM)
    v_win_p = kv_p[..., KV_WIDTH:].reshape(1, n_batch, WINDOW, N_KV_HEADS, HEAD_DIM)
    gla_p = s_fin[None]

    ps3 = p_all.reshape(-1, t_new, N_MAIN)
    mixed_as, k_win_s, v_win_s, mixed_gs, s_new = _sample_mixers(
        ps3, kv_all.reshape(-1, t_new, 2 * KV_WIDTH), lf_all.reshape(-1, t_new, GLA_KEY_WIDTH),
        _window_feature_major(cache_k_win[l]), _window_feature_major(cache_v_win[l]), state_gla[l],
        rel_bias, sinks, gn, sample_seq0)

    y_prompt, y_sample = _out_proj(
        mixed_a, mixed_g.reshape(n_batch * seq, GLA_WIDTH), x_prompt.reshape(n_batch * seq, D_MODEL),
        mixed_as.reshape(db * t_new, ATT_WIDTH), mixed_gs.reshape(db * t_new, GLA_WIDTH),
        x_sample.reshape(db * t_new, D_MODEL), w_out[l], npost)
    y_prompt = y_prompt.reshape(n_batch, seq, D_MODEL)
    y_sample = y_sample.reshape(db, t_new, D_MODEL)
    k_win_s = _window_from_feature_major(k_win_s)[None]
    v_win_s = _window_from_feature_major(v_win_s)[None]

    return (y_prompt, y_sample, k_win_p, v_win_p, gla_p, k_win_s, v_win_s, s_new[None])
```
